```python
import jax, jax.numpy as jnp
from jax import lax
import numpy as np

D_MODEL = 1024
BATCH = 8
SEQ = 2048
DEPTH = 2

ML_HEADS = 4
ML_HEAD_DIM = 64
ML_WIDTH = ML_HEADS * ML_HEAD_DIM
ML_CHUNK = 64
LRU_WIDTH = 256
LRU_BLOCKS = 4
LRU_BLOCK_DIM = LRU_WIDTH // LRU_BLOCKS
CONV_WIDTH = 4
LRU_C = 8.0
MLA_HEADS = 8
MLA_NOPE = 64
MLA_ROPE = 32
MLA_V = 64
MLA_WIDTH = MLA_HEADS * MLA_V
Q_LORA = 256
KV_LORA = 128
ROPE_THETA = 10000.0
Q_BLOCK = 128
MIX_WIDTH = ML_WIDTH + LRU_WIDTH + MLA_WIDTH
N_GROUPS = 4
EXPERTS_PER_GROUP = 8
N_EXPERTS = N_GROUPS * EXPERTS_PER_GROUP
TOP_K = 2
D_EXPERT = 256
EPS = 1e-6

SPLIT_SIZES = (ML_WIDTH, ML_WIDTH, ML_WIDTH, ML_WIDTH, ML_HEADS, ML_HEADS,
               LRU_WIDTH, LRU_WIDTH,
               Q_LORA, KV_LORA, MLA_ROPE)
N_IN = int(sum(SPLIT_SIZES))
SPLIT_POINTS = [int(v) for v in np.cumsum(SPLIT_SIZES)[:-1]]

kernel_name = "hybrid_mlstm_rglru_mla_hmoe"


def rmsnorm(x, g):
    xf = x.astype(jnp.float32)
    y = xf * lax.rsqrt(jnp.mean(xf * xf, axis=-1, keepdims=True) + EPS)
    return (y * g.astype(jnp.float32)).astype(x.dtype)


def mlstm(q, k, v, i_pre, f_pre, o_pre, norm_g):
    B, S, _ = q.shape
    H, dh, L = ML_HEADS, ML_HEAD_DIM, ML_CHUNK
    nc = S // L
    f32 = jnp.float32

    def heads(t):
        return t.astype(f32).reshape(B, nc, L, H, dh).transpose(1, 0, 3, 2, 4)

    def gates(t):
        return t.astype(f32).reshape(B, nc, L, H).transpose(1, 0, 3, 2)

    qh, kh, vh = heads(q), heads(k) * (dh ** -0.5), heads(v)
    log_i = gates(i_pre)
    log_f = jax.nn.log_sigmoid(gates(f_pre))
    causal = jnp.tril(jnp.ones((L, L), dtype=bool))

    def step(carry, inp):
        C, n, m = carry
        qc, kc, vc, li, lf = inp
        b = jnp.cumsum(lf, axis=-1)
        D = b[..., :, None] - b[..., None, :] + li[..., None, :]
        D = jnp.where(causal, D, -jnp.inf)
        inter = b + m[..., None]
        m_row = jnp.maximum(jnp.max(D, axis=-1), inter)
        w_intra = jnp.exp(D - m_row[..., None])
        w_inter = jnp.exp(inter - m_row)
        s = jnp.einsum('bhjd,bhsd->bhjs', qc, kc) * w_intra
        num = jnp.einsum('bhjs,bhsd->bhjd', s, vc) + w_inter[..., None] * jnp.einsum('bhvk,bhjk->bhjv', C, qc)
        den = jnp.sum(s, axis=-1) + w_inter * jnp.einsum('bhk,bhjk->bhj', n, qc)
        h = num / jnp.maximum(jnp.abs(den), jnp.exp(-m_row))[..., None]
        bL = b[..., -1]
        g = bL[..., None] - b + li
        m_new = jnp.maximum(bL + m, jnp.max(g, axis=-1))
        ws = jnp.exp(g - m_new[..., None])
        decay = jnp.exp(bL + m - m_new)
        C_new = decay[..., None, None] * C + jnp.einsum('bhs,bhsv,bhsk->bhvk', ws, vc, kc)
        n_new = decay[..., None] * n + jnp.einsum('bhs,bhsk->bhk', ws, kc)
        return (C_new, n_new, m_new), h

    init = (jnp.zeros((B, H, dh, dh), f32), jnp.zeros((B, H, dh), f32), jnp.zeros((B, H), f32))
    _, h = lax.scan(step, init, (qh, kh, vh, log_i, log_f))
    h = h.transpose(1, 0, 3, 2, 4).reshape(B, S, H, dh)
    o = jax.nn.sigmoid(o_pre.astype(f32)).reshape(B, S, H, dh)
    y = rmsnorm(o * h, norm_g.reshape(H, dh))
    return y.reshape(B, S, ML_WIDTH).astype(q.dtype)


def rglru(xb, gate, conv_w, conv_b, w_r, b_r, w_i, b_i, lam):
    B, S, W = xb.shape
    f32 = jnp.float32
    xc = lax.conv_general_dilated(xb, conv_w[:, None, :], window_strides=(1,),
                                  padding=[(CONV_WIDTH - 1, 0)],
                                  dimension_numbers=('NWC', 'WIO', 'NWC'),
                                  feature_group_count=W) + conv_b
    xc = xc.astype(f32)
    xblk = xc.reshape(B, S, LRU_BLOCKS, LRU_BLOCK_DIM)
    r = jax.nn.sigmoid(jnp.einsum('bsnd,nde->bsne', xblk, w_r.astype(f32)).reshape(B, S, W) + b_r.astype(f32))
    i = jax.nn.sigmoid(jnp.einsum('bsnd,nde->bsne', xblk, w_i.astype(f32)).reshape(B, S, W) + b_i.astype(f32))
    log_a = -LRU_C * r * jax.nn.softplus(-lam.astype(f32))
    a = jnp.exp(log_a)
    u = jnp.sqrt(-jnp.expm1(2.0 * log_a)) * (i * xc)

    def combine(lhs, rhs):
        a1, b1 = lhs
        a2, b2 = rhs
        return a1 * a2, a2 * b1 + b2

    _, h = lax.associative_scan(combine, (a, u), axis=1)
    return (h * jax.nn.gelu(gate.astype(f32))).astype(xb.dtype)


def rope_tables(S):
    inv = 1.0 / (ROPE_THETA ** (jnp.arange(0, MLA_ROPE, 2, dtype=jnp.float32) / MLA_ROPE))
    ang = jnp.arange(S, dtype=jnp.float32)[:, None] * inv[None, :]
    return jnp.cos(ang), jnp.sin(ang)


def apply_rope(t, cos, sin):
    t1, t2 = jnp.split(t.astype(jnp.float32), 2, axis=-1)
    return jnp.concatenate([t1 * cos - t2 * sin, t1 * sin + t2 * cos], axis=-1).astype(t.dtype)


def mla(cq, ckv, kr, q_g, w_uq, kv_g, w_ukv, cos, sin):
    B, S, _ = cq.shape
    H = MLA_HEADS
    q = (rmsnorm(cq, q_g) @ w_uq).reshape(B, S, H, MLA_NOPE + MLA_ROPE)
    q_nope = q[..., :MLA_NOPE]
    q_rope = apply_rope(q[..., MLA_NOPE:], cos[:, None, :], sin[:, None, :])
    kv = (rmsnorm(ckv, kv_g) @ w_ukv).reshape(B, S, H, MLA_NOPE + MLA_V)
    k_nope, v = kv[..., :MLA_NOPE], kv[..., MLA_NOPE:]
    k_rope = apply_rope(kr, cos, sin)
    scale = (MLA_NOPE + MLA_ROPE) ** -0.5
    nb = S // Q_BLOCK
    qn_b = q_nope.reshape(B, nb, Q_BLOCK, H, MLA_NOPE).transpose(1, 0, 2, 3, 4)
    qr_b = q_rope.reshape(B, nb, Q_BLOCK, H, MLA_ROPE).transpose(1, 0, 2, 3, 4)
    key_pos = jnp.arange(S)

    def block(args):
        qn, qr, idx = args
        s = (jnp.einsum('bqhd,bkhd->bhqk', qn, k_nope) +
             jnp.einsum('bqhr,bkr->bhqk', qr, k_rope)).astype(jnp.float32) * scale
        q_pos = idx * Q_BLOCK + jnp.arange(Q_BLOCK)
        s = jnp.where(key_pos[None, :] <= q_pos[:, None], s, -jnp.inf)
        p = jax.nn.softmax(s, axis=-1).astype(v.dtype)
        return jnp.einsum('bhqk,bkhd->bqhd', p, v)

    o = lax.map(block, (qn_b, qr_b, jnp.arange(nb)))
    return o.transpose(1, 0, 2, 3, 4).reshape(B, S, MLA_WIDTH)


def hier_moe(x, w_grp, b_grp, w_exp, b_exp, w_gate, w_up, w_down):
    B, S, D = x.shape
    T = B * S
    t = x.reshape(T, D)
    grp_p = jax.nn.softmax((t @ w_grp + b_grp).astype(jnp.float32), axis=-1)
    g_val, g_idx = lax.top_k(grp_p, 1)
    exp_logits = (t @ w_exp + b_exp).astype(jnp.float32).reshape(T, N_GROUPS, EXPERTS_PER_GROUP)
    sel = jnp.take_along_axis(exp_logits, g_idx[:, :, None], axis=1)[:, 0]
    e_val, e_idx = lax.top_k(jax.nn.softmax(sel, axis=-1), TOP_K)
    e_w = e_val / jnp.sum(e_val, axis=-1, keepdims=True) * g_val
    expert_id = g_idx * EXPERTS_PER_GROUP + e_idx
    combine = jnp.sum(jax.nn.one_hot(expert_id, N_EXPERTS, dtype=jnp.float32) * e_w[..., None], axis=1)

    def expert_step(acc, ew):
        wg, wu, wd, c = ew
        h = jax.nn.silu(t @ wg) * (t @ wu)
        return acc + c[:, None].astype(t.dtype) * (h @ wd), None

    y, _ = lax.scan(expert_step, jnp.zeros_like(t), (w_gate, w_up, w_down, combine.T))
    return y.reshape(B, S, D)


def setup_inputs(seed: int = 0) -> dict:
    key = jax.random.key(seed)
    ks = iter(jax.random.split(key, 40))
    f32 = jnp.float32
    L = DEPTH

    def nrm(shape, scale):
        return scale * jax.random.normal(next(ks), shape, f32)

    def gain(shape):
        return 1.0 + 0.02 * jax.random.normal(next(ks), shape, f32)

    x = jax.random.normal(next(ks), (BATCH, SEQ, D_MODEL), f32)
    norm1_g = gain((L, D_MODEL))
    w_in = nrm((L, D_MODEL, N_IN), D_MODEL ** -0.5)
    ml_i_bias = nrm((L, ML_HEADS), 0.1)
    ml_f_bias = jnp.linspace(3.0, 6.0, ML_HEADS, dtype=f32)[None, :] + nrm((L, ML_HEADS), 0.1)
    ml_norm_g = gain((L, ML_WIDTH))
    conv_w = nrm((L, CONV_WIDTH, LRU_WIDTH), CONV_WIDTH ** -0.5)
    conv_b = nrm((L, LRU_WIDTH), 0.02)
    lru_w_r = nrm((L, LRU_BLOCKS, LRU_BLOCK_DIM, LRU_BLOCK_DIM), LRU_BLOCK_DIM ** -0.5)
    lru_b_r = nrm((L, LRU_WIDTH), 0.02)
    lru_w_i = nrm((L, LRU_BLOCKS, LRU_BLOCK_DIM, LRU_BLOCK_DIM), LRU_BLOCK_DIM ** -0.5)
    lru_b_i = nrm((L, LRU_WIDTH), 0.02)
    a_c = jax.random.uniform(next(ks), (L, LRU_WIDTH), f32, 0.9, 0.999)
    a_base = a_c ** (1.0 / LRU_C)
    lru_lambda = jnp.log(a_base) - jnp.log1p(-a_base)
    lru_norm_g = gain((L, LRU_WIDTH))
    q_norm_g = gain((L, Q_LORA))
    w_uq = nrm((L, Q_LORA, MLA_HEADS * (MLA_NOPE + MLA_ROPE)), Q_LORA ** -0.5)
    kv_norm_g = gain((L, KV_LORA))
    w_ukv = nrm((L, KV_LORA, MLA_HEADS * (MLA_NOPE + MLA_V)), KV_LORA ** -0.5)
    mla_norm_g = gain((L, MLA_WIDTH))
    w_out = nrm((L, MIX_WIDTH, D_MODEL), MIX_WIDTH ** -0.5)
    norm2_g = gain((L, D_MODEL))
    w_group = nrm((L, D_MODEL, N_GROUPS), D_MODEL ** -0.5)
    b_group = nrm((L, N_GROUPS), 0.01)
    w_expert = nrm((L, D_MODEL, N_EXPERTS), D_MODEL ** -0.5)
    b_expert = nrm((L, N_EXPERTS), 0.01)
    w_gate = nrm((L, N_EXPERTS, D_MODEL, D_EXPERT), D_MODEL ** -0.5)
    w_up = nrm((L, N_EXPERTS, D_MODEL, D_EXPERT), D_MODEL ** -0.5)
    w_down = nrm((L, N_EXPERTS, D_EXPERT, D_MODEL), D_EXPERT ** -0.5)
    final_norm_g = gain((D_MODEL,))
    return {"x": x, "norm1_g": norm1_g, "w_in": w_in, "ml_i_bias": ml_i_bias, "ml_f_bias": ml_f_bias,
            "ml_norm_g": ml_norm_g, "conv_w": conv_w, "conv_b": conv_b, "lru_w_r": lru_w_r,
            "lru_b_r": lru_b_r, "lru_w_i": lru_w_i, "lru_b_i": lru_b_i, "lru_lambda": lru_lambda,
            "lru_norm_g": lru_norm_g, "q_norm_g": q_norm_g, "w_uq": w_uq, "kv_norm_g": kv_norm_g,
            "w_ukv": w_ukv, "mla_norm_g": mla_norm_g, "w_out": w_out, "norm2_g": norm2_g,
            "w_group": w_group, "b_group": b_group, "w_expert": w_expert, "b_expert": b_expert,
            "w_gate": w_gate, "w_up": w_up, "w_down": w_down, "final_norm_g": final_norm_g}


def reference(x, norm1_g, w_in, ml_i_bias, ml_f_bias, ml_norm_g, conv_w, conv_b, lru_w_r, lru_b_r,
              lru_w_i, lru_b_i, lru_lambda, lru_norm_g, q_norm_g, w_uq, kv_norm_g, w_ukv, mla_norm_g,
              w_out, norm2_g, w_group, b_group, w_expert, b_expert, w_gate, w_up, w_down, final_norm_g):
    S = x.shape[1]
    cos, sin = rope_tables(S)
    for l in range(DEPTH):
        hn = rmsnorm(x, norm1_g[l])
        proj = hn @ w_in[l]
        mq, mk, mv, mo, mi, mf, rx, rg, cq, ckv, kr = jnp.split(proj, SPLIT_POINTS, axis=-1)
        y_m = mlstm(mq, mk, mv, mi + ml_i_bias[l], mf + ml_f_bias[l], mo, ml_norm_g[l])
        y_r = rmsnorm(rglru(rx, rg, conv_w[l], conv_b[l], lru_w_r[l], lru_b_r[l],
                            lru_w_i[l], lru_b_i[l], lru_lambda[l]), lru_norm_g[l])
        y_a = rmsnorm(mla(cq, ckv, kr, q_norm_g[l], w_uq[l], kv_norm_g[l], w_ukv[l], cos, sin),
                      mla_norm_g[l])
        x = x + jnp.concatenate([y_m, y_r, y_a], axis=-1) @ w_out[l]
        x = x + hier_moe(rmsnorm(x, norm2_g[l]), w_group[l], b_group[l], w_expert[l], b_expert[l],
                         w_gate[l], w_up[l], w_down[l])
    return rmsnorm(x, final_norm_g)
```

```python
import functools

import numpy as np
import jax
import jax.numpy as jnp
from jax import lax
from jax.experimental import pallas as pl
from jax.experimental.pallas import tpu as pltpu

F32 = jnp.float32
BF16 = jnp.bfloat16

D_MODEL = 1024
ML_HEADS = 4
ML_DH = 64
ML_W = ML_HEADS * ML_DH
LRU_W = 256
LRU_BLOCKS = 4
CONV_W = 4
LRU_C = 8.0
MLA_H = 8
NOPE = 64
ROPE = 32
MLA_V = 64
MLA_W = MLA_H * MLA_V
Q_LORA = 256
KV_LORA = 128
ROPE_THETA = 10000.0
N_GROUPS = 4
EPG = 8
N_EXP = N_GROUPS * EPG
D_EXP = 256
EPS = 1e-6

LANES = 128
SUBLANES = 8
TOK_ROWS = D_MODEL // LANES
PROJ_W = 2048
SMALL_COL = 1920
VMEM_LIMIT = 56 * 1024 * 1024

ML_CHUNK = 128
LRU_TC = 256
TM_IN = 512
TQ = 256
ST = 1024
TM_E = 256


def _cp(sem):
    return pltpu.CompilerParams(dimension_semantics=sem, vmem_limit_bytes=VMEM_LIMIT)


def _rms(x, g):
    return x * lax.rsqrt(jnp.mean(x * x, axis=-1, keepdims=True) + EPS) * g


def _in_proj_kernel(x_ref, g_ref, w_ref, o_ref):
    y = _rms(x_ref[...], g_ref[...])
    o_ref[...] = jnp.dot(y.astype(BF16), w_ref[...], preferred_element_type=F32)


def in_proj(x2d, g, w):
    T = x2d.shape[0]
    return pl.pallas_call(
        _in_proj_kernel,
        grid=(T // TM_IN,),
        in_specs=[pl.BlockSpec((TM_IN, D_MODEL), lambda i: (i, 0)),
                  pl.BlockSpec((1, D_MODEL), lambda i: (0, 0)),
                  pl.BlockSpec((D_MODEL, PROJ_W), lambda i: (0, 0))],
        out_specs=pl.BlockSpec((TM_IN, PROJ_W), lambda i: (i, 0)),
        out_shape=jax.ShapeDtypeStruct((T, PROJ_W), F32),
        compiler_params=_cp(("arbitrary",)),
        name="in_proj",
    )(x2d, g, w)


def _mlstm_kernel(q_ref, k_ref, v_ref, o_ref, s_ref, bias_ref, g_ref, y_ref, c_ref, m_ref):
    L = q_ref.shape[0]
    c = pl.program_id(1)

    @pl.when(c == 0)
    def _():
        c_ref[...] = jnp.zeros_like(c_ref)
        m_ref[...] = jnp.zeros_like(m_ref)

    pre = s_ref[...] + bias_ref[...]
    lane = lax.broadcasted_iota(jnp.int32, (L, LANES), 1)
    logsig = jnp.minimum(pre, 0.0) - jnp.log1p(jnp.exp(-jnp.abs(pre)))
    gates = jnp.where(lane < ML_HEADS, pre, logsig)
    r_i = lax.broadcasted_iota(jnp.int32, (L, L), 0)
    c_i = lax.broadcasted_iota(jnp.int32, (L, L), 1)
    causal = c_i <= r_i
    cum = jnp.dot(causal.astype(F32), gates, precision=lax.Precision.HIGHEST,
                  preferred_element_type=F32)
    gates_t = gates.T
    cum_t = cum.T
    v_t = v_ref[...].T
    q = q_ref[...]
    k = k_ref[...] * (ML_DH ** -0.5)
    v = v_ref[...]
    o = jax.nn.sigmoid(o_ref[...])
    ones = jnp.ones((SUBLANES, L), F32)

    for h in range(ML_HEADS):
        sl = slice(h * ML_DH, (h + 1) * ML_DH)
        b_col = cum[:, ML_HEADS + h:ML_HEADS + h + 1]
        b_row = cum_t[ML_HEADS + h:ML_HEADS + h + 1, :]
        li_row = gates_t[h:h + 1, :]
        m_prev = m_ref[:, h:h + 1]
        dmat = jnp.where(causal, b_col - b_row + li_row, -jnp.inf)
        inter = b_col + m_prev
        m_row = jnp.maximum(jnp.max(dmat, axis=1, keepdims=True), inter)
        w_intra = jnp.exp(dmat - m_row)
        w_inter = jnp.exp(inter - m_row)
        qh = q[:, sl].astype(BF16)
        kh = k[:, sl].astype(BF16)
        vh = v[:, sl].astype(BF16)
        s = lax.dot_general(qh, kh, (((1,), (1,)), ((), ())), preferred_element_type=F32) * w_intra
        state = c_ref[h]
        qc = lax.dot_general(qh, state.astype(BF16), (((1,), (1,)), ((), ())),
                             preferred_element_type=F32)
        num = jnp.dot(s.astype(BF16), vh, preferred_element_type=F32) + w_inter * qc[:, :ML_DH]
        den = jnp.sum(s, axis=1, keepdims=True) + w_inter * qc[:, ML_DH:ML_DH + 1]
        hh = num / jnp.maximum(jnp.abs(den), jnp.exp(-m_row))
        b_last = b_row[:, L - 1:L]
        g_row = b_last - b_row + li_row
        m_new = jnp.maximum(b_last + m_prev, jnp.max(g_row, axis=1, keepdims=True))
        ws_row = jnp.exp(g_row - m_new)
        decay = jnp.exp(b_last + m_prev - m_new)
        vt_ext = jnp.concatenate([v_t[sl, :], ones], axis=0) * ws_row
        c_ref[h] = decay * state + jnp.dot(vt_ext.astype(BF16), kh, preferred_element_type=F32)
        m_ref[:, h:h + 1] = m_new
        y = o[:, sl] * hh
        y_ref[:, sl] = y * lax.rsqrt(jnp.mean(y * y, axis=1, keepdims=True) + EPS) * g_ref[:, sl]


def mlstm(proj, bias, g, B, S):
    T = B * S
    L = ML_CHUNK
    nc = S // L
    col = lambda j: pl.BlockSpec((L, ML_W), lambda b, c, j=j: (b * nc + c, j))
    return pl.pallas_call(
        _mlstm_kernel,
        grid=(B, nc),
        in_specs=[col(0), col(1), col(2), col(3),
                  pl.BlockSpec((L, LANES), lambda b, c: (b * nc + c, SMALL_COL // LANES)),
                  pl.BlockSpec((1, LANES), lambda b, c: (0, 0)),
                  pl.BlockSpec((1, ML_W), lambda b, c: (0, 0))],
        out_specs=pl.BlockSpec((L, ML_W), lambda b, c: (b * nc + c, 0)),
        out_shape=jax.ShapeDtypeStruct((T, ML_W), F32),
        scratch_shapes=[pltpu.VMEM((ML_HEADS, ML_DH + SUBLANES, ML_DH), F32),
                        pltpu.VMEM((1, LANES), F32)],
        compiler_params=_cp(("arbitrary", "arbitrary")),
        name="mlstm",
    )(proj, proj, proj, proj, proj, bias, g)


def _rglru_kernel(x_ref, gate_ref, cw_ref, cb_ref, wg_ref, bg_ref, lam_ref, g_ref, y_ref,
                  xp_ref, a_ref, u_ref, h_ref, carry_ref):
    B, tc, W = x_ref.shape
    c = pl.program_id(0)

    @pl.when(c == 0)
    def _():
        xp_ref[:, 0:SUBLANES, :] = jnp.zeros((B, SUBLANES, W), F32)
        carry_ref[...] = jnp.zeros_like(carry_ref)

    xp_ref[:, SUBLANES:, :] = x_ref[...]
    xc = jnp.zeros((B, tc, W), F32) + cb_ref[...]
    for j in range(CONV_W):
        off = SUBLANES - (CONV_W - 1) + j
        xc = xc + cw_ref[j:j + 1, :] * xp_ref[:, off:off + tc, :]
    xp_ref[:, 0:SUBLANES, :] = x_ref[:, tc - SUBLANES:, :]

    xc2 = xc.reshape(B * tc, W)
    ri = jax.nn.sigmoid(jnp.dot(xc2.astype(BF16), wg_ref[...], preferred_element_type=F32) + bg_ref[...])
    r = ri[:, :W]
    ig = ri[:, W:]
    lam = lam_ref[...]
    softplus_neg = jnp.maximum(-lam, 0.0) + jnp.log1p(jnp.exp(-jnp.abs(lam)))
    log_a = -LRU_C * r * softplus_neg
    a = jnp.exp(log_a)
    u = jnp.sqrt(1.0 - jnp.exp(2.0 * log_a)) * (ig * xc2)
    halves = W // LANES
    for k in range(halves):
        a_ref[k] = a[:, k * LANES:(k + 1) * LANES]
        u_ref[k] = u[:, k * LANES:(k + 1) * LANES]

    def step(i, hs):
        base = pl.multiple_of(i * SUBLANES, SUBLANES)
        for j in range(SUBLANES):
            rows = pl.ds(base + j, B, stride=tc)
            hs = tuple(a_ref.at[k][rows, :] * hs[k] + u_ref.at[k][rows, :] for k in range(halves))
            for k in range(halves):
                h_ref.at[k][rows, :] = hs[k]
        return hs

    init = tuple(carry_ref[k] for k in range(halves))
    last = lax.fori_loop(0, tc // SUBLANES, step, init)
    for k in range(halves):
        carry_ref[k] = last[k]
    h_all = jnp.concatenate([h_ref[k] for k in range(halves)], axis=1)
    y = h_all.reshape(B, tc, W) * jax.nn.gelu(gate_ref[...])
    y_ref[...] = _rms(y, g_ref[...])


def rglru(proj3, cw, cb, wg, bg, lam, g):
    B, S, _ = proj3.shape
    tc = LRU_TC
    vec = lambda n: pl.BlockSpec((1, n), lambda c: (0, 0))
    return pl.pallas_call(
        _rglru_kernel,
        grid=(S // tc,),
        in_specs=[pl.BlockSpec((B, tc, LRU_W), lambda c: (0, c, 4)),
                  pl.BlockSpec((B, tc, LRU_W), lambda c: (0, c, 5)),
                  pl.BlockSpec((CONV_W, LRU_W), lambda c: (0, 0)),
                  vec(LRU_W),
                  pl.BlockSpec((LRU_W, 2 * LRU_W), lambda c: (0, 0)),
                  vec(2 * LRU_W), vec(LRU_W), vec(LRU_W)],
        out_specs=pl.BlockSpec((B, tc, LRU_W), lambda c: (0, c, 0)),
        out_shape=jax.ShapeDtypeStruct((B, S, LRU_W), F32),
        scratch_shapes=[pltpu.VMEM((B, tc + SUBLANES, LRU_W), F32),
                        pltpu.VMEM((LRU_W // LANES, B * tc, LANES), F32),
                        pltpu.VMEM((LRU_W // LANES, B * tc, LANES), F32),
                        pltpu.VMEM((LRU_W // LANES, B * tc, LANES), F32),
                        pltpu.VMEM((LRU_W // LANES, B, LANES), F32)],
        compiler_params=_cp(("arbitrary",)),
        name="rglru",
    )(proj3, proj3, cw, cb, wg, bg, lam, g)


def _rot_half(blk):
    lane = lax.broadcasted_iota(jnp.int32, blk.shape, 1)
    return jnp.where(lane < NOPE + ROPE // 2,
                     pltpu.roll(blk, LANES - ROPE // 2, 1), pltpu.roll(blk, ROPE // 2, 1))


def _mla_proj_kernel(cq_ref, ckv_ref, qg_ref, kvg_ref, wq_ref, wk_ref, wv_ref,
                     cq_t, sq_t, ck_t, sk_t, q_ref, k_ref, v_ref):
    qn = _rms(cq_ref[...], qg_ref[...]).astype(BF16)
    qf = jnp.dot(qn, wq_ref[...], preferred_element_type=F32)
    kvn = _rms(ckv_ref[:, :KV_LORA], kvg_ref[...]).astype(BF16)
    kf = jnp.dot(kvn, wk_ref[...], preferred_element_type=F32)
    v_ref[...] = jnp.dot(kvn, wv_ref[...], preferred_element_type=F32).astype(BF16)
    small = ckv_ref[:, KV_LORA:]
    kr = small * ck_t[...] + _rot_half(small) * sk_t[...]
    cq = cq_t[...]
    sq = sq_t[...]
    for h in range(MLA_H):
        sl = slice(h * LANES, (h + 1) * LANES)
        blk = qf[:, sl]
        q_ref[:, sl] = (blk * cq + _rot_half(blk) * sq).astype(BF16)
        k_ref[:, sl] = (kf[:, sl] + kr).astype(BF16)


def mla_proj(proj, qg, kvg, wq, wk, wv, tabs, S):
    T = proj.shape[0]
    tm = TM_IN
    npos = S // tm
    tab = pl.BlockSpec((tm, LANES), lambda i: (i % npos, 0))
    full = lambda a: pl.BlockSpec(a.shape, lambda i: (0, 0))
    return pl.pallas_call(
        _mla_proj_kernel,
        grid=(T // tm,),
        in_specs=[pl.BlockSpec((tm, Q_LORA), lambda i: (i, 6)),
                  pl.BlockSpec((tm, 2 * KV_LORA), lambda i: (i, 7)),
                  full(qg), full(kvg), full(wq), full(wk), full(wv), tab, tab, tab, tab],
        out_specs=[pl.BlockSpec((tm, MLA_H * LANES), lambda i: (i, 0)),
                   pl.BlockSpec((tm, MLA_H * LANES), lambda i: (i, 0)),
                   pl.BlockSpec((tm, MLA_W), lambda i: (i, 0))],
        out_shape=[jax.ShapeDtypeStruct((T, MLA_H * LANES), BF16),
                   jax.ShapeDtypeStruct((T, MLA_H * LANES), BF16),
                   jax.ShapeDtypeStruct((T, MLA_W), BF16)],
        compiler_params=_cp(("arbitrary",)),
        name="mla_proj",
    )(proj, proj, qg, kvg, wq, wk, wv, *tabs)


def _attn_kernel(q_ref, k_ref, v_ref, g_ref, o_ref):
    tq = q_ref.shape[0]
    i = pl.program_id(1)
    r_i = lax.broadcasted_iota(jnp.int32, (tq, tq), 0)
    c_i = lax.broadcasted_iota(jnp.int32, (tq, tq), 1)
    causal = c_i <= r_i
    lane = lax.broadcasted_iota(jnp.int32, (tq, LANES), 1)
    ssq = jnp.zeros((tq, 1), F32)
    for hp in range(MLA_H // 2):
        vsl = slice(hp * LANES, (hp + 1) * LANES)
        res = []
        for sub in range(2):
            hsl = slice((2 * hp + sub) * LANES, (2 * hp + sub + 1) * LANES)
            qh = q_ref[:, hsl]

            def block(off, carry, masked):
                m, l, acc = carry
                s = lax.dot_general(qh, k_ref[pl.ds(off, tq), hsl], (((1,), (1,)), ((), ())),
                                    preferred_element_type=F32)
                if masked:
                    s = jnp.where(causal, s, -jnp.inf)
                m_new = jnp.maximum(m, jnp.max(s, axis=1, keepdims=True))
                alpha = jnp.exp(m - m_new)
                p = jnp.exp(s - m_new)
                l = alpha * l + jnp.sum(p, axis=1, keepdims=True)
                acc = alpha * acc + jnp.dot(p.astype(BF16), v_ref[pl.ds(off, tq), vsl],
                                            preferred_element_type=F32)
                return m_new, l, acc

            init = (jnp.full((tq, 1), -jnp.inf, F32), jnp.zeros((tq, 1), F32),
                    jnp.zeros((tq, LANES), F32))
            carry = lax.fori_loop(
                0, i, lambda j, cr: block(pl.multiple_of(j * tq, tq), cr, False), init)
            _, l, acc = block(pl.multiple_of(i * tq, tq), carry, True)
            res.append(acc / l)
        pair = jnp.where(lane < MLA_V, res[0], res[1])
        o_ref[:, vsl] = pair
        ssq = ssq + jnp.sum(pair * pair, axis=1, keepdims=True)
    o_ref[...] = o_ref[...] * lax.rsqrt(ssq * (1.0 / MLA_W) + EPS) * g_ref[...]


def attention(q, k, v, g, B, S):
    T = B * S
    nq = S // TQ
    return pl.pallas_call(
        _attn_kernel,
        grid=(B, nq),
        in_specs=[pl.BlockSpec((TQ, MLA_H * LANES), lambda b, i: (b * nq + i, 0)),
                  pl.BlockSpec((S, MLA_H * LANES), lambda b, i: (b, 0)),
                  pl.BlockSpec((S, MLA_W), lambda b, i: (b, 0)),
                  pl.BlockSpec((1, MLA_W), lambda b, i: (0, 0))],
        out_specs=pl.BlockSpec((TQ, MLA_W), lambda b, i: (b * nq + i, 0)),
        out_shape=jax.ShapeDtypeStruct((T, MLA_W), F32),
        compiler_params=_cp(("arbitrary", "arbitrary")),
        name="attention",
    )(q, k, v, g)


def _out_proj_kernel(x_ref, ym_ref, yr_ref, ya_ref, w_ref, g_ref, wr_ref, br_ref,
                     xm_ref, t_ref, lg_ref):
    tm = x_ref.shape[0]
    acc = jnp.dot(ym_ref[...].astype(BF16), w_ref[0:ML_W, :], preferred_element_type=F32)
    acc += jnp.dot(yr_ref[...].astype(BF16), w_ref[ML_W:ML_W + LRU_W, :], preferred_element_type=F32)
    acc += jnp.dot(ya_ref[...].astype(BF16), w_ref[ML_W + LRU_W:, :], preferred_element_type=F32)
    xm = x_ref[...] + acc
    xm_ref[...] = xm
    t = _rms(xm, g_ref[...])
    lg_ref[...] = jnp.dot(t, wr_ref[...], precision=lax.Precision.HIGHEST,
                          preferred_element_type=F32) + br_ref[...]
    for s in range(TOK_ROWS):
        t_ref[pl.ds(s, tm, stride=TOK_ROWS), :] = t[:, s * LANES:(s + 1) * LANES]


def out_proj(x2d, ym, yr, ya, w, g, wr, br):
    T = x2d.shape[0]
    tm = TM_IN
    row = lambda n: pl.BlockSpec((tm, n), lambda i: (i, 0))
    full = lambda a: pl.BlockSpec(a.shape, lambda i: (0, 0))
    return pl.pallas_call(
        _out_proj_kernel,
        grid=(T // tm,),
        in_specs=[row(D_MODEL), row(ML_W), row(LRU_W), row(MLA_W), full(w), full(g), full(wr), full(br)],
        out_specs=[row(D_MODEL),
                   pl.BlockSpec((tm * TOK_ROWS, LANES), lambda i: (i, 0)),
                   row(LANES)],
        out_shape=[jax.ShapeDtypeStruct((T, D_MODEL), F32),
                   jax.ShapeDtypeStruct((T * TOK_ROWS, LANES), F32),
                   jax.ShapeDtypeStruct((T, LANES), F32)],
        compiler_params=_cp(("arbitrary",)),
        name="out_proj",
    )(x2d, ym, yr, ya, w, g, wr, br)


def _route_kernel(lg_ref, meta_ref, cnt_ref):
    st = lg_ref.shape[0]
    lg = lg_ref[...]
    lane = lax.broadcasted_iota(jnp.int32, (st, LANES), 1)
    row_max = lambda a: jnp.max(a, axis=1, keepdims=True)
    row_sum = lambda a: jnp.sum(a, axis=1, keepdims=True)
    first = lambda cond: jnp.min(jnp.where(cond, lane, LANES), axis=1, keepdims=True)

    gmask = lane < N_GROUPS
    gl = jnp.where(gmask, lg, -jnp.inf)
    ge = jnp.exp(gl - row_max(gl))
    gp = ge / row_sum(ge)
    g_val = row_max(gp)
    g_idx = first(gmask & (gp == g_val))

    smask = (lane >= N_GROUPS) & (lane < N_GROUPS + N_EXP) & (((lane - N_GROUPS) >> 3) == g_idx)
    el = jnp.where(smask, lg, -jnp.inf)
    ee = jnp.exp(el - row_max(el))
    ep = ee / row_sum(ee)
    v1 = row_max(ep)
    i1 = first(smask & (ep == v1))
    rest = smask & (lane != i1)
    v2 = row_max(jnp.where(rest, ep, -1.0))
    i2 = first(rest & (ep == v2))
    den = v1 + v2
    w1 = v1 / den * g_val
    w2 = v2 / den * g_val

    onehot = ((lane == i1) | (lane == i2)).astype(F32)
    sb = min(st, 256)
    r_i = lax.broadcasted_iota(jnp.int32, (sb, sb), 0)
    c_i = lax.broadcasted_iota(jnp.int32, (sb, sb), 1)
    tri = (c_i < r_i).astype(BF16)
    run = jnp.zeros((1, LANES), F32)
    ranks = []
    for j in range(st // sb):
        oh = onehot[j * sb:(j + 1) * sb]
        ranks.append(jnp.dot(tri, oh.astype(BF16), preferred_element_type=F32) + run)
        run = run + jnp.sum(oh, axis=0, keepdims=True)
    rank = jnp.concatenate(ranks, axis=0)
    u_r = lax.broadcasted_iota(jnp.int32, (LANES, LANES), 0)
    u_c = lax.broadcasted_iota(jnp.int32, (LANES, LANES), 1)
    lstart = jnp.dot(jnp.broadcast_to(run, (SUBLANES, LANES)), (u_r < u_c).astype(F32),
                     precision=lax.Precision.HIGHEST, preferred_element_type=F32)[0:1]
    slot = rank + lstart
    p1 = row_sum(jnp.where(lane == i1, slot, 0.0))
    p2 = row_sum(jnp.where(lane == i2, slot, 0.0))
    meta = jnp.where(lane == 0, p1, jnp.where(lane == 1, p2, jnp.where(lane == 2, w1,
                                                                      jnp.where(lane == 3, w2, 0.0))))
    meta_ref[...] = meta.T[0:SUBLANES, :]
    sub = lax.broadcasted_iota(jnp.int32, (SUBLANES, LANES), 0)
    cnt_ref[...] = jnp.where(sub == 0, run, jnp.where(sub == 1, lstart, 0.0))


def route(logits):
    T = logits.shape[0]
    n_st = T // ST
    return pl.pallas_call(
        _route_kernel,
        grid=(n_st,),
        in_specs=[pl.BlockSpec((ST, LANES), lambda i: (i, 0))],
        out_specs=[pl.BlockSpec((SUBLANES, ST), lambda i: (i, 0)),
                   pl.BlockSpec((SUBLANES, LANES), lambda i: (i, 0))],
        out_shape=[jax.ShapeDtypeStruct((n_st * SUBLANES, ST), F32),
                   jax.ShapeDtypeStruct((n_st * SUBLANES, LANES), F32)],
        compiler_params=_cp(("arbitrary",)),
        name="route",
    )(logits)


def _copy_by_bits(n, nbits, copy_at, start):
    off = jnp.int32(0)
    for bit in reversed(range(nbits)):
        sz = 1 << bit

        @pl.when((n & sz) != 0)
        def _(off=off, sz=sz):
            cp = copy_at(off, sz)
            if start:
                cp.start()
            else:
                cp.wait()

        off = off + (n & sz)


def _strip_copies(st, cnt_ref, lst_ref, gst_ref, make_copy, start):
    def body(e, carry):
        loc = lst_ref[st * N_EXP + e]
        glob = gst_ref[st * N_EXP + e]
        _copy_by_bits(
            cnt_ref[st * N_EXP + e], int(np.log2(ST)) + 1,
            lambda off, sz: make_copy(pl.multiple_of((loc + off) * TOK_ROWS, TOK_ROWS),
                                      pl.multiple_of((glob + off) * TOK_ROWS, TOK_ROWS), sz * TOK_ROWS),
            start)
        return carry

    lax.fori_loop(0, N_EXP, body, 0)


def _zero_padding(pad_ref, xs_ref, zero_ref, sem, start):
    tile_rows = TM_E * TOK_ROWS

    def body(e, carry):
        first = pad_ref[e]
        _copy_by_bits(
            pad_ref[N_EXP + e], int(np.log2(TM_E)),
            lambda off, sz: pltpu.make_async_copy(
                zero_ref.at[pl.ds(0, sz * TOK_ROWS)],
                xs_ref.at[pl.ds(pl.multiple_of((first + off) * TOK_ROWS, TOK_ROWS), sz * TOK_ROWS)], sem),
            start)
        return carry

    lax.fori_loop(0, N_EXP, body, 0)

    def tail(i, carry):
        cp = pltpu.make_async_copy(
            zero_ref, xs_ref.at[pl.ds(pl.multiple_of(i * tile_rows, tile_rows), tile_rows)], sem)
        if start:
            cp.start()
        else:
            cp.wait()
        return carry

    lax.fori_loop(pad_ref[2 * N_EXP], xs_ref.shape[0] // tile_rows, tail, 0)


def _dispatch_kernel(lpos_ref, cnt_ref, lst_ref, gst_ref, pad_ref, t_ref, xs_ref,
                     stage_ref, zero_ref, sem, zsem):
    st = pl.program_id(0)
    base = st * 2 * ST

    @pl.when(st == 0)
    def _():
        zero_ref[...] = jnp.zeros_like(zero_ref)
        _zero_padding(pad_ref, xs_ref, zero_ref, zsem, True)

    def scatter(i, carry):
        for u in range(SUBLANES):
            t = i * SUBLANES + u
            tile = t_ref[pl.ds(pl.multiple_of(t * TOK_ROWS, TOK_ROWS), TOK_ROWS), :]
            p0 = lpos_ref[base + t]
            p1 = lpos_ref[base + ST + t]
            stage_ref[pl.ds(pl.multiple_of(p0 * TOK_ROWS, TOK_ROWS), TOK_ROWS), :] = tile
            stage_ref[pl.ds(pl.multiple_of(p1 * TOK_ROWS, TOK_ROWS), TOK_ROWS), :] = tile
        return carry

    lax.fori_loop(0, ST // SUBLANES, scatter, 0)

    def make_copy(loc, glob, rows):
        return pltpu.make_async_copy(stage_ref.at[pl.ds(loc, rows)], xs_ref.at[pl.ds(glob, rows)], sem)

    _strip_copies(st, cnt_ref, lst_ref, gst_ref, make_copy, True)
    _strip_copies(st, cnt_ref, lst_ref, gst_ref, make_copy, False)

    @pl.when(st == 0)
    def _():
        _zero_padding(pad_ref, xs_ref, zero_ref, zsem, False)


def dispatch(lpos, cnt, lst, gst, pad, t_tiles, n_rows):
    T = t_tiles.shape[0] // TOK_ROWS
    return pl.pallas_call(
        _dispatch_kernel,
        grid_spec=pltpu.PrefetchScalarGridSpec(
            num_scalar_prefetch=5,
            grid=(T // ST,),
            in_specs=[pl.BlockSpec((ST * TOK_ROWS, LANES), lambda i, *_: (i, 0))],
            out_specs=pl.BlockSpec(memory_space=pl.ANY),
            scratch_shapes=[pltpu.VMEM((2 * ST * TOK_ROWS, LANES), F32),
                            pltpu.VMEM((TM_E * TOK_ROWS, LANES), F32),
                            pltpu.SemaphoreType.DMA(()),
                            pltpu.SemaphoreType.DMA(())]),
        out_shape=jax.ShapeDtypeStruct((n_rows * TOK_ROWS, LANES), F32),
        compiler_params=_cp(("arbitrary",)),
        name="dispatch",
    )(lpos, cnt, lst, gst, pad, t_tiles)


def _combine_kernel(lpos_ref, cnt_ref, lst_ref, gst_ref, w_ref, ys_ref, xm_ref, fg_ref, o_ref,
                    stage_ref, comb_ref, sem, *, final):
    st = pl.program_id(0)
    base = st * 2 * ST

    def make_copy(loc, glob, rows):
        return pltpu.make_async_copy(ys_ref.at[pl.ds(glob, rows)], stage_ref.at[pl.ds(loc, rows)], sem)

    _strip_copies(st, cnt_ref, lst_ref, gst_ref, make_copy, True)
    _strip_copies(st, cnt_ref, lst_ref, gst_ref, make_copy, False)

    def gather(i, carry):
        for u in range(SUBLANES):
            t = i * SUBLANES + u
            p0 = lpos_ref[base + t]
            p1 = lpos_ref[base + ST + t]
            y0 = stage_ref[pl.ds(pl.multiple_of(p0 * TOK_ROWS, TOK_ROWS), TOK_ROWS), :]
            y1 = stage_ref[pl.ds(pl.multiple_of(p1 * TOK_ROWS, TOK_ROWS), TOK_ROWS), :]
            comb_ref[pl.ds(pl.multiple_of(t * TOK_ROWS, TOK_ROWS), TOK_ROWS), :] = (
                w_ref[0, 0, t] * y0 + w_ref[0, 1, t] * y1)
        return carry

    lax.fori_loop(0, ST // SUBLANES, gather, 0)
    moe = jnp.concatenate([comb_ref[pl.ds(s, ST, stride=TOK_ROWS), :] for s in range(TOK_ROWS)], axis=1)
    x_new = xm_ref[...] + moe
    o_ref[...] = _rms(x_new, fg_ref[...]) if final else x_new


def combine(lpos, cnt, lst, gst, w, ys, xm, fg, final):
    T = xm.shape[0]
    return pl.pallas_call(
        functools.partial(_combine_kernel, final=final),
        grid_spec=pltpu.PrefetchScalarGridSpec(
            num_scalar_prefetch=4,
            grid=(T // ST,),
            in_specs=[pl.BlockSpec((1, 2, ST), lambda i, *_: (i, 0, 0), memory_space=pltpu.SMEM),
                      pl.BlockSpec(memory_space=pl.ANY),
                      pl.BlockSpec((ST, D_MODEL), lambda i, *_: (i, 0)),
                      pl.BlockSpec((1, D_MODEL), lambda i, *_: (0, 0))],
            out_specs=pl.BlockSpec((ST, D_MODEL), lambda i, *_: (i, 0)),
            scratch_shapes=[pltpu.VMEM((2 * ST * TOK_ROWS, LANES), F32),
                            pltpu.VMEM((ST * TOK_ROWS, LANES), F32),
                            pltpu.SemaphoreType.DMA(())]),
        out_shape=jax.ShapeDtypeStruct((T, D_MODEL), F32),
        compiler_params=_cp(("arbitrary",)),
        name="combine",
    )(lpos, cnt, lst, gst, w, ys, xm, fg)


def _expert_kernel(te_ref, nused_ref, xs_ref, wg_ref, wu_ref, wd_ref, ys_ref):
    i = pl.program_id(0)
    tm = xs_ref.shape[0] // TOK_ROWS

    @pl.when(i < nused_ref[0])
    def _():
        x = jnp.concatenate([xs_ref[pl.ds(s, tm, stride=TOK_ROWS), :] for s in range(TOK_ROWS)],
                            axis=1).astype(BF16)
        g = jnp.dot(x, wg_ref[0].astype(BF16), preferred_element_type=F32)
        u = jnp.dot(x, wu_ref[0].astype(BF16), preferred_element_type=F32)
        h = (g * jax.nn.sigmoid(g) * u).astype(BF16)
        y = jnp.dot(h, wd_ref[0].astype(BF16), preferred_element_type=F32)
        for s in range(TOK_ROWS):
            ys_ref[pl.ds(s, tm, stride=TOK_ROWS), :] = y[:, s * LANES:(s + 1) * LANES]

    @pl.when(i >= nused_ref[0])
    def _():
        ys_ref[...] = jnp.zeros_like(ys_ref)


def experts(tile_expert, n_used, xs, wg, wu, wd):
    n_tiles = xs.shape[0] // (TM_E * TOK_ROWS)
    tile = lambda i, te, nu: (jnp.minimum(i, nu[0] - 1), 0)
    return pl.pallas_call(
        _expert_kernel,
        grid_spec=pltpu.PrefetchScalarGridSpec(
            num_scalar_prefetch=2,
            grid=(n_tiles,),
            in_specs=[pl.BlockSpec((TM_E * TOK_ROWS, LANES), tile),
                      pl.BlockSpec((1, D_MODEL, D_EXP), lambda i, te, nu: (te[i], 0, 0)),
                      pl.BlockSpec((1, D_MODEL, D_EXP), lambda i, te, nu: (te[i], 0, 0)),
                      pl.BlockSpec((1, D_EXP, D_MODEL), lambda i, te, nu: (te[i], 0, 0))],
            out_specs=pl.BlockSpec((TM_E * TOK_ROWS, LANES), lambda i, te, nu: (i, 0))),
        out_shape=jax.ShapeDtypeStruct(xs.shape, F32),
        compiler_params=_cp(("arbitrary",)),
        name="experts",
    )(tile_expert, n_used, xs, wg, wu, wd)


def _rope_tables(S):
    inv = 1.0 / (ROPE_THETA ** (jnp.arange(0, ROPE, 2, dtype=F32) / ROPE))
    ang = jnp.arange(S, dtype=F32)[:, None] * inv[None, :]
    cos, sin = jnp.cos(ang), jnp.sin(ang)
    z = lambda n: jnp.zeros((S, n), F32)
    scale = (NOPE + ROPE) ** -0.5
    pad = LANES - NOPE - ROPE
    cq = scale * jnp.concatenate([jnp.ones((S, NOPE), F32), cos, cos, z(pad)], axis=1)
    sq = scale * jnp.concatenate([z(NOPE), -sin, sin, z(pad)], axis=1)
    ck = jnp.concatenate([z(NOPE), cos, cos, z(pad)], axis=1)
    sk = jnp.concatenate([z(NOPE), -sin, sin, z(pad)], axis=1)
    return cq, sq, ck, sk


def _block_diag(w):
    n, d, _ = w.shape
    return (jnp.eye(n, dtype=w.dtype)[:, None, :, None] * w[:, :, None, :]).reshape(n * d, n * d)


def _in_weights(w_in):
    z = lambda n: jnp.zeros((D_MODEL, n), F32)
    o = 4 * ML_W
    mi, mf = w_in[:, o:o + ML_HEADS], w_in[:, o + ML_HEADS:o + 2 * ML_HEADS]
    o += 2 * ML_HEADS
    rest = w_in[:, o:o + 2 * LRU_W + Q_LORA + KV_LORA]
    kr = w_in[:, o + 2 * LRU_W + Q_LORA + KV_LORA:]
    small = jnp.concatenate([mi, mf, z(NOPE - 2 * ML_HEADS), kr, z(LANES - NOPE - ROPE)], axis=1)
    return jnp.concatenate([w_in[:, :4 * ML_W], rest, small], axis=1).astype(BF16)


def _routing_tables(cnt_rows):
    n_st = cnt_rows.shape[0] // SUBLANES
    rows = cnt_rows.reshape(n_st, SUBLANES, LANES)
    cnt = rows[:, 0, N_GROUPS:N_GROUPS + N_EXP].astype(jnp.int32)
    lst = rows[:, 1, N_GROUPS:N_GROUPS + N_EXP].astype(jnp.int32)
    tiles_e = (jnp.sum(cnt, axis=0) + TM_E - 1) // TM_E
    tile_cum = jnp.cumsum(tiles_e)
    base = (tile_cum - tiles_e) * TM_E
    gst = base[None, :] + jnp.cumsum(cnt, axis=0) - cnt
    n_used = tile_cum[-1:].astype(jnp.int32)
    n_e = jnp.sum(cnt, axis=0)
    pad = jnp.concatenate([base + n_e, tiles_e * TM_E - n_e, n_used]).astype(jnp.int32)
    return cnt.reshape(-1), lst.reshape(-1), gst.reshape(-1), pad, tile_cum, n_used


def kernel(x, norm1_g, w_in, ml_i_bias, ml_f_bias, ml_norm_g, conv_w, conv_b, lru_w_r, lru_b_r, lru_w_i, lru_b_i, lru_lambda, lru_norm_g, q_norm_g, w_uq, kv_norm_g, w_ukv, mla_norm_g, w_out, norm2_g, w_group, b_group, w_expert, b_expert, w_gate, w_up, w_down, final_norm_g):
    B, S, _ = x.shape
    T = B * S
    depth = w_in.shape[0]
    n_tiles = (2 * T) // TM_E + N_EXP
    tabs = _rope_tables(S)
    row = lambda a: a.reshape(1, -1)
    x2d = x.reshape(T, D_MODEL)
    for l in range(depth):
        proj = in_proj(x2d, row(norm1_g[l]), _in_weights(w_in[l]))
        gate_bias = jnp.concatenate([ml_i_bias[l], ml_f_bias[l], jnp.zeros((LANES - 2 * ML_HEADS,), F32)])
        y_m = mlstm(proj, row(gate_bias), row(ml_norm_g[l]), B, S)
        w_gates = jnp.concatenate([_block_diag(lru_w_r[l]), _block_diag(lru_w_i[l])], axis=1).astype(BF16)
        b_gates = jnp.concatenate([lru_b_r[l], lru_b_i[l]])
        y_r = rglru(proj.reshape(B, S, PROJ_W), conv_w[l], row(conv_b[l]), w_gates, row(b_gates),
                    row(lru_lambda[l]), row(lru_norm_g[l])).reshape(T, LRU_W)
        wq = jnp.pad(w_uq[l].reshape(Q_LORA, MLA_H, NOPE + ROPE),
                     ((0, 0), (0, 0), (0, LANES - NOPE - ROPE))).reshape(Q_LORA, MLA_H * LANES).astype(BF16)
        wkv = w_ukv[l].reshape(KV_LORA, MLA_H, NOPE + MLA_V)
        wk = jnp.pad(wkv[:, :, :NOPE], ((0, 0), (0, 0), (0, LANES - NOPE))).reshape(KV_LORA, MLA_H * LANES).astype(BF16)
        wv = wkv[:, :, NOPE:].reshape(KV_LORA, MLA_W).astype(BF16)
        q, k, v = mla_proj(proj, row(q_norm_g[l]), row(kv_norm_g[l]), wq, wk, wv, tabs, S)
        y_a = attention(q, k, v, row(mla_norm_g[l]), B, S)
        w_router = jnp.concatenate([w_group[l], w_expert[l],
                                    jnp.zeros((D_MODEL, LANES - N_GROUPS - N_EXP), F32)], axis=1)
        b_router = jnp.concatenate([b_group[l], b_expert[l], jnp.zeros((LANES - N_GROUPS - N_EXP,), F32)])
        x_mid, t_tiles, logits = out_proj(x2d, y_m, y_r, y_a, w_out[l].astype(BF16), row(norm2_g[l]),
                                          w_router, row(b_router))
        meta, cnt_rows = route(logits)
        meta = meta.reshape(T // ST, SUBLANES, ST)
        lpos = meta[:, 0:2, :].astype(jnp.int32).reshape(-1)
        w_pick = meta[:, 2:4, :]
        cnt, lst, gst, pad, tile_cum, n_used = _routing_tables(cnt_rows)
        tile_ids = jnp.minimum(jnp.arange(n_tiles, dtype=jnp.int32), n_used[0] - 1)
        tile_expert = jnp.minimum(jnp.searchsorted(tile_cum, tile_ids, side="right"),
                                  N_EXP - 1).astype(jnp.int32)
        xs = dispatch(lpos, cnt, lst, gst, pad, t_tiles, n_tiles * TM_E)
        ys = experts(tile_expert, n_used, xs, w_gate[l], w_up[l], w_down[l])
        x2d = combine(lpos, cnt, lst, gst, w_pick, ys, x_mid, row(final_norm_g), l == depth - 1)
    return x2d.reshape(B, S, D_MODEL)
```

```python
import functools

import numpy as np
import jax
import jax.numpy as jnp
from jax import lax
from jax.experimental import pallas as pl
from jax.experimental.pallas import tpu as pltpu

F32 = jnp.float32
BF16 = jnp.bfloat16

D_MODEL = 1024
ML_HEADS = 4
ML_DH = 64
ML_W = ML_HEADS * ML_DH
LRU_W = 256
LRU_BLOCKS = 4
CONV_W = 4
LRU_C = 8.0
MLA_H = 8
NOPE = 64
ROPE = 32
MLA_V = 64
MLA_W = MLA_H * MLA_V
Q_LORA = 256
KV_LORA = 128
ROPE_THETA = 10000.0
N_GROUPS = 4
EPG = 8
N_EXP = N_GROUPS * EPG
D_EXP = 256
EPS = 1e-6

LANES = 128
SUBLANES = 8
V_EXT = MLA_V + SUBLANES
TOK_ROWS = D_MODEL // LANES
PROJ_W = 2048
SMALL_COL = 1920
VMEM_LIMIT = 56 * 1024 * 1024

ML_CHUNK = 128
LRU_TC = 256
TM_IN = 512
TQ = 256
ST = 1024
TM_E = 256


def _cp(sem):
    return pltpu.CompilerParams(dimension_semantics=sem, vmem_limit_bytes=VMEM_LIMIT)


def _rms(x, g):
    return x * lax.rsqrt(jnp.mean(x * x, axis=-1, keepdims=True) + EPS) * g


def _in_proj_kernel(x_ref, g_ref, w_ref, o_ref):
    y = _rms(x_ref[...], g_ref[...])
    o_ref[...] = jnp.dot(y.astype(BF16), w_ref[...], preferred_element_type=F32)


def in_proj(x2d, g, w):
    T = x2d.shape[0]
    return pl.pallas_call(
        _in_proj_kernel,
        grid=(T // TM_IN,),
        in_specs=[pl.BlockSpec((TM_IN, D_MODEL), lambda i: (i, 0)),
                  pl.BlockSpec((1, D_MODEL), lambda i: (0, 0)),
                  pl.BlockSpec((D_MODEL, PROJ_W), lambda i: (0, 0))],
        out_specs=pl.BlockSpec((TM_IN, PROJ_W), lambda i: (i, 0)),
        out_shape=jax.ShapeDtypeStruct((T, PROJ_W), F32),
        compiler_params=_cp(("arbitrary",)),
        name="in_proj",
    )(x2d, g, w)


def _mlstm_kernel(q_ref, k_ref, v_ref, o_ref, s_ref, bias_ref, g_ref, y_ref, c_ref, m_ref):
    L = q_ref.shape[0]
    c = pl.program_id(1)

    @pl.when(c == 0)
    def _():
        c_ref[...] = jnp.zeros_like(c_ref)
        m_ref[...] = jnp.zeros_like(m_ref)

    pre = s_ref[...] + bias_ref[...]
    lane = lax.broadcasted_iota(jnp.int32, (L, LANES), 1)
    logsig = jnp.minimum(pre, 0.0) - jnp.log1p(jnp.exp(-jnp.abs(pre)))
    gates = jnp.where(lane < ML_HEADS, pre, logsig)
    r_i = lax.broadcasted_iota(jnp.int32, (L, L), 0)
    c_i = lax.broadcasted_iota(jnp.int32, (L, L), 1)
    causal = c_i <= r_i
    cum = jnp.dot(causal.astype(F32), gates, precision=lax.Precision.HIGHEST,
                  preferred_element_type=F32)
    gates_t = gates.T
    cum_t = cum.T
    v_t = v_ref[...].T
    q = q_ref[...]
    k = k_ref[...] * (ML_DH ** -0.5)
    v = v_ref[...]
    o = jax.nn.sigmoid(o_ref[...])
    ones = jnp.ones((SUBLANES, L), F32)

    for h in range(ML_HEADS):
        sl = slice(h * ML_DH, (h + 1) * ML_DH)
        b_col = cum[:, ML_HEADS + h:ML_HEADS + h + 1]
        b_row = cum_t[ML_HEADS + h:ML_HEADS + h + 1, :]
        li_row = gates_t[h:h + 1, :]
        m_prev = m_ref[:, h:h + 1]
        dmat = jnp.where(causal, b_col - b_row + li_row, -jnp.inf)
        inter = b_col + m_prev
        m_row = jnp.maximum(jnp.max(dmat, axis=1, keepdims=True), inter)
        w_intra = jnp.exp(dmat - m_row)
        w_inter = jnp.exp(inter - m_row)
        qh = q[:, sl].astype(BF16)
        kh = k[:, sl].astype(BF16)
        vh = v[:, sl].astype(BF16)
        s = lax.dot_general(qh, kh, (((1,), (1,)), ((), ())), preferred_element_type=F32) * w_intra
        state = c_ref[h]
        qc = lax.dot_general(qh, state.astype(BF16), (((1,), (1,)), ((), ())),
                             preferred_element_type=F32)
        num = jnp.dot(s.astype(BF16), vh, preferred_element_type=F32) + w_inter * qc[:, :ML_DH]
        den = jnp.sum(s, axis=1, keepdims=True) + w_inter * qc[:, ML_DH:ML_DH + 1]
        hh = num / jnp.maximum(jnp.abs(den), jnp.exp(-m_row))
        b_last = b_row[:, L - 1:L]
        g_row = b_last - b_row + li_row
        m_new = jnp.maximum(b_last + m_prev, jnp.max(g_row, axis=1, keepdims=True))
        ws_row = jnp.exp(g_row - m_new)
        decay = jnp.exp(b_last + m_prev - m_new)
        vt_ext = jnp.concatenate([v_t[sl, :], ones], axis=0) * ws_row
        c_ref[h] = decay * state + jnp.dot(vt_ext.astype(BF16), kh, preferred_element_type=F32)
        m_ref[:, h:h + 1] = m_new
        y = o[:, sl] * hh
        y_ref[:, sl] = y * lax.rsqrt(jnp.mean(y * y, axis=1, keepdims=True) + EPS) * g_ref[:, sl]


def mlstm(proj, bias, g, B, S):
    T = B * S
    L = ML_CHUNK
    nc = S // L
    col = lambda j: pl.BlockSpec((L, ML_W), lambda b, c, j=j: (b * nc + c, j))
    return pl.pallas_call(
        _mlstm_kernel,
        grid=(B, nc),
        in_specs=[col(0), col(1), col(2), col(3),
                  pl.BlockSpec((L, LANES), lambda b, c: (b * nc + c, SMALL_COL // LANES)),
                  pl.BlockSpec((1, LANES), lambda b, c: (0, 0)),
                  pl.BlockSpec((1, ML_W), lambda b, c: (0, 0))],
        out_specs=pl.BlockSpec((L, ML_W), lambda b, c: (b * nc + c, 0)),
        out_shape=jax.ShapeDtypeStruct((T, ML_W), F32),
        scratch_shapes=[pltpu.VMEM((ML_HEADS, ML_DH + SUBLANES, ML_DH), F32),
                        pltpu.VMEM((1, LANES), F32)],
        compiler_params=_cp(("arbitrary", "arbitrary")),
        name="mlstm",
    )(proj, proj, proj, proj, proj, bias, g)


def _rglru_kernel(x_ref, gate_ref, cw_ref, cb_ref, wg_ref, bg_ref, lam_ref, g_ref, y_ref,
                  xp_ref, a_ref, u_ref, h_ref, carry_ref):
    B, tc, W = x_ref.shape
    c = pl.program_id(0)

    @pl.when(c == 0)
    def _():
        xp_ref[:, 0:SUBLANES, :] = jnp.zeros((B, SUBLANES, W), F32)
        carry_ref[...] = jnp.zeros_like(carry_ref)

    xp_ref[:, SUBLANES:, :] = x_ref[...]
    xc = jnp.zeros((B, tc, W), F32) + cb_ref[...]
    for j in range(CONV_W):
        off = SUBLANES - (CONV_W - 1) + j
        xc = xc + cw_ref[j:j + 1, :] * xp_ref[:, off:off + tc, :]
    xp_ref[:, 0:SUBLANES, :] = x_ref[:, tc - SUBLANES:, :]

    xc2 = xc.reshape(B * tc, W)
    ri = jax.nn.sigmoid(jnp.dot(xc2.astype(BF16), wg_ref[...], preferred_element_type=F32) + bg_ref[...])
    r = ri[:, :W]
    ig = ri[:, W:]
    lam = lam_ref[...]
    softplus_neg = jnp.maximum(-lam, 0.0) + jnp.log1p(jnp.exp(-jnp.abs(lam)))
    log_a = -LRU_C * r * softplus_neg
    a = jnp.exp(log_a)
    u = jnp.sqrt(1.0 - jnp.exp(2.0 * log_a)) * (ig * xc2)
    halves = W // LANES
    for k in range(halves):
        a_ref[k] = a[:, k * LANES:(k + 1) * LANES]
        u_ref[k] = u[:, k * LANES:(k + 1) * LANES]

    def step(i, hs):
        base = pl.multiple_of(i * SUBLANES, SUBLANES)
        for j in range(SUBLANES):
            rows = pl.ds(base + j, B, stride=tc)
            hs = tuple(a_ref.at[k][rows, :] * hs[k] + u_ref.at[k][rows, :] for k in range(halves))
            for k in range(halves):
                h_ref.at[k][rows, :] = hs[k]
        return hs

    init = tuple(carry_ref[k] for k in range(halves))
    last = lax.fori_loop(0, tc // SUBLANES, step, init)
    for k in range(halves):
        carry_ref[k] = last[k]
    h_all = jnp.concatenate([h_ref[k] for k in range(halves)], axis=1)
    y = h_all.reshape(B, tc, W) * jax.nn.gelu(gate_ref[...])
    y_ref[...] = _rms(y, g_ref[...])


def rglru(proj3, cw, cb, wg, bg, lam, g):
    B, S, _ = proj3.shape
    tc = LRU_TC
    vec = lambda n: pl.BlockSpec((1, n), lambda c: (0, 0))
    return pl.pallas_call(
        _rglru_kernel,
        grid=(S // tc,),
        in_specs=[pl.BlockSpec((B, tc, LRU_W), lambda c: (0, c, 4)),
                  pl.BlockSpec((B, tc, LRU_W), lambda c: (0, c, 5)),
                  pl.BlockSpec((CONV_W, LRU_W), lambda c: (0, 0)),
                  vec(LRU_W),
                  pl.BlockSpec((LRU_W, 2 * LRU_W), lambda c: (0, 0)),
                  vec(2 * LRU_W), vec(LRU_W), vec(LRU_W)],
        out_specs=pl.BlockSpec((B, tc, LRU_W), lambda c: (0, c, 0)),
        out_shape=jax.ShapeDtypeStruct((B, S, LRU_W), F32),
        scratch_shapes=[pltpu.VMEM((B, tc + SUBLANES, LRU_W), F32),
                        pltpu.VMEM((LRU_W // LANES, B * tc, LANES), F32),
                        pltpu.VMEM((LRU_W // LANES, B * tc, LANES), F32),
                        pltpu.VMEM((LRU_W // LANES, B * tc, LANES), F32),
                        pltpu.VMEM((LRU_W // LANES, B, LANES), F32)],
        compiler_params=_cp(("arbitrary",)),
        name="rglru",
    )(proj3, proj3, cw, cb, wg, bg, lam, g)


def _rot_half(blk):
    lane = lax.broadcasted_iota(jnp.int32, blk.shape, 1)
    return jnp.where(lane < NOPE + ROPE // 2,
                     pltpu.roll(blk, LANES - ROPE // 2, 1), pltpu.roll(blk, ROPE // 2, 1))


def _mla_proj_kernel(cq_ref, ckv_ref, qg_ref, kvg_ref, wq_ref, wk_ref, wvt_ref, vones_ref,
                     cq_t, sq_t, ck_t, sk_t, q_ref, k_ref, vt_ref):
    qn = _rms(cq_ref[...], qg_ref[...]).astype(BF16)
    qf = jnp.dot(qn, wq_ref[...], preferred_element_type=F32)
    kvn = _rms(ckv_ref[:, :KV_LORA], kvg_ref[...]).astype(BF16)
    kf = jnp.dot(kvn, wk_ref[...], preferred_element_type=F32)
    vt_ref[...] = (lax.dot_general(wvt_ref[...], kvn, (((1,), (1,)), ((), ())), preferred_element_type=F32)
                   + vones_ref[...]).astype(BF16)
    small = ckv_ref[:, KV_LORA:]
    kr = small * ck_t[...] + _rot_half(small) * sk_t[...]
    cq = cq_t[...]
    sq = sq_t[...]
    for h in range(MLA_H):
        sl = slice(h * LANES, (h + 1) * LANES)
        blk = qf[:, sl]
        q_ref[:, sl] = (blk * cq + _rot_half(blk) * sq).astype(BF16)
        k_ref[:, sl] = (kf[:, sl] + kr).astype(BF16)


def mla_proj(proj, qg, kvg, wq, wk, wvt, vones, tabs, S):
    T = proj.shape[0]
    tm = TM_IN
    npos = S // tm
    tab = pl.BlockSpec((tm, LANES), lambda i: (i % npos, 0))
    full = lambda a: pl.BlockSpec(a.shape, lambda i: (0, 0))
    return pl.pallas_call(
        _mla_proj_kernel,
        grid=(T // tm,),
        in_specs=[pl.BlockSpec((tm, Q_LORA), lambda i: (i, 6)),
                  pl.BlockSpec((tm, 2 * KV_LORA), lambda i: (i, 7)),
                  full(qg), full(kvg), full(wq), full(wk), full(wvt), full(vones), tab, tab, tab, tab],
        out_specs=[pl.BlockSpec((tm, MLA_H * LANES), lambda i: (i, 0)),
                   pl.BlockSpec((tm, MLA_H * LANES), lambda i: (i, 0)),
                   pl.BlockSpec((MLA_H * V_EXT, tm), lambda i: (0, i))],
        out_shape=[jax.ShapeDtypeStruct((T, MLA_H * LANES), BF16),
                   jax.ShapeDtypeStruct((T, MLA_H * LANES), BF16),
                   jax.ShapeDtypeStruct((MLA_H * V_EXT, T), BF16)],
        compiler_params=_cp(("arbitrary",)),
        name="mla_proj",
    )(proj, proj, qg, kvg, wq, wk, wvt, vones, *tabs)


def _attn_kernel(q_ref, k_ref, vt_ref, g_ref, o_ref, acc_ref, m_ref, yt_ref):
    tq = q_ref.shape[0]
    i = pl.program_id(1)
    key_i = lax.broadcasted_iota(jnp.int32, (tq, tq), 0)
    qry_i = lax.broadcasted_iota(jnp.int32, (tq, tq), 1)
    causal = key_i <= qry_i
    nt = (((1,), (1,)), ((), ()))
    m_ref[...] = jnp.full_like(m_ref, -jnp.inf)
    acc_ref[...] = jnp.zeros_like(acc_ref)

    ahead = 4

    def scores(off, h):
        hsl = slice(h * LANES, (h + 1) * LANES)
        return lax.dot_general(k_ref[pl.ds(off, tq), hsl], q_ref[:, hsl], nt, preferred_element_type=F32)

    def update(h, alpha, pv):
        rows = slice(h * V_EXT, (h + 1) * V_EXT)
        acc_ref[rows, :] = alpha * acc_ref[rows, :] + pv

    def block(j, masked):
        off = pl.multiple_of(j * tq, tq)
        pending = [scores(off, h) for h in range(ahead)]
        late = None
        for h in range(MLA_H):
            s = pending.pop(0)
            if h + ahead < MLA_H:
                pending.append(scores(off, h + ahead))
            if masked:
                s = jnp.where(causal, s, -jnp.inf)
            m = m_ref[h:h + 1, :]
            m_new = jnp.maximum(m, jnp.max(s, axis=0, keepdims=True))
            m_ref[h:h + 1, :] = m_new
            alpha = jnp.exp2(m - m_new)
            p = jnp.exp2(s - m_new).astype(BF16)
            pv = jnp.dot(vt_ref[h * V_EXT:(h + 1) * V_EXT, pl.ds(off, tq)], p, preferred_element_type=F32)
            if late is not None:
                update(*late)
            late = (h, alpha, pv)
        update(*late)

    def body(j, carry):
        block(j, False)
        return carry

    lax.fori_loop(0, i, body, 0)
    block(i, True)

    ssq = jnp.zeros((1, tq), F32)
    for h in range(MLA_H):
        vrows = slice(h * MLA_V, (h + 1) * MLA_V)
        yh = acc_ref[h * V_EXT:h * V_EXT + MLA_V, :] * (1.0 / acc_ref[h * V_EXT + MLA_V:h * V_EXT + MLA_V + 1, :])
        yt_ref[vrows, :] = yh
        ssq = ssq + jnp.sum(yh * yh, axis=0, keepdims=True)
    yt = yt_ref[...] * lax.rsqrt(ssq * (1.0 / MLA_W) + EPS) * g_ref[...]
    o_ref[...] = yt.T


def attention(q, k, vt, g, B, S):
    T = B * S
    nq = S // TQ
    return pl.pallas_call(
        _attn_kernel,
        grid=(B, nq),
        in_specs=[pl.BlockSpec((TQ, MLA_H * LANES), lambda b, i: (b * nq + i, 0)),
                  pl.BlockSpec((S, MLA_H * LANES), lambda b, i: (b, 0)),
                  pl.BlockSpec((MLA_H * V_EXT, S), lambda b, i: (0, b)),
                  pl.BlockSpec((MLA_W, 1), lambda b, i: (0, 0))],
        out_specs=pl.BlockSpec((TQ, MLA_W), lambda b, i: (b * nq + i, 0)),
        out_shape=jax.ShapeDtypeStruct((T, MLA_W), F32),
        scratch_shapes=[pltpu.VMEM((MLA_H * V_EXT, TQ), F32),
                        pltpu.VMEM((MLA_H, TQ), F32),
                        pltpu.VMEM((MLA_W, TQ), F32)],
        compiler_params=_cp(("arbitrary", "arbitrary")),
        name="attention",
    )(q, k, vt, g)


def _out_proj_kernel(x_ref, ym_ref, yr_ref, ya_ref, w_ref, g_ref, wr_ref, br_ref,
                     xm_ref, t_ref, lg_ref):
    tm = x_ref.shape[0]
    acc = jnp.dot(ym_ref[...].astype(BF16), w_ref[0:ML_W, :], preferred_element_type=F32)
    acc += jnp.dot(yr_ref[...].astype(BF16), w_ref[ML_W:ML_W + LRU_W, :], preferred_element_type=F32)
    acc += jnp.dot(ya_ref[...].astype(BF16), w_ref[ML_W + LRU_W:, :], preferred_element_type=F32)
    xm = x_ref[...] + acc
    xm_ref[...] = xm
    t = _rms(xm, g_ref[...])
    lg_ref[...] = jnp.dot(t, wr_ref[...], precision=lax.Precision.HIGHEST,
                          preferred_element_type=F32) + br_ref[...]
    for s in range(TOK_ROWS):
        t_ref[pl.ds(s, tm, stride=TOK_ROWS), :] = t[:, s * LANES:(s + 1) * LANES]


def out_proj(x2d, ym, yr, ya, w, g, wr, br):
    T = x2d.shape[0]
    tm = TM_IN
    row = lambda n: pl.BlockSpec((tm, n), lambda i: (i, 0))
    full = lambda a: pl.BlockSpec(a.shape, lambda i: (0, 0))
    return pl.pallas_call(
        _out_proj_kernel,
        grid=(T // tm,),
        in_specs=[row(D_MODEL), row(ML_W), row(LRU_W), row(MLA_W), full(w), full(g), full(wr), full(br)],
        out_specs=[row(D_MODEL),
                   pl.BlockSpec((tm * TOK_ROWS, LANES), lambda i: (i, 0)),
                   row(LANES)],
        out_shape=[jax.ShapeDtypeStruct((T, D_MODEL), F32),
                   jax.ShapeDtypeStruct((T * TOK_ROWS, LANES), F32),
                   jax.ShapeDtypeStruct((T, LANES), F32)],
        compiler_params=_cp(("arbitrary",)),
        name="out_proj",
    )(x2d, ym, yr, ya, w, g, wr, br)


def _route_kernel(lg_ref, meta_ref, cnt_ref):
    st = lg_ref.shape[0]
    lg = lg_ref[...]
    lane = lax.broadcasted_iota(jnp.int32, (st, LANES), 1)
    row_max = lambda a: jnp.max(a, axis=1, keepdims=True)
    row_sum = lambda a: jnp.sum(a, axis=1, keepdims=True)
    first = lambda cond: jnp.min(jnp.where(cond, lane, LANES), axis=1, keepdims=True)

    gmask = lane < N_GROUPS
    gl = jnp.where(gmask, lg, -jnp.inf)
    ge = jnp.exp(gl - row_max(gl))
    gp = ge / row_sum(ge)
    g_val = row_max(gp)
    g_idx = first(gmask & (gp == g_val))

    smask = (lane >= N_GROUPS) & (lane < N_GROUPS + N_EXP) & (((lane - N_GROUPS) >> 3) == g_idx)
    el = jnp.where(smask, lg, -jnp.inf)
    ee = jnp.exp(el - row_max(el))
    ep = ee / row_sum(ee)
    v1 = row_max(ep)
    i1 = first(smask & (ep == v1))
    rest = smask & (lane != i1)
    v2 = row_max(jnp.where(rest, ep, -1.0))
    i2 = first(rest & (ep == v2))
    den = v1 + v2
    w1 = v1 / den * g_val
    w2 = v2 / den * g_val

    onehot = ((lane == i1) | (lane == i2)).astype(F32)
    sb = min(st, 256)
    r_i = lax.broadcasted_iota(jnp.int32, (sb, sb), 0)
    c_i = lax.broadcasted_iota(jnp.int32, (sb, sb), 1)
    tri = (c_i < r_i).astype(BF16)
    run = jnp.zeros((1, LANES), F32)
    ranks = []
    for j in range(st // sb):
        oh = onehot[j * sb:(j + 1) * sb]
        ranks.append(jnp.dot(tri, oh.astype(BF16), preferred_element_type=F32) + run)
        run = run + jnp.sum(oh, axis=0, keepdims=True)
    rank = jnp.concatenate(ranks, axis=0)
    u_r = lax.broadcasted_iota(jnp.int32, (LANES, LANES), 0)
    u_c = lax.broadcasted_iota(jnp.int32, (LANES, LANES), 1)
    lstart = jnp.dot(jnp.broadcast_to(run, (SUBLANES, LANES)), (u_r < u_c).astype(F32),
                     precision=lax.Precision.HIGHEST, preferred_element_type=F32)[0:1]
    slot = rank + lstart
    p1 = row_sum(jnp.where(lane == i1, slot, 0.0))
    p2 = row_sum(jnp.where(lane == i2, slot, 0.0))
    meta = jnp.where(lane == 0, p1, jnp.where(lane == 1, p2, jnp.where(lane == 2, w1,
                                                                      jnp.where(lane == 3, w2, 0.0))))
    meta_ref[...] = meta.T[0:SUBLANES, :]
    sub = lax.broadcasted_iota(jnp.int32, (SUBLANES, LANES), 0)
    cnt_ref[...] = jnp.where(sub == 0, run, jnp.where(sub == 1, lstart, 0.0))


def route(logits):
    T = logits.shape[0]
    n_st = T // ST
    return pl.pallas_call(
        _route_kernel,
        grid=(n_st,),
        in_specs=[pl.BlockSpec((ST, LANES), lambda i: (i, 0))],
        out_specs=[pl.BlockSpec((SUBLANES, ST), lambda i: (i, 0)),
                   pl.BlockSpec((SUBLANES, LANES), lambda i: (i, 0))],
        out_shape=[jax.ShapeDtypeStruct((n_st * SUBLANES, ST), F32),
                   jax.ShapeDtypeStruct((n_st * SUBLANES, LANES), F32)],
        compiler_params=_cp(("arbitrary",)),
        name="route",
    )(logits)


def _copy_by_bits(n, nbits, copy_at, start):
    off = jnp.int32(0)
    for bit in reversed(range(nbits)):
        sz = 1 << bit

        @pl.when((n & sz) != 0)
        def _(off=off, sz=sz):
            cp = copy_at(off, sz)
            if start:
                cp.start()
            else:
                cp.wait()

        off = off + (n & sz)


def _strip_copies(st, cnt_ref, lst_ref, gst_ref, make_copy, start):
    def body(e, carry):
        loc = lst_ref[st * N_EXP + e]
        glob = gst_ref[st * N_EXP + e]
        _copy_by_bits(
            cnt_ref[st * N_EXP + e], int(np.log2(ST)) + 1,
            lambda off, sz: make_copy(pl.multiple_of((loc + off) * TOK_ROWS, TOK_ROWS),
                                      pl.multiple_of((glob + off) * TOK_ROWS, TOK_ROWS), sz * TOK_ROWS),
            start)
        return carry

    lax.fori_loop(0, N_EXP, body, 0)


def _zero_padding(pad_ref, xs_ref, zero_ref, sem, start):
    tile_rows = TM_E * TOK_ROWS

    def body(e, carry):
        first = pad_ref[e]
        _copy_by_bits(
            pad_ref[N_EXP + e], int(np.log2(TM_E)),
            lambda off, sz: pltpu.make_async_copy(
                zero_ref.at[pl.ds(0, sz * TOK_ROWS)],
                xs_ref.at[pl.ds(pl.multiple_of((first + off) * TOK_ROWS, TOK_ROWS), sz * TOK_ROWS)], sem),
            start)
        return carry

    lax.fori_loop(0, N_EXP, body, 0)

    def tail(i, carry):
        cp = pltpu.make_async_copy(
            zero_ref, xs_ref.at[pl.ds(pl.multiple_of(i * tile_rows, tile_rows), tile_rows)], sem)
        if start:
            cp.start()
        else:
            cp.wait()
        return carry

    lax.fori_loop(pad_ref[2 * N_EXP], xs_ref.shape[0] // tile_rows, tail, 0)


def _dispatch_kernel(lpos_ref, cnt_ref, lst_ref, gst_ref, pad_ref, t_ref, xs_ref,
                     stage_ref, zero_ref, sem, zsem):
    st = pl.program_id(0)
    base = st * 2 * ST

    @pl.when(st == 0)
    def _():
        zero_ref[...] = jnp.zeros_like(zero_ref)
        _zero_padding(pad_ref, xs_ref, zero_ref, zsem, True)

    def scatter(i, carry):
        for u in range(SUBLANES):
            t = i * SUBLANES + u
            tile = t_ref[pl.ds(pl.multiple_of(t * TOK_ROWS, TOK_ROWS), TOK_ROWS), :]
            p0 = lpos_ref[base + t]
            p1 = lpos_ref[base + ST + t]
            stage_ref[pl.ds(pl.multiple_of(p0 * TOK_ROWS, TOK_ROWS), TOK_ROWS), :] = tile
            stage_ref[pl.ds(pl.multiple_of(p1 * TOK_ROWS, TOK_ROWS), TOK_ROWS), :] = tile
        return carry

    lax.fori_loop(0, ST // SUBLANES, scatter, 0)

    def make_copy(loc, glob, rows):
        return pltpu.make_async_copy(stage_ref.at[pl.ds(loc, rows)], xs_ref.at[pl.ds(glob, rows)], sem)

    _strip_copies(st, cnt_ref, lst_ref, gst_ref, make_copy, True)
    _strip_copies(st, cnt_ref, lst_ref, gst_ref, make_copy, False)

    @pl.when(st == 0)
    def _():
        _zero_padding(pad_ref, xs_ref, zero_ref, zsem, False)


def dispatch(lpos, cnt, lst, gst, pad, t_tiles, n_rows):
    T = t_tiles.shape[0] // TOK_ROWS
    return pl.pallas_call(
        _dispatch_kernel,
        grid_spec=pltpu.PrefetchScalarGridSpec(
            num_scalar_prefetch=5,
            grid=(T // ST,),
            in_specs=[pl.BlockSpec((ST * TOK_ROWS, LANES), lambda i, *_: (i, 0))],
            out_specs=pl.BlockSpec(memory_space=pl.ANY),
            scratch_shapes=[pltpu.VMEM((2 * ST * TOK_ROWS, LANES), F32),
                            pltpu.VMEM((TM_E * TOK_ROWS, LANES), F32),
                            pltpu.SemaphoreType.DMA(()),
                            pltpu.SemaphoreType.DMA(())]),
        out_shape=jax.ShapeDtypeStruct((n_rows * TOK_ROWS, LANES), F32),
        compiler_params=_cp(("arbitrary",)),
        name="dispatch",
    )(lpos, cnt, lst, gst, pad, t_tiles)


def _combine_kernel(lpos_ref, cnt_ref, lst_ref, gst_ref, w_ref, ys_ref, xm_ref, fg_ref, o_ref,
                    stage_ref, comb_ref, sem, *, final):
    st = pl.program_id(0)
    base = st * 2 * ST

    def make_copy(loc, glob, rows):
        return pltpu.make_async_copy(ys_ref.at[pl.ds(glob, rows)], stage_ref.at[pl.ds(loc, rows)], sem)

    _strip_copies(st, cnt_ref, lst_ref, gst_ref, make_copy, True)
    _strip_copies(st, cnt_ref, lst_ref, gst_ref, make_copy, False)

    def gather(i, carry):
        for u in range(SUBLANES):
            t = i * SUBLANES + u
            p0 = lpos_ref[base + t]
            p1 = lpos_ref[base + ST + t]
            y0 = stage_ref[pl.ds(pl.multiple_of(p0 * TOK_ROWS, TOK_ROWS), TOK_ROWS), :]
            y1 = stage_ref[pl.ds(pl.multiple_of(p1 * TOK_ROWS, TOK_ROWS), TOK_ROWS), :]
            comb_ref[pl.ds(pl.multiple_of(t * TOK_ROWS, TOK_ROWS), TOK_ROWS), :] = (
                w_ref[0, 0, t] * y0 + w_ref[0, 1, t] * y1)
        return carry

    lax.fori_loop(0, ST // SUBLANES, gather, 0)
    moe = jnp.concatenate([comb_ref[pl.ds(s, ST, stride=TOK_ROWS), :] for s in range(TOK_ROWS)], axis=1)
    x_new = xm_ref[...] + moe
    o_ref[...] = _rms(x_new, fg_ref[...]) if final else x_new


def combine(lpos, cnt, lst, gst, w, ys, xm, fg, final):
    T = xm.shape[0]
    return pl.pallas_call(
        functools.partial(_combine_kernel, final=final),
        grid_spec=pltpu.PrefetchScalarGridSpec(
            num_scalar_prefetch=4,
            grid=(T // ST,),
            in_specs=[pl.BlockSpec((1, 2, ST), lambda i, *_: (i, 0, 0), memory_space=pltpu.SMEM),
                      pl.BlockSpec(memory_space=pl.ANY),
                      pl.BlockSpec((ST, D_MODEL), lambda i, *_: (i, 0)),
                      pl.BlockSpec((1, D_MODEL), lambda i, *_: (0, 0))],
            out_specs=pl.BlockSpec((ST, D_MODEL), lambda i, *_: (i, 0)),
            scratch_shapes=[pltpu.VMEM((2 * ST * TOK_ROWS, LANES), F32),
                            pltpu.VMEM((ST * TOK_ROWS, LANES), F32),
                            pltpu.SemaphoreType.DMA(())]),
        out_shape=jax.ShapeDtypeStruct((T, D_MODEL), F32),
        compiler_params=_cp(("arbitrary",)),
        name="combine",
    )(lpos, cnt, lst, gst, w, ys, xm, fg)


def _expert_kernel(te_ref, nused_ref, xs_ref, wg_ref, wu_ref, wd_ref, ys_ref):
    i = pl.program_id(0)
    tm = xs_ref.shape[0] // TOK_ROWS

    @pl.when(i < nused_ref[0])
    def _():
        x = jnp.concatenate([xs_ref[pl.ds(s, tm, stride=TOK_ROWS), :] for s in range(TOK_ROWS)],
                            axis=1).astype(BF16)
        g = jnp.dot(x, wg_ref[0, 0].astype(BF16), preferred_element_type=F32)
        u = jnp.dot(x, wu_ref[0, 0].astype(BF16), preferred_element_type=F32)
        h = (g * jax.nn.sigmoid(g) * u).astype(BF16)
        y = jnp.dot(h, wd_ref[0, 0].astype(BF16), preferred_element_type=F32)
        for s in range(TOK_ROWS):
            ys_ref[pl.ds(s, tm, stride=TOK_ROWS), :] = y[:, s * LANES:(s + 1) * LANES]

    @pl.when(i >= nused_ref[0])
    def _():
        ys_ref[...] = jnp.zeros_like(ys_ref)


def experts(tile_expert, n_used, xs, wg, wu, wd, layer):
    n_tiles = xs.shape[0] // (TM_E * TOK_ROWS)
    tile = lambda i, te, nu: (jnp.minimum(i, nu[0] - 1), 0)
    return pl.pallas_call(
        _expert_kernel,
        grid_spec=pltpu.PrefetchScalarGridSpec(
            num_scalar_prefetch=2,
            grid=(n_tiles,),
            in_specs=[pl.BlockSpec((TM_E * TOK_ROWS, LANES), tile),
                      pl.BlockSpec((1, 1, D_MODEL, D_EXP), lambda i, te, nu: (layer, te[i], 0, 0)),
                      pl.BlockSpec((1, 1, D_MODEL, D_EXP), lambda i, te, nu: (layer, te[i], 0, 0)),
                      pl.BlockSpec((1, 1, D_EXP, D_MODEL), lambda i, te, nu: (layer, te[i], 0, 0))],
            out_specs=pl.BlockSpec((TM_E * TOK_ROWS, LANES), lambda i, te, nu: (i, 0))),
        out_shape=jax.ShapeDtypeStruct(xs.shape, F32),
        compiler_params=_cp(("arbitrary",)),
        name="experts",
    )(tile_expert, n_used, xs, wg, wu, wd)


def _rope_tables(S):
    inv = 1.0 / (ROPE_THETA ** (jnp.arange(0, ROPE, 2, dtype=F32) / ROPE))
    ang = jnp.arange(S, dtype=F32)[:, None] * inv[None, :]
    cos, sin = jnp.cos(ang), jnp.sin(ang)
    z = lambda n: jnp.zeros((S, n), F32)
    scale = (NOPE + ROPE) ** -0.5 * np.log2(np.e)
    pad = LANES - NOPE - ROPE
    cq = scale * jnp.concatenate([jnp.ones((S, NOPE), F32), cos, cos, z(pad)], axis=1)
    sq = scale * jnp.concatenate([z(NOPE), -sin, sin, z(pad)], axis=1)
    ck = jnp.concatenate([z(NOPE), cos, cos, z(pad)], axis=1)
    sk = jnp.concatenate([z(NOPE), -sin, sin, z(pad)], axis=1)
    return cq, sq, ck, sk


def _block_diag(w):
    n, d, _ = w.shape
    return (jnp.eye(n, dtype=w.dtype)[:, None, :, None] * w[:, :, None, :]).reshape(n * d, n * d)


def _in_weights(w_in):
    z = lambda n: jnp.zeros((D_MODEL, n), F32)
    o = 4 * ML_W
    mi, mf = w_in[:, o:o + ML_HEADS], w_in[:, o + ML_HEADS:o + 2 * ML_HEADS]
    o += 2 * ML_HEADS
    rest = w_in[:, o:o + 2 * LRU_W + Q_LORA + KV_LORA]
    kr = w_in[:, o + 2 * LRU_W + Q_LORA + KV_LORA:]
    small = jnp.concatenate([mi, mf, z(NOPE - 2 * ML_HEADS), kr, z(LANES - NOPE - ROPE)], axis=1)
    return jnp.concatenate([w_in[:, :4 * ML_W], rest, small], axis=1).astype(BF16)


def _routing_tables(cnt_rows):
    n_st = cnt_rows.shape[0] // SUBLANES
    rows = cnt_rows.reshape(n_st, SUBLANES, LANES)
    cnt = rows[:, 0, N_GROUPS:N_GROUPS + N_EXP].astype(jnp.int32)
    lst = rows[:, 1, N_GROUPS:N_GROUPS + N_EXP].astype(jnp.int32)
    tiles_e = (jnp.sum(cnt, axis=0) + TM_E - 1) // TM_E
    tile_cum = jnp.cumsum(tiles_e)
    base = (tile_cum - tiles_e) * TM_E
    gst = base[None, :] + jnp.cumsum(cnt, axis=0) - cnt
    n_used = tile_cum[-1:].astype(jnp.int32)
    n_e = jnp.sum(cnt, axis=0)
    pad = jnp.concatenate([base + n_e, tiles_e * TM_E - n_e, n_used]).astype(jnp.int32)
    return cnt.reshape(-1), lst.reshape(-1), gst.reshape(-1), pad, tile_cum, n_used


def kernel(x, norm1_g, w_in, ml_i_bias, ml_f_bias, ml_norm_g, conv_w, conv_b, lru_w_r, lru_b_r, lru_w_i, lru_b_i, lru_lambda, lru_norm_g, q_norm_g, w_uq, kv_norm_g, w_ukv, mla_norm_g, w_out, norm2_g, w_group, b_group, w_expert, b_expert, w_gate, w_up, w_down, final_norm_g):
    B, S, _ = x.shape
    T = B * S
    depth = w_in.shape[0]
    n_tiles = (2 * T) // TM_E + N_EXP
    tabs = _rope_tables(S)
    vones = jnp.tile(jnp.concatenate([jnp.zeros((MLA_V,), F32), jnp.ones((V_EXT - MLA_V,), F32)]),
                     MLA_H).reshape(MLA_H * V_EXT, 1)
    row = lambda a: a.reshape(1, -1)
    x2d = x.reshape(T, D_MODEL)
    for l in range(depth):
        proj = in_proj(x2d, row(norm1_g[l]), _in_weights(w_in[l]))
        gate_bias = jnp.concatenate([ml_i_bias[l], ml_f_bias[l], jnp.zeros((LANES - 2 * ML_HEADS,), F32)])
        y_m = mlstm(proj, row(gate_bias), row(ml_norm_g[l]), B, S)
        w_gates = jnp.concatenate([_block_diag(lru_w_r[l]), _block_diag(lru_w_i[l])], axis=1).astype(BF16)
        b_gates = jnp.concatenate([lru_b_r[l], lru_b_i[l]])
        y_r = rglru(proj.reshape(B, S, PROJ_W), conv_w[l], row(conv_b[l]), w_gates, row(b_gates),
                    row(lru_lambda[l]), row(lru_norm_g[l])).reshape(T, LRU_W)
        wq = jnp.pad(w_uq[l].reshape(Q_LORA, MLA_H, NOPE + ROPE),
                     ((0, 0), (0, 0), (0, LANES - NOPE - ROPE))).reshape(Q_LORA, MLA_H * LANES).astype(BF16)
        wkv = w_ukv[l].reshape(KV_LORA, MLA_H, NOPE + MLA_V)
        wk = jnp.pad(wkv[:, :, :NOPE], ((0, 0), (0, 0), (0, LANES - NOPE))).reshape(KV_LORA, MLA_H * LANES).astype(BF16)
        wvt = jnp.pad(wkv[:, :, NOPE:], ((0, 0), (0, 0), (0, V_EXT - MLA_V))).reshape(KV_LORA, MLA_H * V_EXT).T.astype(BF16)
        q, k, vt = mla_proj(proj, row(q_norm_g[l]), row(kv_norm_g[l]), wq, wk, wvt, vones, tabs, S)
        y_a = attention(q, k, vt, mla_norm_g[l].reshape(MLA_W, 1), B, S)
        w_router = jnp.concatenate([w_group[l], w_expert[l],
                                    jnp.zeros((D_MODEL, LANES - N_GROUPS - N_EXP), F32)], axis=1)
        b_router = jnp.concatenate([b_group[l], b_expert[l], jnp.zeros((LANES - N_GROUPS - N_EXP,), F32)])
        x_mid, t_tiles, logits = out_proj(x2d, y_m, y_r, y_a, w_out[l].astype(BF16), row(norm2_g[l]),
                                          w_router, row(b_router))
        meta, cnt_rows = route(logits)
        meta = meta.reshape(T // ST, SUBLANES, ST)
        lpos = meta[:, 0:2, :].astype(jnp.int32).reshape(-1)
        w_pick = meta[:, 2:4, :]
        cnt, lst, gst, pad, tile_cum, n_used = _routing_tables(cnt_rows)
        tile_ids = jnp.minimum(jnp.arange(n_tiles, dtype=jnp.int32), n_used[0] - 1)
        tile_expert = jnp.minimum(jnp.sum(tile_ids[:, None] >= tile_cum[None, :], axis=1),
                                  N_EXP - 1).astype(jnp.int32)
        xs = dispatch(lpos, cnt, lst, gst, pad, t_tiles, n_tiles * TM_E)
        ys = experts(tile_expert, n_used, xs, w_gate, w_up, w_down, l)
        x2d = combine(lpos, cnt, lst, gst, w_pick, ys, x_mid, row(final_norm_g), l == depth - 1)
    return x2d.reshape(B, S, D_MODEL)
```

```python
import functools

import numpy as np
import jax
import jax.numpy as jnp
from jax import lax
from jax.experimental import pallas as pl
from jax.experimental.pallas import tpu as pltpu

F32 = jnp.float32
BF16 = jnp.bfloat16

D_MODEL = 1024
ML_HEADS = 4
ML_DH = 64
ML_W = ML_HEADS * ML_DH
LRU_W = 256
LRU_BLOCKS = 4
CONV_W = 4
LRU_C = 8.0
MLA_H = 8
NOPE = 64
ROPE = 32
MLA_V = 64
MLA_W = MLA_H * MLA_V
Q_LORA = 256
KV_LORA = 128
ROPE_THETA = 10000.0
N_GROUPS = 4
EPG = 8
N_EXP = N_GROUPS * EPG
D_EXP = 256
EPS = 1e-6

LANES = 128
SUBLANES = 8
V_EXT = MLA_V + SUBLANES
TOK_ROWS = D_MODEL // LANES
PROJ_W = 2048
SMALL_COL = 1920
VMEM_LIMIT = 56 * 1024 * 1024

ML_CHUNK = 128
ML_NB = 2
LRU_TC = 256
TM_IN = 512
TQ = 256
ST = 1024
TM_E = 256


def _cp(sem):
    return pltpu.CompilerParams(dimension_semantics=sem, vmem_limit_bytes=VMEM_LIMIT)


def _rms(x, g):
    return x * lax.rsqrt(jnp.mean(x * x, axis=-1, keepdims=True) + EPS) * g


def _in_proj_kernel(x_ref, g_ref, w_ref, o_ref):
    y = _rms(x_ref[...], g_ref[...])
    o_ref[...] = jnp.dot(y.astype(BF16), w_ref[...], preferred_element_type=F32)


def in_proj(x2d, g, w):
    T = x2d.shape[0]
    return pl.pallas_call(
        _in_proj_kernel,
        grid=(T // TM_IN,),
        in_specs=[pl.BlockSpec((TM_IN, D_MODEL), lambda i: (i, 0)),
                  pl.BlockSpec((1, D_MODEL), lambda i: (0, 0)),
                  pl.BlockSpec((D_MODEL, PROJ_W), lambda i: (0, 0))],
        out_specs=pl.BlockSpec((TM_IN, PROJ_W), lambda i: (i, 0)),
        out_shape=jax.ShapeDtypeStruct((T, PROJ_W), F32),
        compiler_params=_cp(("arbitrary",)),
        name="in_proj",
    )(x2d, g, w)


def _mlstm_kernel(q_ref, k_ref, v_ref, o_ref, s_ref, bias_ref, g_ref, y_ref, c_ref, m_ref):
    nb, L, _ = q_ref.shape
    c = pl.program_id(1)

    @pl.when(c == 0)
    def _():
        c_ref[...] = jnp.zeros_like(c_ref)
        m_ref[...] = jnp.zeros_like(m_ref)

    lane = lax.broadcasted_iota(jnp.int32, (L, LANES), 1)
    row_i = lax.broadcasted_iota(jnp.int32, (L, L), 0)
    col_i = lax.broadcasted_iota(jnp.int32, (L, L), 1)
    lower = (col_i <= row_i).astype(F32)
    causal_t = row_i <= col_i
    nt = (((1,), (1,)), ((), ()))
    ones = jnp.ones((SUBLANES, L), F32)
    units = [(bi, h) for bi in range(nb) for h in range(ML_HEADS)]

    gates, cums = [], []
    for bi in range(nb):
        pre = s_ref[bi] + bias_ref[...]
        logsig = jnp.minimum(pre, 0.0) - jnp.log1p(jnp.exp(-jnp.abs(pre)))
        gate = jnp.where(lane < ML_HEADS, pre, logsig)
        gates.append(gate)
        cums.append(jnp.dot(lower, gate, precision=lax.Precision.HIGHEST, preferred_element_type=F32))

    qs, ks, states, scores, qcs = {}, {}, {}, {}, {}
    for bi, h in units:
        sl = slice(h * ML_DH, (h + 1) * ML_DH)
        qs[bi, h] = q_ref[bi, :, sl].astype(BF16)
        ks[bi, h] = (k_ref[bi, :, sl] * (ML_DH ** -0.5)).astype(BF16)
        states[bi, h] = c_ref[bi, h]
        scores[bi, h] = lax.dot_general(ks[bi, h], qs[bi, h], nt, preferred_element_type=F32)
        qcs[bi, h] = lax.dot_general(states[bi, h].astype(BF16), qs[bi, h], nt, preferred_element_type=F32)

    gates_t = [g.T for g in gates]
    cums_t = [cu.T for cu in cums]
    v_t = [v_ref[bi].T for bi in range(nb)]
    o_t = [jax.nn.sigmoid(o_ref[bi]).T for bi in range(nb)]

    s_t, m_rows, w_inters, nums, d_states, decays = {}, {}, {}, {}, {}, {}
    for bi, h in units:
        sl = slice(h * ML_DH, (h + 1) * ML_DH)
        a_col = gates[bi][:, h:h + 1] - cums[bi][:, ML_HEADS + h:ML_HEADS + h + 1]
        b_row = cums_t[bi][ML_HEADS + h:ML_HEADS + h + 1, :]
        li_row = gates_t[bi][h:h + 1, :]
        m_prev = m_ref[bi, :, h:h + 1]
        dmat = jnp.where(causal_t, a_col + b_row, -jnp.inf)
        inter = b_row + m_prev
        m_row = jnp.maximum(jnp.max(dmat, axis=0, keepdims=True), inter)
        w_inters[bi, h] = jnp.exp(inter - m_row)
        m_rows[bi, h] = m_row
        s_t[bi, h] = scores[bi, h] * jnp.exp(dmat - m_row)
        b_last = b_row[:, L - 1:L]
        g_row = b_last - b_row + li_row
        m_new = jnp.maximum(b_last + m_prev, jnp.max(g_row, axis=1, keepdims=True))
        decays[bi, h] = jnp.exp(b_last + m_prev - m_new)
        m_ref[bi, :, h:h + 1] = m_new
        vt_ext = jnp.concatenate([v_t[bi][sl, :], ones], axis=0) * jnp.exp(g_row - m_new)
        nums[bi, h] = jnp.dot(v_t[bi][sl, :].astype(BF16), s_t[bi, h].astype(BF16), preferred_element_type=F32)
        d_states[bi, h] = jnp.dot(vt_ext.astype(BF16), ks[bi, h], preferred_element_type=F32)

    for bi in range(nb):
        ys = []
        for h in range(ML_HEADS):
            sl = slice(h * ML_DH, (h + 1) * ML_DH)
            qc, w_inter = qcs[bi, h], w_inters[bi, h]
            den = jnp.sum(s_t[bi, h], axis=0, keepdims=True) + w_inter * qc[ML_DH:ML_DH + 1, :]
            hh = (nums[bi, h] + w_inter * qc[:ML_DH, :]) / jnp.maximum(jnp.abs(den), jnp.exp(-m_rows[bi, h]))
            y = o_t[bi][sl, :] * hh
            ys.append(y * lax.rsqrt(jnp.mean(y * y, axis=0, keepdims=True) + EPS) * g_ref[sl, :])
            c_ref[bi, h] = decays[bi, h] * states[bi, h] + d_states[bi, h]
        y_ref[bi] = jnp.concatenate(ys, axis=0).T


def mlstm(proj3, bias, g):
    B, S, _ = proj3.shape
    L = ML_CHUNK
    nb = ML_NB
    col = lambda j: pl.BlockSpec((nb, L, ML_W), lambda b, c, j=j: (b, c, j))
    return pl.pallas_call(
        _mlstm_kernel,
        grid=(B // nb, S // L),
        in_specs=[col(0), col(1), col(2), col(3),
                  pl.BlockSpec((nb, L, LANES), lambda b, c: (b, c, SMALL_COL // LANES)),
                  pl.BlockSpec((1, LANES), lambda b, c: (0, 0)),
                  pl.BlockSpec((ML_W, 1), lambda b, c: (0, 0))],
        out_specs=pl.BlockSpec((nb, L, ML_W), lambda b, c: (b, c, 0)),
        out_shape=jax.ShapeDtypeStruct((B, S, ML_W), F32),
        scratch_shapes=[pltpu.VMEM((nb, ML_HEADS, ML_DH + SUBLANES, ML_DH), F32),
                        pltpu.VMEM((nb, 1, LANES), F32)],
        compiler_params=_cp(("arbitrary", "arbitrary")),
        name="mlstm",
    )(proj3, proj3, proj3, proj3, proj3, bias, g)


def _rglru_kernel(x_ref, gate_ref, cw_ref, cb_ref, wg_ref, bg_ref, lam_ref, g_ref, y_ref,
                  xp_ref, a_ref, u_ref, h_ref, carry_ref):
    B, tc, W = x_ref.shape
    c = pl.program_id(0)

    @pl.when(c == 0)
    def _():
        xp_ref[:, 0:SUBLANES, :] = jnp.zeros((B, SUBLANES, W), F32)
        carry_ref[...] = jnp.zeros_like(carry_ref)

    xp_ref[:, SUBLANES:, :] = x_ref[...]
    xc = jnp.zeros((B, tc, W), F32) + cb_ref[...]
    for j in range(CONV_W):
        off = SUBLANES - (CONV_W - 1) + j
        xc = xc + cw_ref[j:j + 1, :] * xp_ref[:, off:off + tc, :]
    xp_ref[:, 0:SUBLANES, :] = x_ref[:, tc - SUBLANES:, :]

    xc2 = xc.reshape(B * tc, W)
    ri = jax.nn.sigmoid(jnp.dot(xc2.astype(BF16), wg_ref[...], preferred_element_type=F32) + bg_ref[...])
    r = ri[:, :W]
    ig = ri[:, W:]
    lam = lam_ref[...]
    softplus_neg = jnp.maximum(-lam, 0.0) + jnp.log1p(jnp.exp(-jnp.abs(lam)))
    log_a = -LRU_C * r * softplus_neg
    a = jnp.exp(log_a)
    u = jnp.sqrt(1.0 - jnp.exp(2.0 * log_a)) * (ig * xc2)
    halves = W // LANES
    for k in range(halves):
        a_ref[k] = a[:, k * LANES:(k + 1) * LANES]
        u_ref[k] = u[:, k * LANES:(k + 1) * LANES]

    def step(i, hs):
        base = pl.multiple_of(i * SUBLANES, SUBLANES)
        for j in range(SUBLANES):
            rows = pl.ds(base + j, B, stride=tc)
            hs = tuple(a_ref.at[k][rows, :] * hs[k] + u_ref.at[k][rows, :] for k in range(halves))
            for k in range(halves):
                h_ref.at[k][rows, :] = hs[k]
        return hs

    init = tuple(carry_ref[k] for k in range(halves))
    last = lax.fori_loop(0, tc // SUBLANES, step, init)
    for k in range(halves):
        carry_ref[k] = last[k]
    h_all = jnp.concatenate([h_ref[k] for k in range(halves)], axis=1)
    y = h_all.reshape(B, tc, W) * jax.nn.gelu(gate_ref[...])
    y_ref[...] = _rms(y, g_ref[...])


def rglru(proj3, cw, cb, wg, bg, lam, g):
    B, S, _ = proj3.shape
    tc = LRU_TC
    vec = lambda n: pl.BlockSpec((1, n), lambda c: (0, 0))
    return pl.pallas_call(
        _rglru_kernel,
        grid=(S // tc,),
        in_specs=[pl.BlockSpec((B, tc, LRU_W), lambda c: (0, c, 4)),
                  pl.BlockSpec((B, tc, LRU_W), lambda c: (0, c, 5)),
                  pl.BlockSpec((CONV_W, LRU_W), lambda c: (0, 0)),
                  vec(LRU_W),
                  pl.BlockSpec((LRU_W, 2 * LRU_W), lambda c: (0, 0)),
                  vec(2 * LRU_W), vec(LRU_W), vec(LRU_W)],
        out_specs=pl.BlockSpec((B, tc, LRU_W), lambda c: (0, c, 0)),
        out_shape=jax.ShapeDtypeStruct((B, S, LRU_W), F32),
        scratch_shapes=[pltpu.VMEM((B, tc + SUBLANES, LRU_W), F32),
                        pltpu.VMEM((LRU_W // LANES, B * tc, LANES), F32),
                        pltpu.VMEM((LRU_W // LANES, B * tc, LANES), F32),
                        pltpu.VMEM((LRU_W // LANES, B * tc, LANES), F32),
                        pltpu.VMEM((LRU_W // LANES, B, LANES), F32)],
        compiler_params=_cp(("arbitrary",)),
        name="rglru",
    )(proj3, proj3, cw, cb, wg, bg, lam, g)


def _rot_half(blk):
    lane = lax.broadcasted_iota(jnp.int32, blk.shape, 1)
    return jnp.where(lane < NOPE + ROPE // 2,
                     pltpu.roll(blk, LANES - ROPE // 2, 1), pltpu.roll(blk, ROPE // 2, 1))


def _mla_proj_kernel(cq_ref, ckv_ref, qg_ref, kvg_ref, wq_ref, wk_ref, wvt_ref, vones_ref,
                     cq_t, sq_t, ck_t, sk_t, q_ref, k_ref, vt_ref):
    qn = _rms(cq_ref[...], qg_ref[...]).astype(BF16)
    qf = jnp.dot(qn, wq_ref[...], preferred_element_type=F32)
    kvn = _rms(ckv_ref[:, :KV_LORA], kvg_ref[...]).astype(BF16)
    kf = jnp.dot(kvn, wk_ref[...], preferred_element_type=F32)
    vt_ref[...] = (lax.dot_general(wvt_ref[...], kvn, (((1,), (1,)), ((), ())), preferred_element_type=F32)
                   + vones_ref[...]).astype(BF16)
    small = ckv_ref[:, KV_LORA:]
    kr = small * ck_t[...] + _rot_half(small) * sk_t[...]
    cq = cq_t[...]
    sq = sq_t[...]
    for h in range(MLA_H):
        sl = slice(h * LANES, (h + 1) * LANES)
        blk = qf[:, sl]
        q_ref[:, sl] = (blk * cq + _rot_half(blk) * sq).astype(BF16)
        k_ref[:, sl] = (kf[:, sl] + kr).astype(BF16)


def mla_proj(proj, qg, kvg, wq, wk, wvt, vones, tabs, S):
    T = proj.shape[0]
    tm = TM_IN
    npos = S // tm
    tab = pl.BlockSpec((tm, LANES), lambda i: (i % npos, 0))
    full = lambda a: pl.BlockSpec(a.shape, lambda i: (0, 0))
    return pl.pallas_call(
        _mla_proj_kernel,
        grid=(T // tm,),
        in_specs=[pl.BlockSpec((tm, Q_LORA), lambda i: (i, 6)),
                  pl.BlockSpec((tm, 2 * KV_LORA), lambda i: (i, 7)),
                  full(qg), full(kvg), full(wq), full(wk), full(wvt), full(vones), tab, tab, tab, tab],
        out_specs=[pl.BlockSpec((tm, MLA_H * LANES), lambda i: (i, 0)),
                   pl.BlockSpec((tm, MLA_H * LANES), lambda i: (i, 0)),
                   pl.BlockSpec((MLA_H * V_EXT, tm), lambda i: (0, i))],
        out_shape=[jax.ShapeDtypeStruct((T, MLA_H * LANES), BF16),
                   jax.ShapeDtypeStruct((T, MLA_H * LANES), BF16),
                   jax.ShapeDtypeStruct((MLA_H * V_EXT, T), BF16)],
        compiler_params=_cp(("arbitrary",)),
        name="mla_proj",
    )(proj, proj, qg, kvg, wq, wk, wvt, vones, *tabs)


def _attn_kernel(q_ref, k_ref, vt_ref, g_ref, o_ref, acc_ref, m_ref, yt_ref):
    tq = q_ref.shape[0]
    i = pl.program_id(1)
    key_i = lax.broadcasted_iota(jnp.int32, (tq, tq), 0)
    qry_i = lax.broadcasted_iota(jnp.int32, (tq, tq), 1)
    causal = key_i <= qry_i
    nt = (((1,), (1,)), ((), ()))
    m_ref[...] = jnp.full_like(m_ref, -jnp.inf)
    acc_ref[...] = jnp.zeros_like(acc_ref)

    ahead = 4

    def scores(off, h):
        hsl = slice(h * LANES, (h + 1) * LANES)
        return lax.dot_general(k_ref[pl.ds(off, tq), hsl], q_ref[:, hsl], nt, preferred_element_type=F32)

    def update(h, alpha, pv):
        rows = slice(h * V_EXT, (h + 1) * V_EXT)
        acc_ref[rows, :] = alpha * acc_ref[rows, :] + pv

    def block(j, masked):
        off = pl.multiple_of(j * tq, tq)
        pending = [scores(off, h) for h in range(ahead)]
        late = None
        for h in range(MLA_H):
            s = pending.pop(0)
            if h + ahead < MLA_H:
                pending.append(scores(off, h + ahead))
            if masked:
                s = jnp.where(causal, s, -jnp.inf)
            m = m_ref[h:h + 1, :]
            m_new = jnp.maximum(m, jnp.max(s, axis=0, keepdims=True))
            m_ref[h:h + 1, :] = m_new
            alpha = jnp.exp2(m - m_new)
            p = jnp.exp2(s - m_new).astype(BF16)
            pv = jnp.dot(vt_ref[h * V_EXT:(h + 1) * V_EXT, pl.ds(off, tq)], p, preferred_element_type=F32)
            if late is not None:
                update(*late)
            late = (h, alpha, pv)
        update(*late)

    def body(j, carry):
        block(j, False)
        return carry

    lax.fori_loop(0, i, body, 0)
    block(i, True)

    ssq = jnp.zeros((1, tq), F32)
    for h in range(MLA_H):
        vrows = slice(h * MLA_V, (h + 1) * MLA_V)
        yh = acc_ref[h * V_EXT:h * V_EXT + MLA_V, :] * (1.0 / acc_ref[h * V_EXT + MLA_V:h * V_EXT + MLA_V + 1, :])
        yt_ref[vrows, :] = yh
        ssq = ssq + jnp.sum(yh * yh, axis=0, keepdims=True)
    yt = yt_ref[...] * lax.rsqrt(ssq * (1.0 / MLA_W) + EPS) * g_ref[...]
    o_ref[...] = yt.T


def attention(q, k, vt, g, B, S):
    T = B * S
    nq = S // TQ
    return pl.pallas_call(
        _attn_kernel,
        grid=(B, nq),
        in_specs=[pl.BlockSpec((TQ, MLA_H * LANES), lambda b, i: (b * nq + i, 0)),
                  pl.BlockSpec((S, MLA_H * LANES), lambda b, i: (b, 0)),
                  pl.BlockSpec((MLA_H * V_EXT, S), lambda b, i: (0, b)),
                  pl.BlockSpec((MLA_W, 1), lambda b, i: (0, 0))],
        out_specs=pl.BlockSpec((TQ, MLA_W), lambda b, i: (b * nq + i, 0)),
        out_shape=jax.ShapeDtypeStruct((T, MLA_W), F32),
        scratch_shapes=[pltpu.VMEM((MLA_H * V_EXT, TQ), F32),
                        pltpu.VMEM((MLA_H, TQ), F32),
                        pltpu.VMEM((MLA_W, TQ), F32)],
        compiler_params=_cp(("arbitrary", "arbitrary")),
        name="attention",
    )(q, k, vt, g)


def _out_proj_kernel(x_ref, ym_ref, yr_ref, ya_ref, w_ref, g_ref, wr_ref, br_ref,
                     xm_ref, t_ref, lg_ref):
    tm = x_ref.shape[0]
    acc = jnp.dot(ym_ref[...].astype(BF16), w_ref[0:ML_W, :], preferred_element_type=F32)
    acc += jnp.dot(yr_ref[...].astype(BF16), w_ref[ML_W:ML_W + LRU_W, :], preferred_element_type=F32)
    acc += jnp.dot(ya_ref[...].astype(BF16), w_ref[ML_W + LRU_W:, :], preferred_element_type=F32)
    xm = x_ref[...] + acc
    xm_ref[...] = xm
    t = _rms(xm, g_ref[...])
    lg_ref[...] = jnp.dot(t, wr_ref[...], precision=lax.Precision.HIGHEST,
                          preferred_element_type=F32) + br_ref[...]
    for s in range(TOK_ROWS):
        t_ref[pl.ds(s, tm, stride=TOK_ROWS), :] = t[:, s * LANES:(s + 1) * LANES]


def out_proj(x2d, ym, yr, ya, w, g, wr, br):
    T = x2d.shape[0]
    tm = TM_IN
    row = lambda n: pl.BlockSpec((tm, n), lambda i: (i, 0))
    full = lambda a: pl.BlockSpec(a.shape, lambda i: (0, 0))
    return pl.pallas_call(
        _out_proj_kernel,
        grid=(T // tm,),
        in_specs=[row(D_MODEL), row(ML_W), row(LRU_W), row(MLA_W), full(w), full(g), full(wr), full(br)],
        out_specs=[row(D_MODEL),
                   pl.BlockSpec((tm * TOK_ROWS, LANES), lambda i: (i, 0)),
                   row(LANES)],
        out_shape=[jax.ShapeDtypeStruct((T, D_MODEL), F32),
                   jax.ShapeDtypeStruct((T * TOK_ROWS, LANES), F32),
                   jax.ShapeDtypeStruct((T, LANES), F32)],
        compiler_params=_cp(("arbitrary",)),
        name="out_proj",
    )(x2d, ym, yr, ya, w, g, wr, br)


def _route_kernel(lg_ref, meta_ref, cnt_ref):
    st = lg_ref.shape[0]
    lg = lg_ref[...]
    lane = lax.broadcasted_iota(jnp.int32, (st, LANES), 1)
    row_max = lambda a: jnp.max(a, axis=1, keepdims=True)
    row_sum = lambda a: jnp.sum(a, axis=1, keepdims=True)
    first = lambda cond: jnp.min(jnp.where(cond, lane, LANES), axis=1, keepdims=True)

    gmask = lane < N_GROUPS
    gl = jnp.where(gmask, lg, -jnp.inf)
    ge = jnp.exp(gl - row_max(gl))
    gp = ge / row_sum(ge)
    g_val = row_max(gp)
    g_idx = first(gmask & (gp == g_val))

    smask = (lane >= N_GROUPS) & (lane < N_GROUPS + N_EXP) & (((lane - N_GROUPS) >> 3) == g_idx)
    el = jnp.where(smask, lg, -jnp.inf)
    ee = jnp.exp(el - row_max(el))
    ep = ee / row_sum(ee)
    v1 = row_max(ep)
    i1 = first(smask & (ep == v1))
    rest = smask & (lane != i1)
    v2 = row_max(jnp.where(rest, ep, -1.0))
    i2 = first(rest & (ep == v2))
    den = v1 + v2
    w1 = v1 / den * g_val
    w2 = v2 / den * g_val

    onehot = ((lane == i1) | (lane == i2)).astype(F32)
    sb = min(st, 256)
    r_i = lax.broadcasted_iota(jnp.int32, (sb, sb), 0)
    c_i = lax.broadcasted_iota(jnp.int32, (sb, sb), 1)
    tri = (c_i < r_i).astype(BF16)
    run = jnp.zeros((1, LANES), F32)
    ranks = []
    for j in range(st // sb):
        oh = onehot[j * sb:(j + 1) * sb]
        ranks.append(jnp.dot(tri, oh.astype(BF16), preferred_element_type=F32) + run)
        run = run + jnp.sum(oh, axis=0, keepdims=True)
    rank = jnp.concatenate(ranks, axis=0)
    u_r = lax.broadcasted_iota(jnp.int32, (LANES, LANES), 0)
    u_c = lax.broadcasted_iota(jnp.int32, (LANES, LANES), 1)
    lstart = jnp.dot(jnp.broadcast_to(run, (SUBLANES, LANES)), (u_r < u_c).astype(F32),
                     precision=lax.Precision.HIGHEST, preferred_element_type=F32)[0:1]
    slot = rank + lstart
    p1 = row_sum(jnp.where(lane == i1, slot, 0.0))
    p2 = row_sum(jnp.where(lane == i2, slot, 0.0))
    meta = jnp.where(lane == 0, p1, jnp.where(lane == 1, p2, jnp.where(lane == 2, w1,
                                                                      jnp.where(lane == 3, w2, 0.0))))
    meta_ref[...] = meta.T[0:SUBLANES, :]
    sub = lax.broadcasted_iota(jnp.int32, (SUBLANES, LANES), 0)
    cnt_ref[...] = jnp.where(sub == 0, run, jnp.where(sub == 1, lstart, 0.0))


def route(logits):
    T = logits.shape[0]
    n_st = T // ST
    return pl.pallas_call(
        _route_kernel,
        grid=(n_st,),
        in_specs=[pl.BlockSpec((ST, LANES), lambda i: (i, 0))],
        out_specs=[pl.BlockSpec((SUBLANES, ST), lambda i: (i, 0)),
                   pl.BlockSpec((SUBLANES, LANES), lambda i: (i, 0))],
        out_shape=[jax.ShapeDtypeStruct((n_st * SUBLANES, ST), F32),
                   jax.ShapeDtypeStruct((n_st * SUBLANES, LANES), F32)],
        compiler_params=_cp(("arbitrary",)),
        name="route",
    )(logits)


def _copy_by_bits(n, nbits, copy_at, start):
    off = jnp.int32(0)
    for bit in reversed(range(nbits)):
        sz = 1 << bit

        @pl.when((n & sz) != 0)
        def _(off=off, sz=sz):
            cp = copy_at(off, sz)
            if start:
                cp.start()
            else:
                cp.wait()

        off = off + (n & sz)


def _strip_copies(st, cnt_ref, lst_ref, gst_ref, make_copy, start):
    def body(e, carry):
        loc = lst_ref[st * N_EXP + e]
        glob = gst_ref[st * N_EXP + e]
        _copy_by_bits(
            cnt_ref[st * N_EXP + e], int(np.log2(ST)) + 1,
            lambda off, sz: make_copy(pl.multiple_of((loc + off) * TOK_ROWS, TOK_ROWS),
                                      pl.multiple_of((glob + off) * TOK_ROWS, TOK_ROWS), sz * TOK_ROWS),
            start)
        return carry

    lax.fori_loop(0, N_EXP, body, 0)


def _zero_padding(pad_ref, xs_ref, zero_ref, sem, start):
    tile_rows = TM_E * TOK_ROWS

    def body(e, carry):
        first = pad_ref[e]
        _copy_by_bits(
            pad_ref[N_EXP + e], int(np.log2(TM_E)),
            lambda off, sz: pltpu.make_async_copy(
                zero_ref.at[pl.ds(0, sz * TOK_ROWS)],
                xs_ref.at[pl.ds(pl.multiple_of((first + off) * TOK_ROWS, TOK_ROWS), sz * TOK_ROWS)], sem),
            start)
        return carry

    lax.fori_loop(0, N_EXP, body, 0)

    def tail(i, carry):
        cp = pltpu.make_async_copy(
            zero_ref, xs_ref.at[pl.ds(pl.multiple_of(i * tile_rows, tile_rows), tile_rows)], sem)
        if start:
            cp.start()
        else:
            cp.wait()
        return carry

    lax.fori_loop(pad_ref[2 * N_EXP], xs_ref.shape[0] // tile_rows, tail, 0)


def _dispatch_kernel(lpos_ref, cnt_ref, lst_ref, gst_ref, pad_ref, t_ref, xs_ref,
                     stage_ref, zero_ref, sem, zsem):
    st = pl.program_id(0)
    base = st * 2 * ST

    @pl.when(st == 0)
    def _():
        zero_ref[...] = jnp.zeros_like(zero_ref)
        _zero_padding(pad_ref, xs_ref, zero_ref, zsem, True)

    def scatter(i, carry):
        for u in range(SUBLANES):
            t = i * SUBLANES + u
            tile = t_ref[pl.ds(pl.multiple_of(t * TOK_ROWS, TOK_ROWS), TOK_ROWS), :]
            p0 = lpos_ref[base + t]
            p1 = lpos_ref[base + ST + t]
            stage_ref[pl.ds(pl.multiple_of(p0 * TOK_ROWS, TOK_ROWS), TOK_ROWS), :] = tile
            stage_ref[pl.ds(pl.multiple_of(p1 * TOK_ROWS, TOK_ROWS), TOK_ROWS), :] = tile
        return carry

    lax.fori_loop(0, ST // SUBLANES, scatter, 0)

    def make_copy(loc, glob, rows):
        return pltpu.make_async_copy(stage_ref.at[pl.ds(loc, rows)], xs_ref.at[pl.ds(glob, rows)], sem)

    _strip_copies(st, cnt_ref, lst_ref, gst_ref, make_copy, True)
    _strip_copies(st, cnt_ref, lst_ref, gst_ref, make_copy, False)

    @pl.when(st == 0)
    def _():
        _zero_padding(pad_ref, xs_ref, zero_ref, zsem, False)


def dispatch(lpos, cnt, lst, gst, pad, t_tiles, n_rows):
    T = t_tiles.shape[0] // TOK_ROWS
    return pl.pallas_call(
        _dispatch_kernel,
        grid_spec=pltpu.PrefetchScalarGridSpec(
            num_scalar_prefetch=5,
            grid=(T // ST,),
            in_specs=[pl.BlockSpec((ST * TOK_ROWS, LANES), lambda i, *_: (i, 0))],
            out_specs=pl.BlockSpec(memory_space=pl.ANY),
            scratch_shapes=[pltpu.VMEM((2 * ST * TOK_ROWS, LANES), F32),
                            pltpu.VMEM((TM_E * TOK_ROWS, LANES), F32),
                            pltpu.SemaphoreType.DMA(()),
                            pltpu.SemaphoreType.DMA(())]),
        out_shape=jax.ShapeDtypeStruct((n_rows * TOK_ROWS, LANES), F32),
        compiler_params=_cp(("arbitrary",)),
        name="dispatch",
    )(lpos, cnt, lst, gst, pad, t_tiles)


def _combine_kernel(lpos_ref, cnt_ref, lst_ref, gst_ref, w_ref, ys_ref, xm_ref, fg_ref, o_ref,
                    stage_ref, comb_ref, sem, *, final):
    st = pl.program_id(0)
    base = st * 2 * ST

    def make_copy(loc, glob, rows):
        return pltpu.make_async_copy(ys_ref.at[pl.ds(glob, rows)], stage_ref.at[pl.ds(loc, rows)], sem)

    _strip_copies(st, cnt_ref, lst_ref, gst_ref, make_copy, True)
    _strip_copies(st, cnt_ref, lst_ref, gst_ref, make_copy, False)

    def gather(i, carry):
        for u in range(SUBLANES):
            t = i * SUBLANES + u
            p0 = lpos_ref[base + t]
            p1 = lpos_ref[base + ST + t]
            y0 = stage_ref[pl.ds(pl.multiple_of(p0 * TOK_ROWS, TOK_ROWS), TOK_ROWS), :]
            y1 = stage_ref[pl.ds(pl.multiple_of(p1 * TOK_ROWS, TOK_ROWS), TOK_ROWS), :]
            comb_ref[pl.ds(pl.multiple_of(t * TOK_ROWS, TOK_ROWS), TOK_ROWS), :] = (
                w_ref[0, 0, t] * y0 + w_ref[0, 1, t] * y1)
        return carry

    lax.fori_loop(0, ST // SUBLANES, gather, 0)
    moe = jnp.concatenate([comb_ref[pl.ds(s, ST, stride=TOK_ROWS), :] for s in range(TOK_ROWS)], axis=1)
    x_new = xm_ref[...] + moe
    o_ref[...] = _rms(x_new, fg_ref[...]) if final else x_new


def combine(lpos, cnt, lst, gst, w, ys, xm, fg, final):
    T = xm.shape[0]
    return pl.pallas_call(
        functools.partial(_combine_kernel, final=final),
        grid_spec=pltpu.PrefetchScalarGridSpec(
            num_scalar_prefetch=4,
            grid=(T // ST,),
            in_specs=[pl.BlockSpec((1, 2, ST), lambda i, *_: (i, 0, 0), memory_space=pltpu.SMEM),
                      pl.BlockSpec(memory_space=pl.ANY),
                      pl.BlockSpec((ST, D_MODEL), lambda i, *_: (i, 0)),
                      pl.BlockSpec((1, D_MODEL), lambda i, *_: (0, 0))],
            out_specs=pl.BlockSpec((ST, D_MODEL), lambda i, *_: (i, 0)),
            scratch_shapes=[pltpu.VMEM((2 * ST * TOK_ROWS, LANES), F32),
                            pltpu.VMEM((ST * TOK_ROWS, LANES), F32),
                            pltpu.SemaphoreType.DMA(())]),
        out_shape=jax.ShapeDtypeStruct((T, D_MODEL), F32),
        compiler_params=_cp(("arbitrary",)),
        name="combine",
    )(lpos, cnt, lst, gst, w, ys, xm, fg)


def _expert_kernel(te_ref, nused_ref, xs_ref, wg_ref, wu_ref, wd_ref, ys_ref):
    i = pl.program_id(0)
    tm = xs_ref.shape[0] // TOK_ROWS

    @pl.when(i < nused_ref[0])
    def _():
        x = jnp.concatenate([xs_ref[pl.ds(s, tm, stride=TOK_ROWS), :] for s in range(TOK_ROWS)],
                            axis=1).astype(BF16)
        g = jnp.dot(x, wg_ref[0, 0].astype(BF16), preferred_element_type=F32)
        u = jnp.dot(x, wu_ref[0, 0].astype(BF16), preferred_element_type=F32)
        h = (g * jax.nn.sigmoid(g) * u).astype(BF16)
        y = jnp.dot(h, wd_ref[0, 0].astype(BF16), preferred_element_type=F32)
        for s in range(TOK_ROWS):
            ys_ref[pl.ds(s, tm, stride=TOK_ROWS), :] = y[:, s * LANES:(s + 1) * LANES]

    @pl.when(i >= nused_ref[0])
    def _():
        ys_ref[...] = jnp.zeros_like(ys_ref)


def experts(tile_expert, n_used, xs, wg, wu, wd, layer):
    n_tiles = xs.shape[0] // (TM_E * TOK_ROWS)
    tile = lambda i, te, nu: (jnp.minimum(i, nu[0] - 1), 0)
    return pl.pallas_call(
        _expert_kernel,
        grid_spec=pltpu.PrefetchScalarGridSpec(
            num_scalar_prefetch=2,
            grid=(n_tiles,),
            in_specs=[pl.BlockSpec((TM_E * TOK_ROWS, LANES), tile),
                      pl.BlockSpec((1, 1, D_MODEL, D_EXP), lambda i, te, nu: (layer, te[i], 0, 0)),
                      pl.BlockSpec((1, 1, D_MODEL, D_EXP), lambda i, te, nu: (layer, te[i], 0, 0)),
                      pl.BlockSpec((1, 1, D_EXP, D_MODEL), lambda i, te, nu: (layer, te[i], 0, 0))],
            out_specs=pl.BlockSpec((TM_E * TOK_ROWS, LANES), lambda i, te, nu: (i, 0))),
        out_shape=jax.ShapeDtypeStruct(xs.shape, F32),
        compiler_params=_cp(("arbitrary",)),
        name="experts",
    )(tile_expert, n_used, xs, wg, wu, wd)


def _rope_tables(S):
    inv = 1.0 / (ROPE_THETA ** (jnp.arange(0, ROPE, 2, dtype=F32) / ROPE))
    ang = jnp.arange(S, dtype=F32)[:, None] * inv[None, :]
    cos, sin = jnp.cos(ang), jnp.sin(ang)
    z = lambda n: jnp.zeros((S, n), F32)
    scale = (NOPE + ROPE) ** -0.5 * np.log2(np.e)
    pad = LANES - NOPE - ROPE
    cq = scale * jnp.concatenate([jnp.ones((S, NOPE), F32), cos, cos, z(pad)], axis=1)
    sq = scale * jnp.concatenate([z(NOPE), -sin, sin, z(pad)], axis=1)
    ck = jnp.concatenate([z(NOPE), cos, cos, z(pad)], axis=1)
    sk = jnp.concatenate([z(NOPE), -sin, sin, z(pad)], axis=1)
    return cq, sq, ck, sk


def _block_diag(w):
    n, d, _ = w.shape
    return (jnp.eye(n, dtype=w.dtype)[:, None, :, None] * w[:, :, None, :]).reshape(n * d, n * d)


def _in_weights(w_in):
    z = lambda n: jnp.zeros((D_MODEL, n), F32)
    o = 4 * ML_W
    mi, mf = w_in[:, o:o + ML_HEADS], w_in[:, o + ML_HEADS:o + 2 * ML_HEADS]
    o += 2 * ML_HEADS
    rest = w_in[:, o:o + 2 * LRU_W + Q_LORA + KV_LORA]
    kr = w_in[:, o + 2 * LRU_W + Q_LORA + KV_LORA:]
    small = jnp.concatenate([mi, mf, z(NOPE - 2 * ML_HEADS), kr, z(LANES - NOPE - ROPE)], axis=1)
    return jnp.concatenate([w_in[:, :4 * ML_W], rest, small], axis=1).astype(BF16)


def _routing_tables(cnt_rows):
    n_st = cnt_rows.shape[0] // SUBLANES
    rows = cnt_rows.reshape(n_st, SUBLANES, LANES)
    cnt = rows[:, 0, N_GROUPS:N_GROUPS + N_EXP].astype(jnp.int32)
    lst = rows[:, 1, N_GROUPS:N_GROUPS + N_EXP].astype(jnp.int32)
    tiles_e = (jnp.sum(cnt, axis=0) + TM_E - 1) // TM_E
    tile_cum = jnp.cumsum(tiles_e)
    base = (tile_cum - tiles_e) * TM_E
    gst = base[None, :] + jnp.cumsum(cnt, axis=0) - cnt
    n_used = tile_cum[-1:].astype(jnp.int32)
    n_e = jnp.sum(cnt, axis=0)
    pad = jnp.concatenate([base + n_e, tiles_e * TM_E - n_e, n_used]).astype(jnp.int32)
    return cnt.reshape(-1), lst.reshape(-1), gst.reshape(-1), pad, tile_cum, n_used


def kernel(x, norm1_g, w_in, ml_i_bias, ml_f_bias, ml_norm_g, conv_w, conv_b, lru_w_r, lru_b_r, lru_w_i, lru_b_i, lru_lambda, lru_norm_g, q_norm_g, w_uq, kv_norm_g, w_ukv, mla_norm_g, w_out, norm2_g, w_group, b_group, w_expert, b_expert, w_gate, w_up, w_down, final_norm_g):
    B, S, _ = x.shape
    T = B * S
    depth = w_in.shape[0]
    n_tiles = (2 * T) // TM_E + N_EXP
    tabs = _rope_tables(S)
    vones = jnp.tile(jnp.concatenate([jnp.zeros((MLA_V,), F32), jnp.ones((V_EXT - MLA_V,), F32)]),
                     MLA_H).reshape(MLA_H * V_EXT, 1)
    row = lambda a: a.reshape(1, -1)
    x2d = x.reshape(T, D_MODEL)
    for l in range(depth):
        proj = in_proj(x2d, row(norm1_g[l]), _in_weights(w_in[l]))
        gate_bias = jnp.concatenate([ml_i_bias[l], ml_f_bias[l], jnp.zeros((LANES - 2 * ML_HEADS,), F32)])
        proj3 = proj.reshape(B, S, PROJ_W)
        y_m = mlstm(proj3, row(gate_bias), ml_norm_g[l].reshape(ML_W, 1)).reshape(T, ML_W)
        w_gates = jnp.concatenate([_block_diag(lru_w_r[l]), _block_diag(lru_w_i[l])], axis=1).astype(BF16)
        b_gates = jnp.concatenate([lru_b_r[l], lru_b_i[l]])
        y_r = rglru(proj3, conv_w[l], row(conv_b[l]), w_gates, row(b_gates),
                    row(lru_lambda[l]), row(lru_norm_g[l])).reshape(T, LRU_W)
        wq = jnp.pad(w_uq[l].reshape(Q_LORA, MLA_H, NOPE + ROPE),
                     ((0, 0), (0, 0), (0, LANES - NOPE - ROPE))).reshape(Q_LORA, MLA_H * LANES).astype(BF16)
        wkv = w_ukv[l].reshape(KV_LORA, MLA_H, NOPE + MLA_V)
        wk = jnp.pad(wkv[:, :, :NOPE], ((0, 0), (0, 0), (0, LANES - NOPE))).reshape(KV_LORA, MLA_H * LANES).astype(BF16)
        wvt = jnp.pad(wkv[:, :, NOPE:], ((0, 0), (0, 0), (0, V_EXT - MLA_V))).reshape(KV_LORA, MLA_H * V_EXT).T.astype(BF16)
        q, k, vt = mla_proj(proj, row(q_norm_g[l]), row(kv_norm_g[l]), wq, wk, wvt, vones, tabs, S)
        y_a = attention(q, k, vt, mla_norm_g[l].reshape(MLA_W, 1), B, S)
        w_router = jnp.concatenate([w_group[l], w_expert[l],
                                    jnp.zeros((D_MODEL, LANES - N_GROUPS - N_EXP), F32)], axis=1)
        b_router = jnp.concatenate([b_group[l], b_expert[l], jnp.zeros((LANES - N_GROUPS - N_EXP,), F32)])
        x_mid, t_tiles, logits = out_proj(x2d, y_m, y_r, y_a, w_out[l].astype(BF16), row(norm2_g[l]),
                                          w_router, row(b_router))
        meta, cnt_rows = route(logits)
        meta = meta.reshape(T // ST, SUBLANES, ST)
        lpos = meta[:, 0:2, :].astype(jnp.int32).reshape(-1)
        w_pick = meta[:, 2:4, :]
        cnt, lst, gst, pad, tile_cum, n_used = _routing_tables(cnt_rows)
        tile_ids = jnp.minimum(jnp.arange(n_tiles, dtype=jnp.int32), n_used[0] - 1)
        tile_expert = jnp.minimum(jnp.sum(tile_ids[:, None] >= tile_cum[None, :], axis=1),
                                  N_EXP - 1).astype(jnp.int32)
        xs = dispatch(lpos, cnt, lst, gst, pad, t_tiles, n_tiles * TM_E)
        ys = experts(tile_expert, n_used, xs, w_gate, w_up, w_down, l)
        x2d = combine(lpos, cnt, lst, gst, w_pick, ys, x_mid, row(final_norm_g), l == depth - 1)
    return x2d.reshape(B, S, D_MODEL)
```

```python
import functools

import numpy as np
import jax
import jax.numpy as jnp
from jax import lax
from jax.experimental import pallas as pl
from jax.experimental.pallas import tpu as pltpu

F32 = jnp.float32
BF16 = jnp.bfloat16

D_MODEL = 1024
ML_HEADS = 4
ML_DH = 64
ML_W = ML_HEADS * ML_DH
LRU_W = 256
LRU_BLOCKS = 4
CONV_W = 4
LRU_C = 8.0
MLA_H = 8
NOPE = 64
ROPE = 32
MLA_V = 64
MLA_W = MLA_H * MLA_V
Q_LORA = 256
KV_LORA = 128
ROPE_THETA = 10000.0
N_GROUPS = 4
EPG = 8
N_EXP = N_GROUPS * EPG
D_EXP = 256
EPS = 1e-6

LANES = 128
SUBLANES = 8
V_EXT = MLA_V + SUBLANES
TOK_ROWS = D_MODEL // LANES
R_LOG = 40
PROJ_W = 2048
SMALL_COL = 1920
VMEM_LIMIT = 56 * 1024 * 1024

ML_CHUNK = 128
ML_NB = 2
LRU_TC = 256
TM_IN = 512
TQ = 256
ST = 1024
TM_E = 256


def _cp(sem):
    return pltpu.CompilerParams(dimension_semantics=sem, vmem_limit_bytes=VMEM_LIMIT)


def _rms(x, g):
    return x * lax.rsqrt(jnp.mean(x * x, axis=-1, keepdims=True) + EPS) * g


def _in_proj_kernel(x_ref, g_ref, w_ref, o_ref):
    y = _rms(x_ref[...], g_ref[...])
    o_ref[...] = jnp.dot(y.astype(BF16), w_ref[...], preferred_element_type=F32)


def in_proj(x2d, g, w):
    T = x2d.shape[0]
    return pl.pallas_call(
        _in_proj_kernel,
        grid=(T // TM_IN,),
        in_specs=[pl.BlockSpec((TM_IN, D_MODEL), lambda i: (i, 0)),
                  pl.BlockSpec((1, D_MODEL), lambda i: (0, 0)),
                  pl.BlockSpec((D_MODEL, PROJ_W), lambda i: (0, 0))],
        out_specs=pl.BlockSpec((TM_IN, PROJ_W), lambda i: (i, 0)),
        out_shape=jax.ShapeDtypeStruct((T, PROJ_W), F32),
        compiler_params=_cp(("arbitrary",)),
        name="in_proj",
    )(x2d, g, w)


def _mlstm_kernel(q_ref, k_ref, v_ref, o_ref, s_ref, bias_ref, g_ref, y_ref, c_ref, m_ref):
    nb, L, _ = q_ref.shape
    c = pl.program_id(1)

    @pl.when(c == 0)
    def _():
        c_ref[...] = jnp.zeros_like(c_ref)
        m_ref[...] = jnp.zeros_like(m_ref)

    lane = lax.broadcasted_iota(jnp.int32, (L, LANES), 1)
    row_i = lax.broadcasted_iota(jnp.int32, (L, L), 0)
    col_i = lax.broadcasted_iota(jnp.int32, (L, L), 1)
    lower = (col_i <= row_i).astype(F32)
    causal_t = row_i <= col_i
    nt = (((1,), (1,)), ((), ()))
    ones = jnp.ones((SUBLANES, L), F32)
    units = [(bi, h) for bi in range(nb) for h in range(ML_HEADS)]

    gates, cums = [], []
    for bi in range(nb):
        pre = s_ref[bi] + bias_ref[...]
        logsig = jnp.minimum(pre, 0.0) - jnp.log1p(jnp.exp(-jnp.abs(pre)))
        gate = jnp.where(lane < ML_HEADS, pre, logsig)
        gates.append(gate)
        cums.append(jnp.dot(lower, gate, precision=lax.Precision.HIGHEST, preferred_element_type=F32))

    qs, ks, states, scores, qcs = {}, {}, {}, {}, {}
    for bi, h in units:
        sl = slice(h * ML_DH, (h + 1) * ML_DH)
        qs[bi, h] = q_ref[bi, :, sl].astype(BF16)
        ks[bi, h] = (k_ref[bi, :, sl] * (ML_DH ** -0.5)).astype(BF16)
        states[bi, h] = c_ref[bi, h]
        scores[bi, h] = lax.dot_general(ks[bi, h], qs[bi, h], nt, preferred_element_type=F32)
        qcs[bi, h] = lax.dot_general(states[bi, h].astype(BF16), qs[bi, h], nt, preferred_element_type=F32)

    gates_t = [g.T for g in gates]
    cums_t = [cu.T for cu in cums]
    v_t = [v_ref[bi].T for bi in range(nb)]
    o_t = [jax.nn.sigmoid(o_ref[bi]).T for bi in range(nb)]

    s_t, m_rows, w_inters, nums, d_states, decays = {}, {}, {}, {}, {}, {}
    for bi, h in units:
        sl = slice(h * ML_DH, (h + 1) * ML_DH)
        a_col = gates[bi][:, h:h + 1] - cums[bi][:, ML_HEADS + h:ML_HEADS + h + 1]
        b_row = cums_t[bi][ML_HEADS + h:ML_HEADS + h + 1, :]
        li_row = gates_t[bi][h:h + 1, :]
        m_prev = m_ref[bi, :, h:h + 1]
        dmat = jnp.where(causal_t, a_col + b_row, -jnp.inf)
        inter = b_row + m_prev
        m_row = jnp.maximum(jnp.max(dmat, axis=0, keepdims=True), inter)
        w_inters[bi, h] = jnp.exp(inter - m_row)
        m_rows[bi, h] = m_row
        s_t[bi, h] = scores[bi, h] * jnp.exp(dmat - m_row)
        b_last = b_row[:, L - 1:L]
        g_row = b_last - b_row + li_row
        m_new = jnp.maximum(b_last + m_prev, jnp.max(g_row, axis=1, keepdims=True))
        decays[bi, h] = jnp.exp(b_last + m_prev - m_new)
        m_ref[bi, :, h:h + 1] = m_new
        vt_ext = jnp.concatenate([v_t[bi][sl, :], ones], axis=0) * jnp.exp(g_row - m_new)
        nums[bi, h] = jnp.dot(v_t[bi][sl, :].astype(BF16), s_t[bi, h].astype(BF16), preferred_element_type=F32)
        d_states[bi, h] = jnp.dot(vt_ext.astype(BF16), ks[bi, h], preferred_element_type=F32)

    for bi in range(nb):
        ys = []
        for h in range(ML_HEADS):
            sl = slice(h * ML_DH, (h + 1) * ML_DH)
            qc, w_inter = qcs[bi, h], w_inters[bi, h]
            den = jnp.sum(s_t[bi, h], axis=0, keepdims=True) + w_inter * qc[ML_DH:ML_DH + 1, :]
            hh = (nums[bi, h] + w_inter * qc[:ML_DH, :]) / jnp.maximum(jnp.abs(den), jnp.exp(-m_rows[bi, h]))
            y = o_t[bi][sl, :] * hh
            ys.append(y * lax.rsqrt(jnp.mean(y * y, axis=0, keepdims=True) + EPS) * g_ref[sl, :])
            c_ref[bi, h] = decays[bi, h] * states[bi, h] + d_states[bi, h]
        y_ref[bi] = jnp.concatenate(ys, axis=0).T


def mlstm(proj3, bias, g):
    B, S, _ = proj3.shape
    L = ML_CHUNK
    nb = ML_NB
    col = lambda j: pl.BlockSpec((nb, L, ML_W), lambda b, c, j=j: (b, c, j))
    return pl.pallas_call(
        _mlstm_kernel,
        grid=(B // nb, S // L),
        in_specs=[col(0), col(1), col(2), col(3),
                  pl.BlockSpec((nb, L, LANES), lambda b, c: (b, c, SMALL_COL // LANES)),
                  pl.BlockSpec((1, LANES), lambda b, c: (0, 0)),
                  pl.BlockSpec((ML_W, 1), lambda b, c: (0, 0))],
        out_specs=pl.BlockSpec((nb, L, ML_W), lambda b, c: (b, c, 0)),
        out_shape=jax.ShapeDtypeStruct((B, S, ML_W), F32),
        scratch_shapes=[pltpu.VMEM((nb, ML_HEADS, ML_DH + SUBLANES, ML_DH), F32),
                        pltpu.VMEM((nb, 1, LANES), F32)],
        compiler_params=_cp(("arbitrary", "arbitrary")),
        name="mlstm",
    )(proj3, proj3, proj3, proj3, proj3, bias, g)


def _rglru_kernel(x_ref, gate_ref, cw_ref, cb_ref, wg_ref, bg_ref, lam_ref, g_ref, y_ref,
                  xp_ref, a_ref, u_ref, h_ref, carry_ref):
    B, tc, W = x_ref.shape
    c = pl.program_id(0)

    @pl.when(c == 0)
    def _():
        xp_ref[:, 0:SUBLANES, :] = jnp.zeros((B, SUBLANES, W), F32)
        carry_ref[...] = jnp.zeros_like(carry_ref)

    xp_ref[:, SUBLANES:, :] = x_ref[...]
    xc = jnp.zeros((B, tc, W), F32) + cb_ref[...]
    for j in range(CONV_W):
        off = SUBLANES - (CONV_W - 1) + j
        xc = xc + cw_ref[j:j + 1, :] * xp_ref[:, off:off + tc, :]
    xp_ref[:, 0:SUBLANES, :] = x_ref[:, tc - SUBLANES:, :]

    xc2 = xc.reshape(B * tc, W)
    ri = jax.nn.sigmoid(jnp.dot(xc2.astype(BF16), wg_ref[...], preferred_element_type=F32) + bg_ref[...])
    r = ri[:, :W]
    ig = ri[:, W:]
    lam = lam_ref[...]
    softplus_neg = jnp.maximum(-lam, 0.0) + jnp.log1p(jnp.exp(-jnp.abs(lam)))
    log_a = -LRU_C * r * softplus_neg
    a = jnp.exp(log_a)
    u = jnp.sqrt(1.0 - jnp.exp(2.0 * log_a)) * (ig * xc2)
    halves = W // LANES
    for k in range(halves):
        a_ref[k] = a[:, k * LANES:(k + 1) * LANES]
        u_ref[k] = u[:, k * LANES:(k + 1) * LANES]

    def step(i, hs):
        base = pl.multiple_of(i * SUBLANES, SUBLANES)
        for j in range(SUBLANES):
            rows = pl.ds(base + j, B, stride=tc)
            hs = tuple(a_ref.at[k][rows, :] * hs[k] + u_ref.at[k][rows, :] for k in range(halves))
            for k in range(halves):
                h_ref.at[k][rows, :] = hs[k]
        return hs

    init = tuple(carry_ref[k] for k in range(halves))
    last = lax.fori_loop(0, tc // SUBLANES, step, init)
    for k in range(halves):
        carry_ref[k] = last[k]
    h_all = jnp.concatenate([h_ref[k] for k in range(halves)], axis=1)
    y = h_all.reshape(B, tc, W) * jax.nn.gelu(gate_ref[...])
    y_ref[...] = _rms(y, g_ref[...])


def rglru(proj3, cw, cb, wg, bg, lam, g):
    B, S, _ = proj3.shape
    tc = LRU_TC
    vec = lambda n: pl.BlockSpec((1, n), lambda c: (0, 0))
    return pl.pallas_call(
        _rglru_kernel,
        grid=(S // tc,),
        in_specs=[pl.BlockSpec((B, tc, LRU_W), lambda c: (0, c, 4)),
                  pl.BlockSpec((B, tc, LRU_W), lambda c: (0, c, 5)),
                  pl.BlockSpec((CONV_W, LRU_W), lambda c: (0, 0)),
                  vec(LRU_W),
                  pl.BlockSpec((LRU_W, 2 * LRU_W), lambda c: (0, 0)),
                  vec(2 * LRU_W), vec(LRU_W), vec(LRU_W)],
        out_specs=pl.BlockSpec((B, tc, LRU_W), lambda c: (0, c, 0)),
        out_shape=jax.ShapeDtypeStruct((B, S, LRU_W), F32),
        scratch_shapes=[pltpu.VMEM((B, tc + SUBLANES, LRU_W), F32),
                        pltpu.VMEM((LRU_W // LANES, B * tc, LANES), F32),
                        pltpu.VMEM((LRU_W // LANES, B * tc, LANES), F32),
                        pltpu.VMEM((LRU_W // LANES, B * tc, LANES), F32),
                        pltpu.VMEM((LRU_W // LANES, B, LANES), F32)],
        compiler_params=_cp(("arbitrary",)),
        name="rglru",
    )(proj3, proj3, cw, cb, wg, bg, lam, g)


def _rot_half(blk):
    lane = lax.broadcasted_iota(jnp.int32, blk.shape, 1)
    return jnp.where(lane < NOPE + ROPE // 2,
                     pltpu.roll(blk, LANES - ROPE // 2, 1), pltpu.roll(blk, ROPE // 2, 1))


def _mla_proj_kernel(cq_ref, ckv_ref, qg_ref, kvg_ref, wq_ref, wk_ref, wvt_ref, vones_ref,
                     cq_t, sq_t, ck_t, sk_t, q_ref, k_ref, vt_ref):
    qn = _rms(cq_ref[...], qg_ref[...]).astype(BF16)
    qf = jnp.dot(qn, wq_ref[...], preferred_element_type=F32)
    kvn = _rms(ckv_ref[:, :KV_LORA], kvg_ref[...]).astype(BF16)
    kf = jnp.dot(kvn, wk_ref[...], preferred_element_type=F32)
    vt_ref[...] = (lax.dot_general(wvt_ref[...], kvn, (((1,), (1,)), ((), ())), preferred_element_type=F32)
                   + vones_ref[...]).astype(BF16)
    small = ckv_ref[:, KV_LORA:]
    kr = small * ck_t[...] + _rot_half(small) * sk_t[...]
    cq = cq_t[...]
    sq = sq_t[...]
    for h in range(MLA_H):
        sl = slice(h * LANES, (h + 1) * LANES)
        blk = qf[:, sl]
        q_ref[:, sl] = (blk * cq + _rot_half(blk) * sq).astype(BF16)
        k_ref[:, sl] = (kf[:, sl] + kr).astype(BF16)


def mla_proj(proj, qg, kvg, wq, wk, wvt, vones, tabs, S):
    T = proj.shape[0]
    tm = TM_IN
    npos = S // tm
    tab = pl.BlockSpec((tm, LANES), lambda i: (i % npos, 0))
    full = lambda a: pl.BlockSpec(a.shape, lambda i: (0, 0))
    return pl.pallas_call(
        _mla_proj_kernel,
        grid=(T // tm,),
        in_specs=[pl.BlockSpec((tm, Q_LORA), lambda i: (i, 6)),
                  pl.BlockSpec((tm, 2 * KV_LORA), lambda i: (i, 7)),
                  full(qg), full(kvg), full(wq), full(wk), full(wvt), full(vones), tab, tab, tab, tab],
        out_specs=[pl.BlockSpec((tm, MLA_H * LANES), lambda i: (i, 0)),
                   pl.BlockSpec((tm, MLA_H * LANES), lambda i: (i, 0)),
                   pl.BlockSpec((MLA_H * V_EXT, tm), lambda i: (0, i))],
        out_shape=[jax.ShapeDtypeStruct((T, MLA_H * LANES), BF16),
                   jax.ShapeDtypeStruct((T, MLA_H * LANES), BF16),
                   jax.ShapeDtypeStruct((MLA_H * V_EXT, T), BF16)],
        compiler_params=_cp(("arbitrary",)),
        name="mla_proj",
    )(proj, proj, qg, kvg, wq, wk, wvt, vones, *tabs)


def _attn_kernel(q_ref, k_ref, vt_ref, g_ref, o_ref, acc_ref, m_ref, yt_ref):
    tq = q_ref.shape[0]
    i = pl.program_id(1)
    key_i = lax.broadcasted_iota(jnp.int32, (tq, tq), 0)
    qry_i = lax.broadcasted_iota(jnp.int32, (tq, tq), 1)
    causal = key_i <= qry_i
    nt = (((1,), (1,)), ((), ()))
    m_ref[...] = jnp.full_like(m_ref, -jnp.inf)
    acc_ref[...] = jnp.zeros_like(acc_ref)

    ahead = 4

    def scores(off, h):
        hsl = slice(h * LANES, (h + 1) * LANES)
        return lax.dot_general(k_ref[pl.ds(off, tq), hsl], q_ref[:, hsl], nt, preferred_element_type=F32)

    def update(h, alpha, pv):
        rows = slice(h * V_EXT, (h + 1) * V_EXT)
        acc_ref[rows, :] = alpha * acc_ref[rows, :] + pv

    def block(j, masked):
        off = pl.multiple_of(j * tq, tq)
        pending = [scores(off, h) for h in range(ahead)]
        late = None
        for h in range(MLA_H):
            s = pending.pop(0)
            if h + ahead < MLA_H:
                pending.append(scores(off, h + ahead))
            if masked:
                s = jnp.where(causal, s, -jnp.inf)
            m = m_ref[h:h + 1, :]
            m_new = jnp.maximum(m, jnp.max(s, axis=0, keepdims=True))
            m_ref[h:h + 1, :] = m_new
            alpha = jnp.exp2(m - m_new)
            p = jnp.exp2(s - m_new).astype(BF16)
            pv = jnp.dot(vt_ref[h * V_EXT:(h + 1) * V_EXT, pl.ds(off, tq)], p, preferred_element_type=F32)
            if late is not None:
                update(*late)
            late = (h, alpha, pv)
        update(*late)

    def body(j, carry):
        block(j, False)
        return carry

    lax.fori_loop(0, i, body, 0)
    block(i, True)

    ssq = jnp.zeros((1, tq), F32)
    for h in range(MLA_H):
        vrows = slice(h * MLA_V, (h + 1) * MLA_V)
        yh = acc_ref[h * V_EXT:h * V_EXT + MLA_V, :] * (1.0 / acc_ref[h * V_EXT + MLA_V:h * V_EXT + MLA_V + 1, :])
        yt_ref[vrows, :] = yh
        ssq = ssq + jnp.sum(yh * yh, axis=0, keepdims=True)
    yt = yt_ref[...] * lax.rsqrt(ssq * (1.0 / MLA_W) + EPS) * g_ref[...]
    o_ref[...] = yt.T


def attention(q, k, vt, g, B, S):
    T = B * S
    nq = S // TQ
    return pl.pallas_call(
        _attn_kernel,
        grid=(B, nq),
        in_specs=[pl.BlockSpec((TQ, MLA_H * LANES), lambda b, i: (b * nq + i, 0)),
                  pl.BlockSpec((S, MLA_H * LANES), lambda b, i: (b, 0)),
                  pl.BlockSpec((MLA_H * V_EXT, S), lambda b, i: (0, b)),
                  pl.BlockSpec((MLA_W, 1), lambda b, i: (0, 0))],
        out_specs=pl.BlockSpec((TQ, MLA_W), lambda b, i: (b * nq + i, 0)),
        out_shape=jax.ShapeDtypeStruct((T, MLA_W), F32),
        scratch_shapes=[pltpu.VMEM((MLA_H * V_EXT, TQ), F32),
                        pltpu.VMEM((MLA_H, TQ), F32),
                        pltpu.VMEM((MLA_W, TQ), F32)],
        compiler_params=_cp(("arbitrary", "arbitrary")),
        name="attention",
    )(q, k, vt, g)


def _out_proj_kernel(x_ref, ym_ref, yr_ref, ya_ref, w_ref, g_ref, wrh_ref, wrl_ref, br_ref,
                     xm_ref, t_ref, lg_ref):
    tm = x_ref.shape[0]
    acc = jnp.dot(ym_ref[...].astype(BF16), w_ref[0:ML_W, :], preferred_element_type=F32)
    acc += jnp.dot(yr_ref[...].astype(BF16), w_ref[ML_W:ML_W + LRU_W, :], preferred_element_type=F32)
    acc += jnp.dot(ya_ref[...].astype(BF16), w_ref[ML_W + LRU_W:, :], preferred_element_type=F32)
    xm = x_ref[...] + acc
    xm_ref[...] = xm
    t = _rms(xm, g_ref[...])
    t_hi = t.astype(BF16)
    t_lo = (t - t_hi.astype(F32)).astype(BF16)
    nt = (((1,), (1,)), ((), ()))
    lg = lax.dot_general(wrh_ref[...], t_hi, nt, preferred_element_type=F32)
    lg += lax.dot_general(wrl_ref[...], t_hi, nt, preferred_element_type=F32)
    lg += lax.dot_general(wrh_ref[...], t_lo, nt, preferred_element_type=F32)
    lg_ref[...] = lg + br_ref[...]
    for s in range(TOK_ROWS):
        t_ref[pl.ds(s, tm, stride=TOK_ROWS), :] = t[:, s * LANES:(s + 1) * LANES]


def out_proj(x2d, ym, yr, ya, w, g, wr_hi, wr_lo, br):
    T = x2d.shape[0]
    tm = TM_IN
    row = lambda n: pl.BlockSpec((tm, n), lambda i: (i, 0))
    full = lambda a: pl.BlockSpec(a.shape, lambda i: (0, 0))
    return pl.pallas_call(
        _out_proj_kernel,
        grid=(T // tm,),
        in_specs=[row(D_MODEL), row(ML_W), row(LRU_W), row(MLA_W), full(w), full(g),
                  full(wr_hi), full(wr_lo), full(br)],
        out_specs=[row(D_MODEL),
                   pl.BlockSpec((tm * TOK_ROWS, LANES), lambda i: (i, 0)),
                   pl.BlockSpec((R_LOG, tm), lambda i: (0, i))],
        out_shape=[jax.ShapeDtypeStruct((T, D_MODEL), F32),
                   jax.ShapeDtypeStruct((T * TOK_ROWS, LANES), F32),
                   jax.ShapeDtypeStruct((R_LOG, T), F32)],
        compiler_params=_cp(("arbitrary",)),
        name="out_proj",
    )(x2d, ym, yr, ya, w, g, wr_hi, wr_lo, br)


def _route_kernel(lg_ref, meta_ref, cnt_ref):
    st = lg_ref.shape[1]
    lg = lg_ref[...]
    row = lax.broadcasted_iota(jnp.int32, (R_LOG, st), 0)
    col_max = lambda a: jnp.max(a, axis=0, keepdims=True)
    col_sum = lambda a: jnp.sum(a, axis=0, keepdims=True)
    first = lambda cond: jnp.min(jnp.where(cond, row, R_LOG), axis=0, keepdims=True)

    gmask = row < N_GROUPS
    gl = jnp.where(gmask, lg, -jnp.inf)
    ge = jnp.exp(gl - col_max(gl))
    gp = ge / col_sum(ge)
    g_val = col_max(gp)
    g_idx = first(gmask & (gp == g_val))

    smask = (row >= N_GROUPS) & (row < N_GROUPS + N_EXP) & (((row - N_GROUPS) >> 3) == g_idx)
    el = jnp.where(smask, lg, -jnp.inf)
    ee = jnp.exp(el - col_max(el))
    ep = ee / col_sum(ee)
    v1 = col_max(ep)
    i1 = first(smask & (ep == v1))
    rest = smask & (row != i1)
    v2 = col_max(jnp.where(rest, ep, -1.0))
    i2 = first(rest & (ep == v2))
    den = v1 + v2
    w1 = v1 / den * g_val
    w2 = v2 / den * g_val

    onehot = ((row == i1) | (row == i2)).astype(F32)
    sb = min(st, 256)
    r_i = lax.broadcasted_iota(jnp.int32, (sb, sb), 0)
    c_i = lax.broadcasted_iota(jnp.int32, (sb, sb), 1)
    before = (r_i < c_i).astype(BF16)
    run = jnp.zeros((R_LOG, 1), F32)
    ranks = []
    for j in range(st // sb):
        oh = onehot[:, j * sb:(j + 1) * sb]
        ranks.append(jnp.dot(oh.astype(BF16), before, preferred_element_type=F32) + run)
        run = run + jnp.sum(oh, axis=1, keepdims=True)
    rank = jnp.concatenate(ranks, axis=1)
    e_r = lax.broadcasted_iota(jnp.int32, (R_LOG, R_LOG), 0)
    e_c = lax.broadcasted_iota(jnp.int32, (R_LOG, R_LOG), 1)
    counts = jnp.broadcast_to(run, (R_LOG, LANES))
    lstart = jnp.dot((e_c < e_r).astype(F32), counts, precision=lax.Precision.HIGHEST,
                     preferred_element_type=F32)
    slot = rank + lstart[:, 0:1]
    p1 = col_sum(jnp.where(row == i1, slot, 0.0))
    p2 = col_sum(jnp.where(row == i2, slot, 0.0))
    meta_ref[...] = jnp.concatenate([p1, p2, w1, w2, jnp.zeros((SUBLANES - 4, st), F32)], axis=0)
    lane = lax.broadcasted_iota(jnp.int32, (R_LOG, LANES), 1)
    cnt_ref[...] = jnp.where(lane == 0, counts, jnp.where(lane == 1, lstart, 0.0))


def route(logits_t):
    T = logits_t.shape[1]
    n_st = T // ST
    return pl.pallas_call(
        _route_kernel,
        grid=(n_st,),
        in_specs=[pl.BlockSpec((R_LOG, ST), lambda i: (0, i))],
        out_specs=[pl.BlockSpec((SUBLANES, ST), lambda i: (i, 0)),
                   pl.BlockSpec((R_LOG, LANES), lambda i: (i, 0))],
        out_shape=[jax.ShapeDtypeStruct((n_st * SUBLANES, ST), F32),
                   jax.ShapeDtypeStruct((n_st * R_LOG, LANES), F32)],
        compiler_params=_cp(("arbitrary",)),
        name="route",
    )(logits_t)


def _copy_by_bits(n, nbits, copy_at, start):
    off = jnp.int32(0)
    for bit in reversed(range(nbits)):
        sz = 1 << bit

        @pl.when((n & sz) != 0)
        def _(off=off, sz=sz):
            cp = copy_at(off, sz)
            if start:
                cp.start()
            else:
                cp.wait()

        off = off + (n & sz)


def _strip_copies(st, cnt_ref, lst_ref, gst_ref, make_copy, start):
    def body(e, carry):
        loc = lst_ref[st * N_EXP + e]
        glob = gst_ref[st * N_EXP + e]
        _copy_by_bits(
            cnt_ref[st * N_EXP + e], int(np.log2(ST)) + 1,
            lambda off, sz: make_copy(pl.multiple_of((loc + off) * TOK_ROWS, TOK_ROWS),
                                      pl.multiple_of((glob + off) * TOK_ROWS, TOK_ROWS), sz * TOK_ROWS),
            start)
        return carry

    lax.fori_loop(0, N_EXP, body, 0)


def _zero_padding(pad_ref, xs_ref, zero_ref, sem, start):
    tile_rows = TM_E * TOK_ROWS

    def body(e, carry):
        first = pad_ref[e]
        _copy_by_bits(
            pad_ref[N_EXP + e], int(np.log2(TM_E)),
            lambda off, sz: pltpu.make_async_copy(
                zero_ref.at[pl.ds(0, sz * TOK_ROWS)],
                xs_ref.at[pl.ds(pl.multiple_of((first + off) * TOK_ROWS, TOK_ROWS), sz * TOK_ROWS)], sem),
            start)
        return carry

    lax.fori_loop(0, N_EXP, body, 0)

    def tail(i, carry):
        cp = pltpu.make_async_copy(
            zero_ref, xs_ref.at[pl.ds(pl.multiple_of(i * tile_rows, tile_rows), tile_rows)], sem)
        if start:
            cp.start()
        else:
            cp.wait()
        return carry

    lax.fori_loop(pad_ref[2 * N_EXP], xs_ref.shape[0] // tile_rows, tail, 0)


def _dispatch_kernel(lpos_ref, cnt_ref, lst_ref, gst_ref, pad_ref, t_ref, xs_ref,
                     stage_ref, zero_ref, sem, zsem):
    st = pl.program_id(0)
    last = pl.num_programs(0) - 1
    slot = st % 2
    base = st * 2 * ST

    @pl.when(st == 0)
    def _():
        zero_ref[...] = jnp.zeros_like(zero_ref)
        _zero_padding(pad_ref, xs_ref, zero_ref, zsem, True)

    def scatter(i, carry):
        for u in range(SUBLANES):
            t = i * SUBLANES + u
            tile = t_ref[pl.ds(pl.multiple_of(t * TOK_ROWS, TOK_ROWS), TOK_ROWS), :]
            p0 = lpos_ref[base + t]
            p1 = lpos_ref[base + ST + t]
            stage_ref[slot, pl.ds(pl.multiple_of(p0 * TOK_ROWS, TOK_ROWS), TOK_ROWS), :] = tile
            stage_ref[slot, pl.ds(pl.multiple_of(p1 * TOK_ROWS, TOK_ROWS), TOK_ROWS), :] = tile
        return carry

    lax.fori_loop(0, ST // SUBLANES, scatter, 0)

    def copies(step, start):
        buf = step % 2
        _strip_copies(
            step, cnt_ref, lst_ref, gst_ref,
            lambda loc, glob, rows: pltpu.make_async_copy(
                stage_ref.at[buf, pl.ds(loc, rows)], xs_ref.at[pl.ds(glob, rows)], sem.at[buf]),
            start)

    copies(st, True)

    @pl.when(st > 0)
    def _():
        copies(st - 1, False)

    @pl.when(st == last)
    def _():
        copies(st, False)
        _zero_padding(pad_ref, xs_ref, zero_ref, zsem, False)


def dispatch(lpos, cnt, lst, gst, pad, t_tiles, n_rows):
    T = t_tiles.shape[0] // TOK_ROWS
    return pl.pallas_call(
        _dispatch_kernel,
        grid_spec=pltpu.PrefetchScalarGridSpec(
            num_scalar_prefetch=5,
            grid=(T // ST,),
            in_specs=[pl.BlockSpec((ST * TOK_ROWS, LANES), lambda i, *_: (i, 0))],
            out_specs=pl.BlockSpec(memory_space=pl.ANY),
            scratch_shapes=[pltpu.VMEM((2, 2 * ST * TOK_ROWS, LANES), F32),
                            pltpu.VMEM((TM_E * TOK_ROWS, LANES), F32),
                            pltpu.SemaphoreType.DMA((2,)),
                            pltpu.SemaphoreType.DMA(())]),
        out_shape=jax.ShapeDtypeStruct((n_rows * TOK_ROWS, LANES), F32),
        compiler_params=_cp(("arbitrary",)),
        name="dispatch",
    )(lpos, cnt, lst, gst, pad, t_tiles)


def _combine_kernel(lpos_ref, cnt_ref, lst_ref, gst_ref, w_ref, ys_ref, xm_ref, fg_ref, o_ref,
                    stage_ref, comb_ref, sem, *, final):
    st = pl.program_id(0)
    last = pl.num_programs(0) - 1
    slot = st % 2
    base = st * 2 * ST

    def copies(step, start):
        buf = step % 2
        _strip_copies(
            step, cnt_ref, lst_ref, gst_ref,
            lambda loc, glob, rows: pltpu.make_async_copy(
                ys_ref.at[pl.ds(glob, rows)], stage_ref.at[buf, pl.ds(loc, rows)], sem.at[buf]),
            start)

    @pl.when(st == 0)
    def _():
        copies(st, True)

    @pl.when(st < last)
    def _():
        copies(st + 1, True)

    copies(st, False)

    def gather(i, carry):
        for u in range(SUBLANES):
            t = i * SUBLANES + u
            p0 = lpos_ref[base + t]
            p1 = lpos_ref[base + ST + t]
            y0 = stage_ref[slot, pl.ds(pl.multiple_of(p0 * TOK_ROWS, TOK_ROWS), TOK_ROWS), :]
            y1 = stage_ref[slot, pl.ds(pl.multiple_of(p1 * TOK_ROWS, TOK_ROWS), TOK_ROWS), :]
            comb_ref[pl.ds(pl.multiple_of(t * TOK_ROWS, TOK_ROWS), TOK_ROWS), :] = (
                w_ref[0, 0, t] * y0 + w_ref[0, 1, t] * y1)
        return carry

    lax.fori_loop(0, ST // SUBLANES, gather, 0)
    moe = jnp.concatenate([comb_ref[pl.ds(s, ST, stride=TOK_ROWS), :] for s in range(TOK_ROWS)], axis=1)
    x_new = xm_ref[...] + moe
    o_ref[...] = _rms(x_new, fg_ref[...]) if final else x_new


def combine(lpos, cnt, lst, gst, w, ys, xm, fg, final):
    T = xm.shape[0]
    return pl.pallas_call(
        functools.partial(_combine_kernel, final=final),
        grid_spec=pltpu.PrefetchScalarGridSpec(
            num_scalar_prefetch=4,
            grid=(T // ST,),
            in_specs=[pl.BlockSpec((1, 2, ST), lambda i, *_: (i, 0, 0), memory_space=pltpu.SMEM),
                      pl.BlockSpec(memory_space=pl.ANY),
                      pl.BlockSpec((ST, D_MODEL), lambda i, *_: (i, 0)),
                      pl.BlockSpec((1, D_MODEL), lambda i, *_: (0, 0))],
            out_specs=pl.BlockSpec((ST, D_MODEL), lambda i, *_: (i, 0)),
            scratch_shapes=[pltpu.VMEM((2, 2 * ST * TOK_ROWS, LANES), F32),
                            pltpu.VMEM((ST * TOK_ROWS, LANES), F32),
                            pltpu.SemaphoreType.DMA((2,))]),
        out_shape=jax.ShapeDtypeStruct((T, D_MODEL), F32),
        compiler_params=_cp(("arbitrary",)),
        name="combine",
    )(lpos, cnt, lst, gst, w, ys, xm, fg)


def _expert_kernel(te_ref, nused_ref, xs_ref, wg_ref, wu_ref, wd_ref, ys_ref):
    i = pl.program_id(0)
    tm = xs_ref.shape[0] // TOK_ROWS

    @pl.when(i < nused_ref[0])
    def _():
        x = jnp.concatenate([xs_ref[pl.ds(s, tm, stride=TOK_ROWS), :] for s in range(TOK_ROWS)],
                            axis=1).astype(BF16)
        g = jnp.dot(x, wg_ref[0, 0].astype(BF16), preferred_element_type=F32)
        u = jnp.dot(x, wu_ref[0, 0].astype(BF16), preferred_element_type=F32)
        h = (g * jax.nn.sigmoid(g) * u).astype(BF16)
        y = jnp.dot(h, wd_ref[0, 0].astype(BF16), preferred_element_type=F32)
        for s in range(TOK_ROWS):
            ys_ref[pl.ds(s, tm, stride=TOK_ROWS), :] = y[:, s * LANES:(s + 1) * LANES]

    @pl.when(i >= nused_ref[0])
    def _():
        ys_ref[...] = jnp.zeros_like(ys_ref)


def experts(tile_expert, n_used, xs, wg, wu, wd, layer):
    n_tiles = xs.shape[0] // (TM_E * TOK_ROWS)
    tile = lambda i, te, nu: (jnp.minimum(i, nu[0] - 1), 0)
    return pl.pallas_call(
        _expert_kernel,
        grid_spec=pltpu.PrefetchScalarGridSpec(
            num_scalar_prefetch=2,
            grid=(n_tiles,),
            in_specs=[pl.BlockSpec((TM_E * TOK_ROWS, LANES), tile),
                      pl.BlockSpec((1, 1, D_MODEL, D_EXP), lambda i, te, nu: (layer, te[i], 0, 0)),
                      pl.BlockSpec((1, 1, D_MODEL, D_EXP), lambda i, te, nu: (layer, te[i], 0, 0)),
                      pl.BlockSpec((1, 1, D_EXP, D_MODEL), lambda i, te, nu: (layer, te[i], 0, 0))],
            out_specs=pl.BlockSpec((TM_E * TOK_ROWS, LANES), lambda i, te, nu: (i, 0))),
        out_shape=jax.ShapeDtypeStruct(xs.shape, F32),
        compiler_params=_cp(("arbitrary",)),
        name="experts",
    )(tile_expert, n_used, xs, wg, wu, wd)


def _rope_tables(S):
    inv = 1.0 / (ROPE_THETA ** (jnp.arange(0, ROPE, 2, dtype=F32) / ROPE))
    ang = jnp.arange(S, dtype=F32)[:, None] * inv[None, :]
    cos, sin = jnp.cos(ang), jnp.sin(ang)
    z = lambda n: jnp.zeros((S, n), F32)
    scale = (NOPE + ROPE) ** -0.5 * np.log2(np.e)
    pad = LANES - NOPE - ROPE
    cq = scale * jnp.concatenate([jnp.ones((S, NOPE), F32), cos, cos, z(pad)], axis=1)
    sq = scale * jnp.concatenate([z(NOPE), -sin, sin, z(pad)], axis=1)
    ck = jnp.concatenate([z(NOPE), cos, cos, z(pad)], axis=1)
    sk = jnp.concatenate([z(NOPE), -sin, sin, z(pad)], axis=1)
    return cq, sq, ck, sk


def _block_diag(w):
    n, d, _ = w.shape
    return (jnp.eye(n, dtype=w.dtype)[:, None, :, None] * w[:, :, None, :]).reshape(n * d, n * d)


def _in_weights(w_in):
    z = lambda n: jnp.zeros((D_MODEL, n), F32)
    o = 4 * ML_W
    mi, mf = w_in[:, o:o + ML_HEADS], w_in[:, o + ML_HEADS:o + 2 * ML_HEADS]
    o += 2 * ML_HEADS
    rest = w_in[:, o:o + 2 * LRU_W + Q_LORA + KV_LORA]
    kr = w_in[:, o + 2 * LRU_W + Q_LORA + KV_LORA:]
    small = jnp.concatenate([mi, mf, z(NOPE - 2 * ML_HEADS), kr, z(LANES - NOPE - ROPE)], axis=1)
    return jnp.concatenate([w_in[:, :4 * ML_W], rest, small], axis=1).astype(BF16)


def _routing_tables(cnt_rows):
    n_st = cnt_rows.shape[0] // R_LOG
    rows = cnt_rows.reshape(n_st, R_LOG, LANES)
    cnt = rows[:, N_GROUPS:N_GROUPS + N_EXP, 0].astype(jnp.int32)
    lst = rows[:, N_GROUPS:N_GROUPS + N_EXP, 1].astype(jnp.int32)
    tiles_e = (jnp.sum(cnt, axis=0) + TM_E - 1) // TM_E
    tile_cum = jnp.cumsum(tiles_e)
    base = (tile_cum - tiles_e) * TM_E
    gst = base[None, :] + jnp.cumsum(cnt, axis=0) - cnt
    n_used = tile_cum[-1:].astype(jnp.int32)
    n_e = jnp.sum(cnt, axis=0)
    pad = jnp.concatenate([base + n_e, tiles_e * TM_E - n_e, n_used]).astype(jnp.int32)
    return cnt.reshape(-1), lst.reshape(-1), gst.reshape(-1), pad, tile_cum, n_used


def kernel(x, norm1_g, w_in, ml_i_bias, ml_f_bias, ml_norm_g, conv_w, conv_b, lru_w_r, lru_b_r, lru_w_i, lru_b_i, lru_lambda, lru_norm_g, q_norm_g, w_uq, kv_norm_g, w_ukv, mla_norm_g, w_out, norm2_g, w_group, b_group, w_expert, b_expert, w_gate, w_up, w_down, final_norm_g):
    B, S, _ = x.shape
    T = B * S
    depth = w_in.shape[0]
    n_tiles = (2 * T) // TM_E + N_EXP
    tabs = _rope_tables(S)
    vones = jnp.tile(jnp.concatenate([jnp.zeros((MLA_V,), F32), jnp.ones((V_EXT - MLA_V,), F32)]),
                     MLA_H).reshape(MLA_H * V_EXT, 1)
    row = lambda a: a.reshape(1, -1)
    x2d = x.reshape(T, D_MODEL)
    for l in range(depth):
        proj = in_proj(x2d, row(norm1_g[l]), _in_weights(w_in[l]))
        gate_bias = jnp.concatenate([ml_i_bias[l], ml_f_bias[l], jnp.zeros((LANES - 2 * ML_HEADS,), F32)])
        proj3 = proj.reshape(B, S, PROJ_W)
        y_m = mlstm(proj3, row(gate_bias), ml_norm_g[l].reshape(ML_W, 1)).reshape(T, ML_W)
        w_gates = jnp.concatenate([_block_diag(lru_w_r[l]), _block_diag(lru_w_i[l])], axis=1).astype(BF16)
        b_gates = jnp.concatenate([lru_b_r[l], lru_b_i[l]])
        y_r = rglru(proj3, conv_w[l], row(conv_b[l]), w_gates, row(b_gates),
                    row(lru_lambda[l]), row(lru_norm_g[l])).reshape(T, LRU_W)
        wq = jnp.pad(w_uq[l].reshape(Q_LORA, MLA_H, NOPE + ROPE),
                     ((0, 0), (0, 0), (0, LANES - NOPE - ROPE))).reshape(Q_LORA, MLA_H * LANES).astype(BF16)
        wkv = w_ukv[l].reshape(KV_LORA, MLA_H, NOPE + MLA_V)
        wk = jnp.pad(wkv[:, :, :NOPE], ((0, 0), (0, 0), (0, LANES - NOPE))).reshape(KV_LORA, MLA_H * LANES).astype(BF16)
        wvt = jnp.pad(wkv[:, :, NOPE:], ((0, 0), (0, 0), (0, V_EXT - MLA_V))).reshape(KV_LORA, MLA_H * V_EXT).T.astype(BF16)
        q, k, vt = mla_proj(proj, row(q_norm_g[l]), row(kv_norm_g[l]), wq, wk, wvt, vones, tabs, S)
        y_a = attention(q, k, vt, mla_norm_g[l].reshape(MLA_W, 1), B, S)
        w_router = jnp.concatenate([w_group[l], w_expert[l],
                                    jnp.zeros((D_MODEL, R_LOG - N_GROUPS - N_EXP), F32)], axis=1).T
        wr_hi = w_router.astype(BF16)
        wr_lo = (w_router - wr_hi.astype(F32)).astype(BF16)
        b_router = jnp.concatenate([b_group[l], b_expert[l], jnp.zeros((R_LOG - N_GROUPS - N_EXP,), F32)])
        x_mid, t_tiles, logits = out_proj(x2d, y_m, y_r, y_a, w_out[l].astype(BF16), row(norm2_g[l]),
                                          wr_hi, wr_lo, b_router.reshape(R_LOG, 1))
        meta, cnt_rows = route(logits)
        meta = meta.reshape(T // ST, SUBLANES, ST)
        lpos = meta[:, 0:2, :].astype(jnp.int32).reshape(-1)
        w_pick = meta[:, 2:4, :]
        cnt, lst, gst, pad, tile_cum, n_used = _routing_tables(cnt_rows)
        tile_ids = jnp.minimum(jnp.arange(n_tiles, dtype=jnp.int32), n_used[0] - 1)
        tile_expert = jnp.minimum(jnp.sum(tile_ids[:, None] >= tile_cum[None, :], axis=1),
                                  N_EXP - 1).astype(jnp.int32)
        xs = dispatch(lpos, cnt, lst, gst, pad, t_tiles, n_tiles * TM_E)
        ys = experts(tile_expert, n_used, xs, w_gate, w_up, w_down, l)
        x2d = combine(lpos, cnt, lst, gst, w_pick, ys, x_mid, row(final_norm_g), l == depth - 1)
    return x2d.reshape(B, S, D_MODEL)
```

```python
import functools

import numpy as np
import jax
import jax.numpy as jnp
from jax import lax
from jax.experimental import pallas as pl
from jax.experimental.pallas import tpu as pltpu

F32 = jnp.float32
BF16 = jnp.bfloat16

D_MODEL = 1024
ML_HEADS = 4
ML_DH = 64
ML_W = ML_HEADS * ML_DH
LRU_W = 256
LRU_BLOCKS = 4
CONV_W = 4
LRU_C = 8.0
MLA_H = 8
NOPE = 64
ROPE = 32
MLA_V = 64
MLA_W = MLA_H * MLA_V
Q_LORA = 256
KV_LORA = 128
ROPE_THETA = 10000.0
N_GROUPS = 4
EPG = 8
N_EXP = N_GROUPS * EPG
D_EXP = 256
EPS = 1e-6

LANES = 128
SUBLANES = 8
V_EXT = MLA_V + SUBLANES
TOK_ROWS = D_MODEL // LANES
R_LOG = 40
PROJ_W = 2048
SMALL_COL = 1920
VMEM_LIMIT = 56 * 1024 * 1024

ML_CHUNK = 128
ML_NB = 2
LRU_TC = 256
TM_IN = 512
TQ = 256
ST = 1024
TM_E = 256


def _cp(sem):
    return pltpu.CompilerParams(dimension_semantics=sem, vmem_limit_bytes=VMEM_LIMIT)


def _rms(x, g):
    return x * lax.rsqrt(jnp.mean(x * x, axis=-1, keepdims=True) + EPS) * g


def _in_proj_kernel(x_ref, g_ref, w_ref, o_ref, small_ref):
    y = _rms(x_ref[...], g_ref[...])
    acc = jnp.dot(y.astype(BF16), w_ref[...], preferred_element_type=F32)
    o_ref[...] = acc[:, :SMALL_COL].astype(BF16)
    small_ref[...] = acc[:, SMALL_COL:]


def in_proj(x2d, g, w):
    T = x2d.shape[0]
    return pl.pallas_call(
        _in_proj_kernel,
        grid=(T // TM_IN,),
        in_specs=[pl.BlockSpec((TM_IN, D_MODEL), lambda i: (i, 0)),
                  pl.BlockSpec((1, D_MODEL), lambda i: (0, 0)),
                  pl.BlockSpec((D_MODEL, PROJ_W), lambda i: (0, 0))],
        out_specs=[pl.BlockSpec((TM_IN, SMALL_COL), lambda i: (i, 0)),
                   pl.BlockSpec((TM_IN, LANES), lambda i: (i, 0))],
        out_shape=[jax.ShapeDtypeStruct((T, SMALL_COL), BF16),
                   jax.ShapeDtypeStruct((T, LANES), F32)],
        compiler_params=_cp(("arbitrary",)),
        name="in_proj",
    )(x2d, g, w)


def _mlstm_kernel(q_ref, k_ref, v_ref, o_ref, s_ref, bias_ref, g_ref, y_ref, c_ref, m_ref):
    nb, L, _ = q_ref.shape
    c = pl.program_id(1)

    @pl.when(c == 0)
    def _():
        c_ref[...] = jnp.zeros_like(c_ref)
        m_ref[...] = jnp.zeros_like(m_ref)

    lane = lax.broadcasted_iota(jnp.int32, (L, LANES), 1)
    row_i = lax.broadcasted_iota(jnp.int32, (L, L), 0)
    col_i = lax.broadcasted_iota(jnp.int32, (L, L), 1)
    lower = (col_i <= row_i).astype(F32)
    causal_t = row_i <= col_i
    nt = (((1,), (1,)), ((), ()))
    ones = jnp.ones((SUBLANES, L), F32)
    units = [(bi, h) for bi in range(nb) for h in range(ML_HEADS)]

    gates, cums = [], []
    for bi in range(nb):
        pre = s_ref[bi] + bias_ref[...]
        logsig = jnp.minimum(pre, 0.0) - jnp.log1p(jnp.exp(-jnp.abs(pre)))
        gate = jnp.where(lane < ML_HEADS, pre, logsig)
        gates.append(gate)
        cums.append(jnp.dot(lower, gate, precision=lax.Precision.HIGHEST, preferred_element_type=F32))

    qs, ks, states, scores, qcs = {}, {}, {}, {}, {}
    for bi, h in units:
        sl = slice(h * ML_DH, (h + 1) * ML_DH)
        qs[bi, h] = q_ref[bi, :, sl].astype(BF16)
        ks[bi, h] = (k_ref[bi, :, sl].astype(F32) * (ML_DH ** -0.5)).astype(BF16)
        states[bi, h] = c_ref[bi, h]
        scores[bi, h] = lax.dot_general(ks[bi, h], qs[bi, h], nt, preferred_element_type=F32)
        qcs[bi, h] = lax.dot_general(states[bi, h].astype(BF16), qs[bi, h], nt, preferred_element_type=F32)

    gates_t = [g.T for g in gates]
    cums_t = [cu.T for cu in cums]
    v_t = [v_ref[bi].astype(F32).T for bi in range(nb)]
    o_t = [jax.nn.sigmoid(o_ref[bi].astype(F32)).T for bi in range(nb)]

    s_t, m_rows, w_inters, nums, d_states, decays = {}, {}, {}, {}, {}, {}
    for bi, h in units:
        sl = slice(h * ML_DH, (h + 1) * ML_DH)
        a_col = gates[bi][:, h:h + 1] - cums[bi][:, ML_HEADS + h:ML_HEADS + h + 1]
        b_row = cums_t[bi][ML_HEADS + h:ML_HEADS + h + 1, :]
        li_row = gates_t[bi][h:h + 1, :]
        m_prev = m_ref[bi, :, h:h + 1]
        dmat = jnp.where(causal_t, a_col + b_row, -jnp.inf)
        inter = b_row + m_prev
        m_row = jnp.maximum(jnp.max(dmat, axis=0, keepdims=True), inter)
        w_inters[bi, h] = jnp.exp(inter - m_row)
        m_rows[bi, h] = m_row
        s_t[bi, h] = scores[bi, h] * jnp.exp(dmat - m_row)
        b_last = b_row[:, L - 1:L]
        g_row = b_last - b_row + li_row
        m_new = jnp.maximum(b_last + m_prev, jnp.max(g_row, axis=1, keepdims=True))
        decays[bi, h] = jnp.exp(b_last + m_prev - m_new)
        m_ref[bi, :, h:h + 1] = m_new
        vt_ext = jnp.concatenate([v_t[bi][sl, :], ones], axis=0) * jnp.exp(g_row - m_new)
        nums[bi, h] = jnp.dot(v_t[bi][sl, :].astype(BF16), s_t[bi, h].astype(BF16), preferred_element_type=F32)
        d_states[bi, h] = jnp.dot(vt_ext.astype(BF16), ks[bi, h], preferred_element_type=F32)

    for bi in range(nb):
        ys = []
        for h in range(ML_HEADS):
            sl = slice(h * ML_DH, (h + 1) * ML_DH)
            qc, w_inter = qcs[bi, h], w_inters[bi, h]
            den = jnp.sum(s_t[bi, h], axis=0, keepdims=True) + w_inter * qc[ML_DH:ML_DH + 1, :]
            hh = (nums[bi, h] + w_inter * qc[:ML_DH, :]) / jnp.maximum(jnp.abs(den), jnp.exp(-m_rows[bi, h]))
            y = o_t[bi][sl, :] * hh
            ys.append(y * lax.rsqrt(jnp.mean(y * y, axis=0, keepdims=True) + EPS) * g_ref[sl, :])
            c_ref[bi, h] = decays[bi, h] * states[bi, h] + d_states[bi, h]
        y_ref[bi] = jnp.concatenate(ys, axis=0).T.astype(BF16)


def mlstm(proj3, small3, bias, g):
    B, S, _ = proj3.shape
    L = ML_CHUNK
    nb = ML_NB
    col = lambda j: pl.BlockSpec((nb, L, ML_W), lambda b, c, j=j: (b, c, j))
    return pl.pallas_call(
        _mlstm_kernel,
        grid=(B // nb, S // L),
        in_specs=[col(0), col(1), col(2), col(3),
                  pl.BlockSpec((nb, L, LANES), lambda b, c: (b, c, 0)),
                  pl.BlockSpec((1, LANES), lambda b, c: (0, 0)),
                  pl.BlockSpec((ML_W, 1), lambda b, c: (0, 0))],
        out_specs=pl.BlockSpec((nb, L, ML_W), lambda b, c: (b, c, 0)),
        out_shape=jax.ShapeDtypeStruct((B, S, ML_W), BF16),
        scratch_shapes=[pltpu.VMEM((nb, ML_HEADS, ML_DH + SUBLANES, ML_DH), F32),
                        pltpu.VMEM((nb, 1, LANES), F32)],
        compiler_params=_cp(("arbitrary", "arbitrary")),
        name="mlstm",
    )(proj3, proj3, proj3, proj3, small3, bias, g)


def _rglru_kernel(x_ref, gate_ref, cw_ref, cb_ref, wg_ref, bg_ref, lam_ref, g_ref, y_ref,
                  xp_ref, a_ref, u_ref, h_ref, carry_ref):
    B, tc, W = x_ref.shape
    c = pl.program_id(0)

    @pl.when(c == 0)
    def _():
        xp_ref[:, 0:SUBLANES, :] = jnp.zeros((B, SUBLANES, W), F32)
        carry_ref[...] = jnp.zeros_like(carry_ref)

    xp_ref[:, SUBLANES:, :] = x_ref[...].astype(F32)
    xc = jnp.zeros((B, tc, W), F32) + cb_ref[...]
    for j in range(CONV_W):
        off = SUBLANES - (CONV_W - 1) + j
        xc = xc + cw_ref[j:j + 1, :] * xp_ref[:, off:off + tc, :]
    xp_ref[:, 0:SUBLANES, :] = x_ref[:, tc - SUBLANES:, :].astype(F32)

    xc2 = xc.reshape(B * tc, W)
    ri = jax.nn.sigmoid(jnp.dot(xc2.astype(BF16), wg_ref[...], preferred_element_type=F32) + bg_ref[...])
    r = ri[:, :W]
    ig = ri[:, W:]
    lam = lam_ref[...]
    softplus_neg = jnp.maximum(-lam, 0.0) + jnp.log1p(jnp.exp(-jnp.abs(lam)))
    log_a = -LRU_C * r * softplus_neg
    a = jnp.exp(log_a)
    u = jnp.sqrt(1.0 - jnp.exp(2.0 * log_a)) * (ig * xc2)
    halves = W // LANES
    for k in range(halves):
        a_ref[k] = a[:, k * LANES:(k + 1) * LANES]
        u_ref[k] = u[:, k * LANES:(k + 1) * LANES]

    def step(i, hs):
        base = pl.multiple_of(i * SUBLANES, SUBLANES)
        for j in range(SUBLANES):
            rows = pl.ds(base + j, B, stride=tc)
            hs = tuple(a_ref.at[k][rows, :] * hs[k] + u_ref.at[k][rows, :] for k in range(halves))
            for k in range(halves):
                h_ref.at[k][rows, :] = hs[k]
        return hs

    init = tuple(carry_ref[k] for k in range(halves))
    last = lax.fori_loop(0, tc // SUBLANES, step, init)
    for k in range(halves):
        carry_ref[k] = last[k]
    h_all = jnp.concatenate([h_ref[k] for k in range(halves)], axis=1)
    y = h_all.reshape(B, tc, W) * jax.nn.gelu(gate_ref[...].astype(F32))
    y_ref[...] = _rms(y, g_ref[...]).astype(BF16)


def rglru(proj3, cw, cb, wg, bg, lam, g):
    B, S, _ = proj3.shape
    tc = LRU_TC
    vec = lambda n: pl.BlockSpec((1, n), lambda c: (0, 0))
    return pl.pallas_call(
        _rglru_kernel,
        grid=(S // tc,),
        in_specs=[pl.BlockSpec((B, tc, LRU_W), lambda c: (0, c, 4)),
                  pl.BlockSpec((B, tc, LRU_W), lambda c: (0, c, 5)),
                  pl.BlockSpec((CONV_W, LRU_W), lambda c: (0, 0)),
                  vec(LRU_W),
                  pl.BlockSpec((LRU_W, 2 * LRU_W), lambda c: (0, 0)),
                  vec(2 * LRU_W), vec(LRU_W), vec(LRU_W)],
        out_specs=pl.BlockSpec((B, tc, LRU_W), lambda c: (0, c, 0)),
        out_shape=jax.ShapeDtypeStruct((B, S, LRU_W), BF16),
        scratch_shapes=[pltpu.VMEM((B, tc + SUBLANES, LRU_W), F32),
                        pltpu.VMEM((LRU_W // LANES, B * tc, LANES), F32),
                        pltpu.VMEM((LRU_W // LANES, B * tc, LANES), F32),
                        pltpu.VMEM((LRU_W // LANES, B * tc, LANES), F32),
                        pltpu.VMEM((LRU_W // LANES, B, LANES), F32)],
        compiler_params=_cp(("arbitrary",)),
        name="rglru",
    )(proj3, proj3, cw, cb, wg, bg, lam, g)


def _rot_half(blk):
    lane = lax.broadcasted_iota(jnp.int32, blk.shape, 1)
    return jnp.where(lane < NOPE + ROPE // 2,
                     pltpu.roll(blk, LANES - ROPE // 2, 1), pltpu.roll(blk, ROPE // 2, 1))


def _mla_proj_kernel(cq_ref, ckv_ref, small_ref, qg_ref, kvg_ref, wq_ref, wk_ref, wvt_ref, vones_ref,
                     cq_t, sq_t, ck_t, sk_t, q_ref, k_ref, vt_ref):
    qn = _rms(cq_ref[...].astype(F32), qg_ref[...]).astype(BF16)
    qf = jnp.dot(qn, wq_ref[...], preferred_element_type=F32)
    kvn = _rms(ckv_ref[...].astype(F32), kvg_ref[...]).astype(BF16)
    kf = jnp.dot(kvn, wk_ref[...], preferred_element_type=F32)
    vt_ref[...] = (lax.dot_general(wvt_ref[...], kvn, (((1,), (1,)), ((), ())), preferred_element_type=F32)
                   + vones_ref[...]).astype(BF16)
    small = small_ref[...]
    kr = small * ck_t[...] + _rot_half(small) * sk_t[...]
    cq = cq_t[...]
    sq = sq_t[...]
    for h in range(MLA_H):
        sl = slice(h * LANES, (h + 1) * LANES)
        blk = qf[:, sl]
        q_ref[:, sl] = (blk * cq + _rot_half(blk) * sq).astype(BF16)
        k_ref[:, sl] = (kf[:, sl] + kr).astype(BF16)


def mla_proj(proj, small, qg, kvg, wq, wk, wvt, vones, tabs, S):
    T = proj.shape[0]
    tm = TM_IN
    npos = S // tm
    tab = pl.BlockSpec((tm, LANES), lambda i: (i % npos, 0))
    full = lambda a: pl.BlockSpec(a.shape, lambda i: (0, 0))
    return pl.pallas_call(
        _mla_proj_kernel,
        grid=(T // tm,),
        in_specs=[pl.BlockSpec((tm, Q_LORA), lambda i: (i, 6)),
                  pl.BlockSpec((tm, KV_LORA), lambda i: (i, (SMALL_COL - KV_LORA) // KV_LORA)),
                  pl.BlockSpec((tm, LANES), lambda i: (i, 0)),
                  full(qg), full(kvg), full(wq), full(wk), full(wvt), full(vones), tab, tab, tab, tab],
        out_specs=[pl.BlockSpec((tm, MLA_H * LANES), lambda i: (i, 0)),
                   pl.BlockSpec((tm, MLA_H * LANES), lambda i: (i, 0)),
                   pl.BlockSpec((MLA_H * V_EXT, tm), lambda i: (0, i))],
        out_shape=[jax.ShapeDtypeStruct((T, MLA_H * LANES), BF16),
                   jax.ShapeDtypeStruct((T, MLA_H * LANES), BF16),
                   jax.ShapeDtypeStruct((MLA_H * V_EXT, T), BF16)],
        compiler_params=_cp(("arbitrary",)),
        name="mla_proj",
    )(proj, proj, small, qg, kvg, wq, wk, wvt, vones, *tabs)


def _attn_kernel(q_ref, k_ref, vt_ref, g_ref, o_ref, acc_ref, m_ref, yt_ref):
    tq = q_ref.shape[0]
    i = pl.program_id(1)
    key_i = lax.broadcasted_iota(jnp.int32, (tq, tq), 0)
    qry_i = lax.broadcasted_iota(jnp.int32, (tq, tq), 1)
    causal = key_i <= qry_i
    nt = (((1,), (1,)), ((), ()))
    m_ref[...] = jnp.full_like(m_ref, -jnp.inf)
    acc_ref[...] = jnp.zeros_like(acc_ref)

    ahead = 4

    def scores(off, h):
        hsl = slice(h * LANES, (h + 1) * LANES)
        return lax.dot_general(k_ref[pl.ds(off, tq), hsl], q_ref[:, hsl], nt, preferred_element_type=F32)

    def update(h, alpha, pv):
        rows = slice(h * V_EXT, (h + 1) * V_EXT)
        acc_ref[rows, :] = alpha * acc_ref[rows, :] + pv

    def block(j, masked):
        off = pl.multiple_of(j * tq, tq)
        pending = [scores(off, h) for h in range(ahead)]
        late = None
        for h in range(MLA_H):
            s = pending.pop(0)
            if h + ahead < MLA_H:
                pending.append(scores(off, h + ahead))
            if masked:
                s = jnp.where(causal, s, -jnp.inf)
            m = m_ref[h:h + 1, :]
            m_new = jnp.maximum(m, jnp.max(s, axis=0, keepdims=True))
            m_ref[h:h + 1, :] = m_new
            alpha = jnp.exp2(m - m_new)
            p = jnp.exp2(s - m_new).astype(BF16)
            pv = jnp.dot(vt_ref[h * V_EXT:(h + 1) * V_EXT, pl.ds(off, tq)], p, preferred_element_type=F32)
            if late is not None:
                update(*late)
            late = (h, alpha, pv)
        update(*late)

    def body(j, carry):
        block(j, False)
        return carry

    lax.fori_loop(0, i, body, 0)
    block(i, True)

    ssq = jnp.zeros((1, tq), F32)
    for h in range(MLA_H):
        vrows = slice(h * MLA_V, (h + 1) * MLA_V)
        yh = acc_ref[h * V_EXT:h * V_EXT + MLA_V, :] * (1.0 / acc_ref[h * V_EXT + MLA_V:h * V_EXT + MLA_V + 1, :])
        yt_ref[vrows, :] = yh
        ssq = ssq + jnp.sum(yh * yh, axis=0, keepdims=True)
    yt = yt_ref[...] * lax.rsqrt(ssq * (1.0 / MLA_W) + EPS) * g_ref[...]
    o_ref[...] = yt.T.astype(BF16)


def attention(q, k, vt, g, B, S):
    T = B * S
    nq = S // TQ
    return pl.pallas_call(
        _attn_kernel,
        grid=(B, nq),
        in_specs=[pl.BlockSpec((TQ, MLA_H * LANES), lambda b, i: (b * nq + i, 0)),
                  pl.BlockSpec((S, MLA_H * LANES), lambda b, i: (b, 0)),
                  pl.BlockSpec((MLA_H * V_EXT, S), lambda b, i: (0, b)),
                  pl.BlockSpec((MLA_W, 1), lambda b, i: (0, 0))],
        out_specs=pl.BlockSpec((TQ, MLA_W), lambda b, i: (b * nq + i, 0)),
        out_shape=jax.ShapeDtypeStruct((T, MLA_W), BF16),
        scratch_shapes=[pltpu.VMEM((MLA_H * V_EXT, TQ), F32),
                        pltpu.VMEM((MLA_H, TQ), F32),
                        pltpu.VMEM((MLA_W, TQ), F32)],
        compiler_params=_cp(("arbitrary", "arbitrary")),
        name="attention",
    )(q, k, vt, g)


def _out_proj_kernel(x_ref, ym_ref, yr_ref, ya_ref, w_ref, g_ref, wrh_ref, wrl_ref, br_ref,
                     xm_ref, t_ref, lg_ref):
    acc = jnp.dot(ym_ref[...].astype(BF16), w_ref[0:ML_W, :], preferred_element_type=F32)
    acc += jnp.dot(yr_ref[...].astype(BF16), w_ref[ML_W:ML_W + LRU_W, :], preferred_element_type=F32)
    acc += jnp.dot(ya_ref[...].astype(BF16), w_ref[ML_W + LRU_W:, :], preferred_element_type=F32)
    xm = x_ref[...] + acc
    xm_ref[...] = xm
    t = _rms(xm, g_ref[...])
    t_hi = t.astype(BF16)
    t_lo = (t - t_hi.astype(F32)).astype(BF16)
    nt = (((1,), (1,)), ((), ()))
    lg = lax.dot_general(wrh_ref[...], t_hi, nt, preferred_element_type=F32)
    lg += lax.dot_general(wrl_ref[...], t_hi, nt, preferred_element_type=F32)
    lg += lax.dot_general(wrh_ref[...], t_lo, nt, preferred_element_type=F32)
    lg_ref[...] = lg + br_ref[...]
    for s in range(TOK_ROWS):
        t_ref[pl.ds(s, t.shape[0], stride=TOK_ROWS), :] = t[:, s * LANES:(s + 1) * LANES]


def out_proj(x2d, ym, yr, ya, w, g, wr_hi, wr_lo, br):
    T = x2d.shape[0]
    tm = TM_IN
    row = lambda n: pl.BlockSpec((tm, n), lambda i: (i, 0))
    full = lambda a: pl.BlockSpec(a.shape, lambda i: (0, 0))
    return pl.pallas_call(
        _out_proj_kernel,
        grid=(T // tm,),
        in_specs=[row(D_MODEL), row(ML_W), row(LRU_W), row(MLA_W), full(w), full(g),
                  full(wr_hi), full(wr_lo), full(br)],
        out_specs=[row(D_MODEL),
                   pl.BlockSpec((tm * TOK_ROWS, LANES), lambda i: (i, 0)),
                   pl.BlockSpec((R_LOG, tm), lambda i: (0, i))],
        out_shape=[jax.ShapeDtypeStruct((T, D_MODEL), F32),
                   jax.ShapeDtypeStruct((T * TOK_ROWS, LANES), F32),
                   jax.ShapeDtypeStruct((R_LOG, T), F32)],
        compiler_params=_cp(("arbitrary",)),
        name="out_proj",
    )(x2d, ym, yr, ya, w, g, wr_hi, wr_lo, br)


def _route_kernel(lg_ref, meta_ref, cnt_ref):
    st = lg_ref.shape[1]
    lg = lg_ref[...]
    row = lax.broadcasted_iota(jnp.int32, (R_LOG, st), 0)
    col_max = lambda a: jnp.max(a, axis=0, keepdims=True)
    col_sum = lambda a: jnp.sum(a, axis=0, keepdims=True)
    first = lambda cond: jnp.min(jnp.where(cond, row, R_LOG), axis=0, keepdims=True)

    gmask = row < N_GROUPS
    gl = jnp.where(gmask, lg, -jnp.inf)
    ge = jnp.exp(gl - col_max(gl))
    gp = ge / col_sum(ge)
    g_val = col_max(gp)
    g_idx = first(gmask & (gp == g_val))

    smask = (row >= N_GROUPS) & (row < N_GROUPS + N_EXP) & (((row - N_GROUPS) >> 3) == g_idx)
    el = jnp.where(smask, lg, -jnp.inf)
    ee = jnp.exp(el - col_max(el))
    ep = ee / col_sum(ee)
    v1 = col_max(ep)
    i1 = first(smask & (ep == v1))
    rest = smask & (row != i1)
    v2 = col_max(jnp.where(rest, ep, -1.0))
    i2 = first(rest & (ep == v2))
    den = v1 + v2
    w1 = v1 / den * g_val
    w2 = v2 / den * g_val

    onehot = ((row == i1) | (row == i2)).astype(F32)
    sb = min(st, 256)
    r_i = lax.broadcasted_iota(jnp.int32, (sb, sb), 0)
    c_i = lax.broadcasted_iota(jnp.int32, (sb, sb), 1)
    before = (r_i < c_i).astype(BF16)
    run = jnp.zeros((R_LOG, 1), F32)
    ranks = []
    for j in range(st // sb):
        oh = onehot[:, j * sb:(j + 1) * sb]
        ranks.append(jnp.dot(oh.astype(BF16), before, preferred_element_type=F32) + run)
        run = run + jnp.sum(oh, axis=1, keepdims=True)
    rank = jnp.concatenate(ranks, axis=1)
    e_r = lax.broadcasted_iota(jnp.int32, (R_LOG, R_LOG), 0)
    e_c = lax.broadcasted_iota(jnp.int32, (R_LOG, R_LOG), 1)
    counts = jnp.broadcast_to(run, (R_LOG, LANES))
    lstart = jnp.dot((e_c < e_r).astype(F32), counts, precision=lax.Precision.HIGHEST,
                     preferred_element_type=F32)
    slot = rank + lstart[:, 0:1]
    p1 = col_sum(jnp.where(row == i1, slot, 0.0))
    p2 = col_sum(jnp.where(row == i2, slot, 0.0))
    meta_ref[...] = jnp.concatenate([p1, p2, w1, w2, jnp.zeros((SUBLANES - 4, st), F32)], axis=0)
    lane = lax.broadcasted_iota(jnp.int32, (R_LOG, LANES), 1)
    cnt_ref[...] = jnp.where(lane == 0, counts, jnp.where(lane == 1, lstart, 0.0))


def route(logits_t):
    T = logits_t.shape[1]
    n_st = T // ST
    return pl.pallas_call(
        _route_kernel,
        grid=(n_st,),
        in_specs=[pl.BlockSpec((R_LOG, ST), lambda i: (0, i))],
        out_specs=[pl.BlockSpec((SUBLANES, ST), lambda i: (i, 0)),
                   pl.BlockSpec((R_LOG, LANES), lambda i: (i, 0))],
        out_shape=[jax.ShapeDtypeStruct((n_st * SUBLANES, ST), F32),
                   jax.ShapeDtypeStruct((n_st * R_LOG, LANES), F32)],
        compiler_params=_cp(("arbitrary",)),
        name="route",
    )(logits_t)


def _copy_by_bits(n, nbits, copy_at, start):
    off = jnp.int32(0)
    for bit in reversed(range(nbits)):
        sz = 1 << bit

        @pl.when((n & sz) != 0)
        def _(off=off, sz=sz):
            cp = copy_at(off, sz)
            if start:
                cp.start()
            else:
                cp.wait()

        off = off + (n & sz)


def _strip_copies(st, cnt_ref, lst_ref, gst_ref, make_copy, start):
    def body(e, carry):
        loc = lst_ref[st * N_EXP + e]
        glob = gst_ref[st * N_EXP + e]
        _copy_by_bits(
            cnt_ref[st * N_EXP + e], int(np.log2(ST)) + 1,
            lambda off, sz: make_copy(pl.multiple_of((loc + off) * TOK_ROWS, TOK_ROWS),
                                      pl.multiple_of((glob + off) * TOK_ROWS, TOK_ROWS), sz * TOK_ROWS),
            start)
        return carry

    lax.fori_loop(0, N_EXP, body, 0)


def _zero_padding(pad_ref, xs_ref, zero_ref, sem, start):
    tile_rows = TM_E * TOK_ROWS

    def body(e, carry):
        first = pad_ref[e]
        _copy_by_bits(
            pad_ref[N_EXP + e], int(np.log2(TM_E)),
            lambda off, sz: pltpu.make_async_copy(
                zero_ref.at[pl.ds(0, sz * TOK_ROWS)],
                xs_ref.at[pl.ds(pl.multiple_of((first + off) * TOK_ROWS, TOK_ROWS), sz * TOK_ROWS)], sem),
            start)
        return carry

    lax.fori_loop(0, N_EXP, body, 0)

    def tail(i, carry):
        cp = pltpu.make_async_copy(
            zero_ref, xs_ref.at[pl.ds(pl.multiple_of(i * tile_rows, tile_rows), tile_rows)], sem)
        if start:
            cp.start()
        else:
            cp.wait()
        return carry

    lax.fori_loop(pad_ref[2 * N_EXP], xs_ref.shape[0] // tile_rows, tail, 0)


def _dispatch_kernel(lpos_ref, cnt_ref, lst_ref, gst_ref, pad_ref, t_ref, xs_ref,
                     stage_ref, zero_ref, sem, zsem):
    st = pl.program_id(0)
    last = pl.num_programs(0) - 1
    slot = st % 2
    base = st * 2 * ST

    @pl.when(st == 0)
    def _():
        zero_ref[...] = jnp.zeros_like(zero_ref)
        _zero_padding(pad_ref, xs_ref, zero_ref, zsem, True)

    def scatter(i, carry):
        for u in range(SUBLANES):
            t = i * SUBLANES + u
            tile = t_ref[pl.ds(pl.multiple_of(t * TOK_ROWS, TOK_ROWS), TOK_ROWS), :]
            p0 = lpos_ref[base + t]
            p1 = lpos_ref[base + ST + t]
            stage_ref[slot, pl.ds(pl.multiple_of(p0 * TOK_ROWS, TOK_ROWS), TOK_ROWS), :] = tile
            stage_ref[slot, pl.ds(pl.multiple_of(p1 * TOK_ROWS, TOK_ROWS), TOK_ROWS), :] = tile
        return carry

    lax.fori_loop(0, ST // SUBLANES, scatter, 0)

    def copies(step, start):
        buf = step % 2
        _strip_copies(
            step, cnt_ref, lst_ref, gst_ref,
            lambda loc, glob, rows: pltpu.make_async_copy(
                stage_ref.at[buf, pl.ds(loc, rows)], xs_ref.at[pl.ds(glob, rows)], sem.at[buf]),
            start)

    def wait_all(buf):
        pltpu.make_async_copy(stage_ref.at[buf], xs_ref.at[pl.ds(0, 2 * ST * TOK_ROWS)], sem.at[buf]).wait()

    copies(st, True)

    @pl.when(st > 0)
    def _():
        wait_all(1 - slot)

    @pl.when(st == last)
    def _():
        wait_all(slot)
        _zero_padding(pad_ref, xs_ref, zero_ref, zsem, False)


def dispatch(lpos, cnt, lst, gst, pad, t_tiles, n_rows):
    T = t_tiles.shape[0] // TOK_ROWS
    return pl.pallas_call(
        _dispatch_kernel,
        grid_spec=pltpu.PrefetchScalarGridSpec(
            num_scalar_prefetch=5,
            grid=(T // ST,),
            in_specs=[pl.BlockSpec((ST * TOK_ROWS, LANES), lambda i, *_: (i, 0))],
            out_specs=pl.BlockSpec(memory_space=pl.ANY),
            scratch_shapes=[pltpu.VMEM((2, 2 * ST * TOK_ROWS, LANES), F32),
                            pltpu.VMEM((TM_E * TOK_ROWS, LANES), F32),
                            pltpu.SemaphoreType.DMA((2,)),
                            pltpu.SemaphoreType.DMA(())]),
        out_shape=jax.ShapeDtypeStruct((n_rows * TOK_ROWS, LANES), F32),
        compiler_params=_cp(("arbitrary",)),
        name="dispatch",
    )(lpos, cnt, lst, gst, pad, t_tiles)


def _combine_kernel(lpos_ref, cnt_ref, lst_ref, gst_ref, w_ref, ys_ref, xm_ref, fg_ref, o_ref,
                    stage_ref, comb_ref, sem, *, final):
    st = pl.program_id(0)
    last = pl.num_programs(0) - 1
    slot = st % 2
    base = st * 2 * ST

    def copies(step, start):
        buf = step % 2
        _strip_copies(
            step, cnt_ref, lst_ref, gst_ref,
            lambda loc, glob, rows: pltpu.make_async_copy(
                ys_ref.at[pl.ds(glob, rows)], stage_ref.at[buf, pl.ds(loc, rows)], sem.at[buf]),
            start)

    @pl.when(st == 0)
    def _():
        copies(st, True)

    @pl.when(st < last)
    def _():
        copies(st + 1, True)

    pltpu.make_async_copy(ys_ref.at[pl.ds(0, 2 * ST * TOK_ROWS)], stage_ref.at[slot], sem.at[slot]).wait()

    def gather(i, carry):
        for u in range(SUBLANES):
            t = i * SUBLANES + u
            p0 = lpos_ref[base + t]
            p1 = lpos_ref[base + ST + t]
            y0 = stage_ref[slot, pl.ds(pl.multiple_of(p0 * TOK_ROWS, TOK_ROWS), TOK_ROWS), :]
            y1 = stage_ref[slot, pl.ds(pl.multiple_of(p1 * TOK_ROWS, TOK_ROWS), TOK_ROWS), :]
            comb_ref[pl.ds(pl.multiple_of(t * TOK_ROWS, TOK_ROWS), TOK_ROWS), :] = (
                w_ref[0, 0, t] * y0 + w_ref[0, 1, t] * y1)
        return carry

    lax.fori_loop(0, ST // SUBLANES, gather, 0)
    moe = jnp.concatenate([comb_ref[pl.ds(s, ST, stride=TOK_ROWS), :] for s in range(TOK_ROWS)], axis=1)
    x_new = xm_ref[...] + moe
    o_ref[...] = _rms(x_new, fg_ref[...]) if final else x_new


def combine(lpos, cnt, lst, gst, w, ys, xm, fg, final):
    T = xm.shape[0]
    return pl.pallas_call(
        functools.partial(_combine_kernel, final=final),
        grid_spec=pltpu.PrefetchScalarGridSpec(
            num_scalar_prefetch=4,
            grid=(T // ST,),
            in_specs=[pl.BlockSpec((1, 2, ST), lambda i, *_: (i, 0, 0), memory_space=pltpu.SMEM),
                      pl.BlockSpec(memory_space=pl.ANY),
                      pl.BlockSpec((ST, D_MODEL), lambda i, *_: (i, 0)),
                      pl.BlockSpec((1, D_MODEL), lambda i, *_: (0, 0))],
            out_specs=pl.BlockSpec((ST, D_MODEL), lambda i, *_: (i, 0)),
            scratch_shapes=[pltpu.VMEM((2, 2 * ST * TOK_ROWS, LANES), F32),
                            pltpu.VMEM((ST * TOK_ROWS, LANES), F32),
                            pltpu.SemaphoreType.DMA((2,))]),
        out_shape=jax.ShapeDtypeStruct((T, D_MODEL), F32),
        compiler_params=_cp(("arbitrary",)),
        name="combine",
    )(lpos, cnt, lst, gst, w, ys, xm, fg)


def _expert_kernel(te_ref, nused_ref, xs_ref, wg_ref, wu_ref, wd_ref, ys_ref):
    i = pl.program_id(0)
    tm = xs_ref.shape[0] // TOK_ROWS

    @pl.when(i < nused_ref[0])
    def _():
        x = jnp.concatenate([xs_ref[pl.ds(s, tm, stride=TOK_ROWS), :] for s in range(TOK_ROWS)],
                            axis=1).astype(BF16)
        g = jnp.dot(x, wg_ref[0, 0].astype(BF16), preferred_element_type=F32)
        u = jnp.dot(x, wu_ref[0, 0].astype(BF16), preferred_element_type=F32)
        h = (g * jax.nn.sigmoid(g) * u).astype(BF16)
        y = jnp.dot(h, wd_ref[0, 0].astype(BF16), preferred_element_type=F32)
        for s in range(TOK_ROWS):
            ys_ref[pl.ds(s, tm, stride=TOK_ROWS), :] = y[:, s * LANES:(s + 1) * LANES]

    @pl.when(i >= nused_ref[0])
    def _():
        ys_ref[...] = jnp.zeros_like(ys_ref)


def experts(tile_expert, n_used, xs, wg, wu, wd, layer):
    n_tiles = xs.shape[0] // (TM_E * TOK_ROWS)
    tile = lambda i, te, nu: (jnp.maximum(jnp.minimum(i, nu[0] - 1), 0), 0)
    return pl.pallas_call(
        _expert_kernel,
        grid_spec=pltpu.PrefetchScalarGridSpec(
            num_scalar_prefetch=2,
            grid=(n_tiles,),
            in_specs=[pl.BlockSpec((TM_E * TOK_ROWS, LANES), tile),
                      pl.BlockSpec((1, 1, D_MODEL, D_EXP), lambda i, te, nu: (layer, te[i], 0, 0)),
                      pl.BlockSpec((1, 1, D_MODEL, D_EXP), lambda i, te, nu: (layer, te[i], 0, 0)),
                      pl.BlockSpec((1, 1, D_EXP, D_MODEL), lambda i, te, nu: (layer, te[i], 0, 0))],
            out_specs=pl.BlockSpec((TM_E * TOK_ROWS, LANES), lambda i, te, nu: (i, 0))),
        out_shape=jax.ShapeDtypeStruct(xs.shape, F32),
        compiler_params=_cp(("arbitrary",)),
        name="experts",
    )(tile_expert, n_used, xs, wg, wu, wd)


def _rope_tables(S):
    inv = 1.0 / (ROPE_THETA ** (jnp.arange(0, ROPE, 2, dtype=F32) / ROPE))
    ang = jnp.arange(S, dtype=F32)[:, None] * inv[None, :]
    cos, sin = jnp.cos(ang), jnp.sin(ang)
    z = lambda n: jnp.zeros((S, n), F32)
    scale = (NOPE + ROPE) ** -0.5 * np.log2(np.e)
    pad = LANES - NOPE - ROPE
    cq = scale * jnp.concatenate([jnp.ones((S, NOPE), F32), cos, cos, z(pad)], axis=1)
    sq = scale * jnp.concatenate([z(NOPE), -sin, sin, z(pad)], axis=1)
    ck = jnp.concatenate([z(NOPE), cos, cos, z(pad)], axis=1)
    sk = jnp.concatenate([z(NOPE), -sin, sin, z(pad)], axis=1)
    return cq, sq, ck, sk


def _block_diag(w):
    n, d, _ = w.shape
    return (jnp.eye(n, dtype=w.dtype)[:, None, :, None] * w[:, :, None, :]).reshape(n * d, n * d)


def _in_weights(w_in):
    z = lambda n: jnp.zeros((D_MODEL, n), F32)
    o = 4 * ML_W
    mi, mf = w_in[:, o:o + ML_HEADS], w_in[:, o + ML_HEADS:o + 2 * ML_HEADS]
    o += 2 * ML_HEADS
    rest = w_in[:, o:o + 2 * LRU_W + Q_LORA + KV_LORA]
    kr = w_in[:, o + 2 * LRU_W + Q_LORA + KV_LORA:]
    small = jnp.concatenate([mi, mf, z(NOPE - 2 * ML_HEADS), kr, z(LANES - NOPE - ROPE)], axis=1)
    return jnp.concatenate([w_in[:, :4 * ML_W], rest, small], axis=1).astype(BF16)


def _routing_tables(cnt_rows):
    n_st = cnt_rows.shape[0] // R_LOG
    rows = cnt_rows.reshape(n_st, R_LOG, LANES)
    cnt = rows[:, N_GROUPS:N_GROUPS + N_EXP, 0].astype(jnp.int32)
    lst = rows[:, N_GROUPS:N_GROUPS + N_EXP, 1].astype(jnp.int32)
    tiles_e = (jnp.sum(cnt, axis=0) + TM_E - 1) // TM_E
    tile_cum = jnp.cumsum(tiles_e)
    base = (tile_cum - tiles_e) * TM_E
    gst = base[None, :] + jnp.cumsum(cnt, axis=0) - cnt
    n_used = tile_cum[-1:].astype(jnp.int32)
    n_e = jnp.sum(cnt, axis=0)
    pad = jnp.concatenate([base + n_e, tiles_e * TM_E - n_e, n_used]).astype(jnp.int32)
    return cnt.reshape(-1), lst.reshape(-1), gst.reshape(-1), pad, tile_cum, n_used


def kernel(x, norm1_g, w_in, ml_i_bias, ml_f_bias, ml_norm_g, conv_w, conv_b, lru_w_r, lru_b_r, lru_w_i, lru_b_i, lru_lambda, lru_norm_g, q_norm_g, w_uq, kv_norm_g, w_ukv, mla_norm_g, w_out, norm2_g, w_group, b_group, w_expert, b_expert, w_gate, w_up, w_down, final_norm_g):
    B, S, _ = x.shape
    T = B * S
    depth = w_in.shape[0]
    n_tiles = (2 * T) // TM_E + N_EXP
    tabs = _rope_tables(S)
    vones = jnp.tile(jnp.concatenate([jnp.zeros((MLA_V,), F32), jnp.ones((V_EXT - MLA_V,), F32)]),
                     MLA_H).reshape(MLA_H * V_EXT, 1)
    row = lambda a: a.reshape(1, -1)
    x2d = x.reshape(T, D_MODEL)
    for l in range(depth):
        proj, small = in_proj(x2d, row(norm1_g[l]), _in_weights(w_in[l]))
        gate_bias = jnp.concatenate([ml_i_bias[l], ml_f_bias[l], jnp.zeros((LANES - 2 * ML_HEADS,), F32)])
        proj3 = proj.reshape(B, S, SMALL_COL)
        y_m = mlstm(proj3, small.reshape(B, S, LANES), row(gate_bias),
                    ml_norm_g[l].reshape(ML_W, 1)).reshape(T, ML_W)
        w_gates = jnp.concatenate([_block_diag(lru_w_r[l]), _block_diag(lru_w_i[l])], axis=1).astype(BF16)
        b_gates = jnp.concatenate([lru_b_r[l], lru_b_i[l]])
        y_r = rglru(proj3, conv_w[l], row(conv_b[l]), w_gates, row(b_gates),
                    row(lru_lambda[l]), row(lru_norm_g[l])).reshape(T, LRU_W)
        wq = jnp.pad(w_uq[l].reshape(Q_LORA, MLA_H, NOPE + ROPE),
                     ((0, 0), (0, 0), (0, LANES - NOPE - ROPE))).reshape(Q_LORA, MLA_H * LANES).astype(BF16)
        wkv = w_ukv[l].reshape(KV_LORA, MLA_H, NOPE + MLA_V)
        wk = jnp.pad(wkv[:, :, :NOPE], ((0, 0), (0, 0), (0, LANES - NOPE))).reshape(KV_LORA, MLA_H * LANES).astype(BF16)
        wvt = jnp.pad(wkv[:, :, NOPE:], ((0, 0), (0, 0), (0, V_EXT - MLA_V))).reshape(KV_LORA, MLA_H * V_EXT).T.astype(BF16)
        q, k, vt = mla_proj(proj, small, row(q_norm_g[l]), row(kv_norm_g[l]), wq, wk, wvt, vones, tabs, S)
        y_a = attention(q, k, vt, mla_norm_g[l].reshape(MLA_W, 1), B, S)
        w_router = jnp.concatenate([w_group[l], w_expert[l],
                                    jnp.zeros((D_MODEL, R_LOG - N_GROUPS - N_EXP), F32)], axis=1).T
        wr_hi = w_router.astype(BF16)
        wr_lo = (w_router - wr_hi.astype(F32)).astype(BF16)
        b_router = jnp.concatenate([b_group[l], b_expert[l], jnp.zeros((R_LOG - N_GROUPS - N_EXP,), F32)])
        x_mid, t_tiles, logits = out_proj(x2d, y_m, y_r, y_a, w_out[l].astype(BF16), row(norm2_g[l]),
                                          wr_hi, wr_lo, b_router.reshape(R_LOG, 1))
        meta, cnt_rows = route(logits)
        meta = meta.reshape(T // ST, SUBLANES, ST)
        lpos = meta[:, 0:2, :].astype(jnp.int32).reshape(-1)
        w_pick = meta[:, 2:4, :]
        cnt, lst, gst, pad, tile_cum, n_used = _routing_tables(cnt_rows)
        tile_ids = jnp.minimum(jnp.arange(n_tiles, dtype=jnp.int32), n_used[0] - 1)
        tile_expert = jnp.minimum(jnp.sum(tile_ids[:, None] >= tile_cum[None, :], axis=1),
                                  N_EXP - 1).astype(jnp.int32)
        xs = dispatch(lpos, cnt, lst, gst, pad, t_tiles, n_tiles * TM_E)
        ys = experts(tile_expert, n_used, xs, w_gate, w_up, w_down, l)
        x2d = combine(lpos, cnt, lst, gst, w_pick, ys, x_mid, row(final_norm_g), l == depth - 1)
    return x2d.reshape(B, S, D_MODEL)
```

```python
import functools

import numpy as np
import jax
import jax.numpy as jnp
from jax import lax
from jax.experimental import pallas as pl
from jax.experimental.pallas import tpu as pltpu

F32 = jnp.float32
BF16 = jnp.bfloat16

D_MODEL = 1024
ML_HEADS = 4
ML_DH = 64
ML_W = ML_HEADS * ML_DH
LRU_W = 256
LRU_BLOCKS = 4
CONV_W = 4
LRU_C = 8.0
MLA_H = 8
NOPE = 64
ROPE = 32
MLA_V = 64
MLA_W = MLA_H * MLA_V
Q_LORA = 256
KV_LORA = 128
ROPE_THETA = 10000.0
N_GROUPS = 4
EPG = 8
N_EXP = N_GROUPS * EPG
D_EXP = 256
EPS = 1e-6

LANES = 128
SUBLANES = 8
V_EXT = MLA_V + SUBLANES
TOK_ROWS = D_MODEL // LANES
R_LOG = 40
PROJ_W = 2048
SMALL_COL = 1920
VMEM_LIMIT = 56 * 1024 * 1024

ML_CHUNK = 128
ML_NB = 2
LRU_TC = 256
TM_IN = 512
TQ = 256
ST = 1024
TM_E = 256


def _cp(sem):
    return pltpu.CompilerParams(dimension_semantics=sem, vmem_limit_bytes=VMEM_LIMIT)


def _rms(x, g):
    return x * lax.rsqrt(jnp.mean(x * x, axis=-1, keepdims=True) + EPS) * g


def _in_proj_kernel(x_ref, g_ref, w_ref, o_ref, small_ref):
    y = _rms(x_ref[...], g_ref[...])
    acc = jnp.dot(y.astype(BF16), w_ref[...], preferred_element_type=F32)
    o_ref[...] = acc[:, :SMALL_COL].astype(BF16)
    small_ref[...] = acc[:, SMALL_COL:]


def in_proj(x2d, g, w):
    T = x2d.shape[0]
    return pl.pallas_call(
        _in_proj_kernel,
        grid=(T // TM_IN,),
        in_specs=[pl.BlockSpec((TM_IN, D_MODEL), lambda i: (i, 0)),
                  pl.BlockSpec((1, D_MODEL), lambda i: (0, 0)),
                  pl.BlockSpec((D_MODEL, PROJ_W), lambda i: (0, 0))],
        out_specs=[pl.BlockSpec((TM_IN, SMALL_COL), lambda i: (i, 0)),
                   pl.BlockSpec((TM_IN, LANES), lambda i: (i, 0))],
        out_shape=[jax.ShapeDtypeStruct((T, SMALL_COL), BF16),
                   jax.ShapeDtypeStruct((T, LANES), F32)],
        compiler_params=_cp(("arbitrary",)),
        name="in_proj",
    )(x2d, g, w)


def _mlstm_kernel(q_ref, k_ref, v_ref, o_ref, s_ref, bias_ref, g_ref, y_ref, c_ref, m_ref):
    nb, L, _ = q_ref.shape
    c = pl.program_id(1)

    @pl.when(c == 0)
    def _():
        c_ref[...] = jnp.zeros_like(c_ref)
        m_ref[...] = jnp.zeros_like(m_ref)

    lane = lax.broadcasted_iota(jnp.int32, (L, LANES), 1)
    row_i = lax.broadcasted_iota(jnp.int32, (L, L), 0)
    col_i = lax.broadcasted_iota(jnp.int32, (L, L), 1)
    lower = (col_i <= row_i).astype(F32)
    causal_t = row_i <= col_i
    nt = (((1,), (1,)), ((), ()))
    ones = jnp.ones((SUBLANES, L), F32)
    units = [(bi, h) for bi in range(nb) for h in range(ML_HEADS)]

    gates, cums = [], []
    for bi in range(nb):
        pre = s_ref[bi] + bias_ref[...]
        logsig = jnp.minimum(pre, 0.0) - jnp.log1p(jnp.exp(-jnp.abs(pre)))
        gate = jnp.where(lane < ML_HEADS, pre, logsig)
        gates.append(gate)
        cums.append(jnp.dot(lower, gate, precision=lax.Precision.HIGHEST, preferred_element_type=F32))

    qs, ks, states, scores, qcs = {}, {}, {}, {}, {}
    for bi, h in units:
        sl = slice(h * ML_DH, (h + 1) * ML_DH)
        qs[bi, h] = q_ref[bi, :, sl].astype(BF16)
        ks[bi, h] = (k_ref[bi, :, sl].astype(F32) * (ML_DH ** -0.5)).astype(BF16)
        states[bi, h] = c_ref[bi, h]
        scores[bi, h] = lax.dot_general(ks[bi, h], qs[bi, h], nt, preferred_element_type=F32)
        qcs[bi, h] = lax.dot_general(states[bi, h].astype(BF16), qs[bi, h], nt, preferred_element_type=F32)

    gates_t = [g.T for g in gates]
    cums_t = [cu.T for cu in cums]
    v_t = [v_ref[bi].astype(F32).T for bi in range(nb)]
    o_t = [jax.nn.sigmoid(o_ref[bi].astype(F32)).T for bi in range(nb)]

    s_t, m_rows, w_inters, nums, d_states, decays = {}, {}, {}, {}, {}, {}
    for bi, h in units:
        sl = slice(h * ML_DH, (h + 1) * ML_DH)
        a_col = gates[bi][:, h:h + 1] - cums[bi][:, ML_HEADS + h:ML_HEADS + h + 1]
        b_row = cums_t[bi][ML_HEADS + h:ML_HEADS + h + 1, :]
        li_row = gates_t[bi][h:h + 1, :]
        m_prev = m_ref[bi, :, h:h + 1]
        dmat = jnp.where(causal_t, a_col + b_row, -jnp.inf)
        inter = b_row + m_prev
        m_row = jnp.maximum(jnp.max(dmat, axis=0, keepdims=True), inter)
        w_inters[bi, h] = jnp.exp(inter - m_row)
        m_rows[bi, h] = m_row
        s_t[bi, h] = scores[bi, h] * jnp.exp(dmat - m_row)
        b_last = b_row[:, L - 1:L]
        g_row = b_last - b_row + li_row
        m_new = jnp.maximum(b_last + m_prev, jnp.max(g_row, axis=1, keepdims=True))
        decays[bi, h] = jnp.exp(b_last + m_prev - m_new)
        m_ref[bi, :, h:h + 1] = m_new
        vt_ext = jnp.concatenate([v_t[bi][sl, :], ones], axis=0) * jnp.exp(g_row - m_new)
        nums[bi, h] = jnp.dot(v_t[bi][sl, :].astype(BF16), s_t[bi, h].astype(BF16), preferred_element_type=F32)
        d_states[bi, h] = jnp.dot(vt_ext.astype(BF16), ks[bi, h], preferred_element_type=F32)

    for bi in range(nb):
        ys = []
        for h in range(ML_HEADS):
            sl = slice(h * ML_DH, (h + 1) * ML_DH)
            qc, w_inter = qcs[bi, h], w_inters[bi, h]
            den = jnp.sum(s_t[bi, h], axis=0, keepdims=True) + w_inter * qc[ML_DH:ML_DH + 1, :]
            hh = (nums[bi, h] + w_inter * qc[:ML_DH, :]) / jnp.maximum(jnp.abs(den), jnp.exp(-m_rows[bi, h]))
            y = o_t[bi][sl, :] * hh
            ys.append(y * lax.rsqrt(jnp.mean(y * y, axis=0, keepdims=True) + EPS) * g_ref[sl, :])
            c_ref[bi, h] = decays[bi, h] * states[bi, h] + d_states[bi, h]
        y_ref[bi] = jnp.concatenate(ys, axis=0).T.astype(BF16)


def mlstm(proj3, small3, bias, g):
    B, S, _ = proj3.shape
    L = ML_CHUNK
    nb = ML_NB
    col = lambda j: pl.BlockSpec((nb, L, ML_W), lambda b, c, j=j: (b, c, j))
    return pl.pallas_call(
        _mlstm_kernel,
        grid=(B // nb, S // L),
        in_specs=[col(0), col(1), col(2), col(3),
                  pl.BlockSpec((nb, L, LANES), lambda b, c: (b, c, 0)),
                  pl.BlockSpec((1, LANES), lambda b, c: (0, 0)),
                  pl.BlockSpec((ML_W, 1), lambda b, c: (0, 0))],
        out_specs=pl.BlockSpec((nb, L, ML_W), lambda b, c: (b, c, 0)),
        out_shape=jax.ShapeDtypeStruct((B, S, ML_W), BF16),
        scratch_shapes=[pltpu.VMEM((nb, ML_HEADS, ML_DH + SUBLANES, ML_DH), F32),
                        pltpu.VMEM((nb, 1, LANES), F32)],
        compiler_params=_cp(("arbitrary", "arbitrary")),
        name="mlstm",
    )(proj3, proj3, proj3, proj3, small3, bias, g)


def _rglru_kernel(x_ref, gate_ref, cw_ref, cb_ref, wg_ref, bg_ref, lam_ref, g_ref, y_ref,
                  xp_ref, a_ref, u_ref, h_ref, carry_ref):
    B, tc, W = x_ref.shape
    c = pl.program_id(0)

    @pl.when(c == 0)
    def _():
        xp_ref[:, 0:SUBLANES, :] = jnp.zeros((B, SUBLANES, W), F32)
        carry_ref[...] = jnp.zeros_like(carry_ref)

    xp_ref[:, SUBLANES:, :] = x_ref[...].astype(F32)
    xc = jnp.zeros((B, tc, W), F32) + cb_ref[...]
    for j in range(CONV_W):
        off = SUBLANES - (CONV_W - 1) + j
        xc = xc + cw_ref[j:j + 1, :] * xp_ref[:, off:off + tc, :]
    xp_ref[:, 0:SUBLANES, :] = x_ref[:, tc - SUBLANES:, :].astype(F32)

    xc2 = xc.reshape(B * tc, W)
    ri = jax.nn.sigmoid(jnp.dot(xc2.astype(BF16), wg_ref[...], preferred_element_type=F32) + bg_ref[...])
    r = ri[:, :W]
    ig = ri[:, W:]
    lam = lam_ref[...]
    softplus_neg = jnp.maximum(-lam, 0.0) + jnp.log1p(jnp.exp(-jnp.abs(lam)))
    log_a = -LRU_C * r * softplus_neg
    a = jnp.exp(log_a)
    u = jnp.sqrt(1.0 - jnp.exp(2.0 * log_a)) * (ig * xc2)
    halves = W // LANES
    for k in range(halves):
        a_ref[k] = a[:, k * LANES:(k + 1) * LANES]
        u_ref[k] = u[:, k * LANES:(k + 1) * LANES]

    def step(i, hs):
        base = pl.multiple_of(i * SUBLANES, SUBLANES)
        for j in range(SUBLANES):
            rows = pl.ds(base + j, B, stride=tc)
            hs = tuple(a_ref.at[k][rows, :] * hs[k] + u_ref.at[k][rows, :] for k in range(halves))
            for k in range(halves):
                h_ref.at[k][rows, :] = hs[k]
        return hs

    init = tuple(carry_ref[k] for k in range(halves))
    last = lax.fori_loop(0, tc // SUBLANES, step, init)
    for k in range(halves):
        carry_ref[k] = last[k]
    h_all = jnp.concatenate([h_ref[k] for k in range(halves)], axis=1)
    y = h_all.reshape(B, tc, W) * jax.nn.gelu(gate_ref[...].astype(F32))
    y_ref[...] = _rms(y, g_ref[...]).astype(BF16)


def rglru(proj3, cw, cb, wg, bg, lam, g):
    B, S, _ = proj3.shape
    tc = LRU_TC
    vec = lambda n: pl.BlockSpec((1, n), lambda c: (0, 0))
    return pl.pallas_call(
        _rglru_kernel,
        grid=(S // tc,),
        in_specs=[pl.BlockSpec((B, tc, LRU_W), lambda c: (0, c, 4)),
                  pl.BlockSpec((B, tc, LRU_W), lambda c: (0, c, 5)),
                  pl.BlockSpec((CONV_W, LRU_W), lambda c: (0, 0)),
                  vec(LRU_W),
                  pl.BlockSpec((LRU_W, 2 * LRU_W), lambda c: (0, 0)),
                  vec(2 * LRU_W), vec(LRU_W), vec(LRU_W)],
        out_specs=pl.BlockSpec((B, tc, LRU_W), lambda c: (0, c, 0)),
        out_shape=jax.ShapeDtypeStruct((B, S, LRU_W), BF16),
        scratch_shapes=[pltpu.VMEM((B, tc + SUBLANES, LRU_W), F32),
                        pltpu.VMEM((LRU_W // LANES, B * tc, LANES), F32),
                        pltpu.VMEM((LRU_W // LANES, B * tc, LANES), F32),
                        pltpu.VMEM((LRU_W // LANES, B * tc, LANES), F32),
                        pltpu.VMEM((LRU_W // LANES, B, LANES), F32)],
        compiler_params=_cp(("arbitrary",)),
        name="rglru",
    )(proj3, proj3, cw, cb, wg, bg, lam, g)


def _rot_half(blk):
    lane = lax.broadcasted_iota(jnp.int32, blk.shape, 1)
    return jnp.where(lane < NOPE + ROPE // 2,
                     pltpu.roll(blk, LANES - ROPE // 2, 1), pltpu.roll(blk, ROPE // 2, 1))


def _mla_proj_kernel(cq_ref, ckv_ref, small_ref, qg_ref, kvg_ref, wq_ref, wqs_ref, wk_ref, wvt_ref, vones_ref,
                     cq_t, sq_t, ck_t, sk_t, q_ref, k_ref, vt_ref):
    qn = _rms(cq_ref[...].astype(F32), qg_ref[...]).astype(BF16)
    qf = jnp.dot(qn, wq_ref[...], preferred_element_type=F32)
    qs = jnp.dot(qn, wqs_ref[...], preferred_element_type=F32)
    kvn = _rms(ckv_ref[...].astype(F32), kvg_ref[...]).astype(BF16)
    kf = jnp.dot(kvn, wk_ref[...], preferred_element_type=F32)
    vt_ref[...] = (lax.dot_general(wvt_ref[...], kvn, (((1,), (1,)), ((), ())), preferred_element_type=F32)
                   + vones_ref[...]).astype(BF16)
    small = small_ref[...]
    kr = small * ck_t[...] + _rot_half(small) * sk_t[...]
    cq = cq_t[...]
    sq = sq_t[...]
    for h in range(MLA_H):
        sl = slice(h * LANES, (h + 1) * LANES)
        q_ref[:, sl] = (qf[:, sl] * cq + qs[:, sl] * sq).astype(BF16)
        k_ref[:, sl] = (kf[:, sl] + kr).astype(BF16)


def mla_proj(proj, small, qg, kvg, wq, wqs, wk, wvt, vones, tabs, S):
    T = proj.shape[0]
    tm = TM_IN
    npos = S // tm
    tab = pl.BlockSpec((tm, LANES), lambda i: (i % npos, 0))
    full = lambda a: pl.BlockSpec(a.shape, lambda i: (0, 0))
    return pl.pallas_call(
        _mla_proj_kernel,
        grid=(T // tm,),
        in_specs=[pl.BlockSpec((tm, Q_LORA), lambda i: (i, 6)),
                  pl.BlockSpec((tm, KV_LORA), lambda i: (i, (SMALL_COL - KV_LORA) // KV_LORA)),
                  pl.BlockSpec((tm, LANES), lambda i: (i, 0)),
                  full(qg), full(kvg), full(wq), full(wqs), full(wk), full(wvt), full(vones),
                  tab, tab, tab, tab],
        out_specs=[pl.BlockSpec((tm, MLA_H * LANES), lambda i: (i, 0)),
                   pl.BlockSpec((tm, MLA_H * LANES), lambda i: (i, 0)),
                   pl.BlockSpec((MLA_H * V_EXT, tm), lambda i: (0, i))],
        out_shape=[jax.ShapeDtypeStruct((T, MLA_H * LANES), BF16),
                   jax.ShapeDtypeStruct((T, MLA_H * LANES), BF16),
                   jax.ShapeDtypeStruct((MLA_H * V_EXT, T), BF16)],
        compiler_params=_cp(("arbitrary",)),
        name="mla_proj",
    )(proj, proj, small, qg, kvg, wq, wqs, wk, wvt, vones, *tabs)


def _attn_kernel(q_ref, k_ref, vt_ref, g_ref, o_ref, acc_ref, m_ref, yt_ref):
    tq = q_ref.shape[0]
    i = pl.program_id(1)
    key_i = lax.broadcasted_iota(jnp.int32, (tq, tq), 0)
    qry_i = lax.broadcasted_iota(jnp.int32, (tq, tq), 1)
    causal = key_i <= qry_i
    nt = (((1,), (1,)), ((), ()))
    m_ref[...] = jnp.full_like(m_ref, -jnp.inf)
    acc_ref[...] = jnp.zeros_like(acc_ref)

    ahead = 4

    def scores(off, h):
        hsl = slice(h * LANES, (h + 1) * LANES)
        return lax.dot_general(k_ref[pl.ds(off, tq), hsl], q_ref[:, hsl], nt, preferred_element_type=F32)

    def update(h, alpha, pv):
        rows = slice(h * V_EXT, (h + 1) * V_EXT)
        acc_ref[rows, :] = alpha * acc_ref[rows, :] + pv

    def blocks(items):
        work = [(pl.multiple_of(j * tq, tq), h, masked) for j, masked in items for h in range(MLA_H)]
        pending = [scores(off, h) for off, h, _ in work[:ahead]]
        late = None
        for n, (off, h, masked) in enumerate(work):
            s = pending.pop(0)
            if n + ahead < len(work):
                pending.append(scores(*work[n + ahead][:2]))
            if masked:
                s = jnp.where(causal, s, -jnp.inf)
            m = m_ref[h:h + 1, :]
            m_new = jnp.maximum(m, jnp.max(s, axis=0, keepdims=True))
            m_ref[h:h + 1, :] = m_new
            alpha = jnp.exp2(m - m_new)
            p = jnp.exp2(s - m_new).astype(BF16)
            pv = jnp.dot(vt_ref[h * V_EXT:(h + 1) * V_EXT, pl.ds(off, tq)], p, preferred_element_type=F32)
            if late is not None:
                update(*late)
            late = (h, alpha, pv)
        update(*late)

    def body(pair, carry):
        blocks([(2 * pair, False), (2 * pair + 1, False)])
        return carry

    lax.fori_loop(0, i // 2, body, 0)

    @pl.when(i % 2 == 1)
    def _():
        blocks([(i - 1, False), (i, True)])

    @pl.when(i % 2 == 0)
    def _():
        blocks([(i, True)])

    ssq = jnp.zeros((1, tq), F32)
    for h in range(MLA_H):
        vrows = slice(h * MLA_V, (h + 1) * MLA_V)
        yh = acc_ref[h * V_EXT:h * V_EXT + MLA_V, :] * (1.0 / acc_ref[h * V_EXT + MLA_V:h * V_EXT + MLA_V + 1, :])
        yt_ref[vrows, :] = yh
        ssq = ssq + jnp.sum(yh * yh, axis=0, keepdims=True)
    yt = yt_ref[...] * lax.rsqrt(ssq * (1.0 / MLA_W) + EPS) * g_ref[...]
    o_ref[...] = yt.T.astype(BF16)


def attention(q, k, vt, g, B, S):
    T = B * S
    nq = S // TQ
    return pl.pallas_call(
        _attn_kernel,
        grid=(B, nq),
        in_specs=[pl.BlockSpec((TQ, MLA_H * LANES), lambda b, i: (b * nq + i, 0)),
                  pl.BlockSpec((S, MLA_H * LANES), lambda b, i: (b, 0)),
                  pl.BlockSpec((MLA_H * V_EXT, S), lambda b, i: (0, b)),
                  pl.BlockSpec((MLA_W, 1), lambda b, i: (0, 0))],
        out_specs=pl.BlockSpec((TQ, MLA_W), lambda b, i: (b * nq + i, 0)),
        out_shape=jax.ShapeDtypeStruct((T, MLA_W), BF16),
        scratch_shapes=[pltpu.VMEM((MLA_H * V_EXT, TQ), F32),
                        pltpu.VMEM((MLA_H, TQ), F32),
                        pltpu.VMEM((MLA_W, TQ), F32)],
        compiler_params=_cp(("arbitrary", "arbitrary")),
        name="attention",
    )(q, k, vt, g)


def _out_proj_kernel(x_ref, ym_ref, yr_ref, ya_ref, w_ref, g_ref, wrh_ref, wrl_ref, br_ref,
                     xm_ref, t_ref, lg_ref):
    acc = jnp.dot(ym_ref[...].astype(BF16), w_ref[0:ML_W, :], preferred_element_type=F32)
    acc += jnp.dot(yr_ref[...].astype(BF16), w_ref[ML_W:ML_W + LRU_W, :], preferred_element_type=F32)
    acc += jnp.dot(ya_ref[...].astype(BF16), w_ref[ML_W + LRU_W:, :], preferred_element_type=F32)
    xm = x_ref[...] + acc
    xm_ref[...] = xm
    t = _rms(xm, g_ref[...])
    t_hi = t.astype(BF16)
    t_lo = (t - t_hi.astype(F32)).astype(BF16)
    nt = (((1,), (1,)), ((), ()))
    lg = lax.dot_general(wrh_ref[...], t_hi, nt, preferred_element_type=F32)
    lg += lax.dot_general(wrl_ref[...], t_hi, nt, preferred_element_type=F32)
    lg += lax.dot_general(wrh_ref[...], t_lo, nt, preferred_element_type=F32)
    lg_ref[...] = lg + br_ref[...]
    for s in range(TOK_ROWS):
        t_ref[pl.ds(s, t.shape[0], stride=TOK_ROWS), :] = t[:, s * LANES:(s + 1) * LANES]


def out_proj(x2d, ym, yr, ya, w, g, wr_hi, wr_lo, br):
    T = x2d.shape[0]
    tm = TM_IN
    row = lambda n: pl.BlockSpec((tm, n), lambda i: (i, 0))
    full = lambda a: pl.BlockSpec(a.shape, lambda i: (0, 0))
    return pl.pallas_call(
        _out_proj_kernel,
        grid=(T // tm,),
        in_specs=[row(D_MODEL), row(ML_W), row(LRU_W), row(MLA_W), full(w), full(g),
                  full(wr_hi), full(wr_lo), full(br)],
        out_specs=[row(D_MODEL),
                   pl.BlockSpec((tm * TOK_ROWS, LANES), lambda i: (i, 0)),
                   pl.BlockSpec((R_LOG, tm), lambda i: (0, i))],
        out_shape=[jax.ShapeDtypeStruct((T, D_MODEL), F32),
                   jax.ShapeDtypeStruct((T * TOK_ROWS, LANES), F32),
                   jax.ShapeDtypeStruct((R_LOG, T), F32)],
        compiler_params=_cp(("arbitrary",)),
        name="out_proj",
    )(x2d, ym, yr, ya, w, g, wr_hi, wr_lo, br)


def _route_kernel(lg_ref, meta_ref, cnt_ref):
    st = lg_ref.shape[1]
    lg = lg_ref[...]
    row = lax.broadcasted_iota(jnp.int32, (R_LOG, st), 0)
    col_max = lambda a: jnp.max(a, axis=0, keepdims=True)
    col_sum = lambda a: jnp.sum(a, axis=0, keepdims=True)
    first = lambda cond: jnp.min(jnp.where(cond, row, R_LOG), axis=0, keepdims=True)

    gmask = row < N_GROUPS
    gl = jnp.where(gmask, lg, -jnp.inf)
    ge = jnp.exp(gl - col_max(gl))
    gp = ge / col_sum(ge)
    g_val = col_max(gp)
    g_idx = first(gmask & (gp == g_val))

    smask = (row >= N_GROUPS) & (row < N_GROUPS + N_EXP) & (((row - N_GROUPS) >> 3) == g_idx)
    el = jnp.where(smask, lg, -jnp.inf)
    ee = jnp.exp(el - col_max(el))
    ep = ee / col_sum(ee)
    v1 = col_max(ep)
    i1 = first(smask & (ep == v1))
    rest = smask & (row != i1)
    v2 = col_max(jnp.where(rest, ep, -1.0))
    i2 = first(rest & (ep == v2))
    den = v1 + v2
    w1 = v1 / den * g_val
    w2 = v2 / den * g_val

    onehot = ((row == i1) | (row == i2)).astype(F32)
    sb = min(st, 256)
    r_i = lax.broadcasted_iota(jnp.int32, (sb, sb), 0)
    c_i = lax.broadcasted_iota(jnp.int32, (sb, sb), 1)
    before = (r_i < c_i).astype(BF16)
    run = jnp.zeros((R_LOG, 1), F32)
    ranks = []
    for j in range(st // sb):
        oh = onehot[:, j * sb:(j + 1) * sb]
        ranks.append(jnp.dot(oh.astype(BF16), before, preferred_element_type=F32) + run)
        run = run + jnp.sum(oh, axis=1, keepdims=True)
    rank = jnp.concatenate(ranks, axis=1)
    e_r = lax.broadcasted_iota(jnp.int32, (R_LOG, R_LOG), 0)
    e_c = lax.broadcasted_iota(jnp.int32, (R_LOG, R_LOG), 1)
    counts = jnp.broadcast_to(run, (R_LOG, LANES))
    lstart = jnp.dot((e_c < e_r).astype(F32), counts, precision=lax.Precision.HIGHEST,
                     preferred_element_type=F32)
    slot = rank + lstart[:, 0:1]
    p1 = col_sum(jnp.where(row == i1, slot, 0.0))
    p2 = col_sum(jnp.where(row == i2, slot, 0.0))
    meta_ref[...] = jnp.concatenate([p1, p2, w1, w2, jnp.zeros((SUBLANES - 4, st), F32)], axis=0)
    lane = lax.broadcasted_iota(jnp.int32, (R_LOG, LANES), 1)
    cnt_ref[...] = jnp.where(lane == 0, counts, jnp.where(lane == 1, lstart, 0.0))


def route(logits_t):
    T = logits_t.shape[1]
    n_st = T // ST
    return pl.pallas_call(
        _route_kernel,
        grid=(n_st,),
        in_specs=[pl.BlockSpec((R_LOG, ST), lambda i: (0, i))],
        out_specs=[pl.BlockSpec((SUBLANES, ST), lambda i: (i, 0)),
                   pl.BlockSpec((R_LOG, LANES), lambda i: (i, 0))],
        out_shape=[jax.ShapeDtypeStruct((n_st * SUBLANES, ST), F32),
                   jax.ShapeDtypeStruct((n_st * R_LOG, LANES), F32)],
        compiler_params=_cp(("arbitrary",)),
        name="route",
    )(logits_t)


def _copy_by_bits(n, nbits, copy_at, start):
    off = jnp.int32(0)
    for bit in reversed(range(nbits)):
        sz = 1 << bit

        @pl.when((n & sz) != 0)
        def _(off=off, sz=sz):
            cp = copy_at(off, sz)
            if start:
                cp.start()
            else:
                cp.wait()

        off = off + (n & sz)


def _strip_copies(st, cnt_ref, lst_ref, gst_ref, make_copy, start):
    def body(e, carry):
        loc = lst_ref[st * N_EXP + e]
        glob = gst_ref[st * N_EXP + e]
        _copy_by_bits(
            cnt_ref[st * N_EXP + e], int(np.log2(ST)) + 1,
            lambda off, sz: make_copy(pl.multiple_of((loc + off) * TOK_ROWS, TOK_ROWS),
                                      pl.multiple_of((glob + off) * TOK_ROWS, TOK_ROWS), sz * TOK_ROWS),
            start)
        return carry

    lax.fori_loop(0, N_EXP, body, 0)


def _zero_padding(pad_ref, xs_ref, zero_ref, sem, start):
    tile_rows = TM_E * TOK_ROWS

    def body(e, carry):
        first = pad_ref[e]
        _copy_by_bits(
            pad_ref[N_EXP + e], int(np.log2(TM_E)),
            lambda off, sz: pltpu.make_async_copy(
                zero_ref.at[pl.ds(0, sz * TOK_ROWS)],
                xs_ref.at[pl.ds(pl.multiple_of((first + off) * TOK_ROWS, TOK_ROWS), sz * TOK_ROWS)], sem),
            start)
        return carry

    lax.fori_loop(0, N_EXP, body, 0)

    def tail(i, carry):
        cp = pltpu.make_async_copy(
            zero_ref, xs_ref.at[pl.ds(pl.multiple_of(i * tile_rows, tile_rows), tile_rows)], sem)
        if start:
            cp.start()
        else:
            cp.wait()
        return carry

    lax.fori_loop(pad_ref[2 * N_EXP], xs_ref.shape[0] // tile_rows, tail, 0)


def _dispatch_kernel(lpos_ref, cnt_ref, lst_ref, gst_ref, pad_ref, t_ref, xs_ref,
                     stage_ref, zero_ref, sem, zsem):
    st = pl.program_id(0)
    last = pl.num_programs(0) - 1
    slot = st % 2
    base = st * 2 * ST

    @pl.when(st == 0)
    def _():
        zero_ref[...] = jnp.zeros_like(zero_ref)
        _zero_padding(pad_ref, xs_ref, zero_ref, zsem, True)

    def scatter(i, carry):
        for u in range(SUBLANES):
            t = i * SUBLANES + u
            tile = t_ref[pl.ds(pl.multiple_of(t * TOK_ROWS, TOK_ROWS), TOK_ROWS), :]
            p0 = lpos_ref[base + t]
            p1 = lpos_ref[base + ST + t]
            stage_ref[slot, pl.ds(pl.multiple_of(p0 * TOK_ROWS, TOK_ROWS), TOK_ROWS), :] = tile
            stage_ref[slot, pl.ds(pl.multiple_of(p1 * TOK_ROWS, TOK_ROWS), TOK_ROWS), :] = tile
        return carry

    lax.fori_loop(0, ST // SUBLANES, scatter, 0)

    def copies(step, start):
        buf = step % 2
        _strip_copies(
            step, cnt_ref, lst_ref, gst_ref,
            lambda loc, glob, rows: pltpu.make_async_copy(
                stage_ref.at[buf, pl.ds(loc, rows)], xs_ref.at[pl.ds(glob, rows)], sem.at[buf]),
            start)

    def wait_all(buf):
        pltpu.make_async_copy(stage_ref.at[buf], xs_ref.at[pl.ds(0, 2 * ST * TOK_ROWS)], sem.at[buf]).wait()

    copies(st, True)

    @pl.when(st > 0)
    def _():
        wait_all(1 - slot)

    @pl.when(st == last)
    def _():
        wait_all(slot)
        _zero_padding(pad_ref, xs_ref, zero_ref, zsem, False)


def dispatch(lpos, cnt, lst, gst, pad, t_tiles, n_rows):
    T = t_tiles.shape[0] // TOK_ROWS
    return pl.pallas_call(
        _dispatch_kernel,
        grid_spec=pltpu.PrefetchScalarGridSpec(
            num_scalar_prefetch=5,
            grid=(T // ST,),
            in_specs=[pl.BlockSpec((ST * TOK_ROWS, LANES), lambda i, *_: (i, 0))],
            out_specs=pl.BlockSpec(memory_space=pl.ANY),
            scratch_shapes=[pltpu.VMEM((2, 2 * ST * TOK_ROWS, LANES), F32),
                            pltpu.VMEM((TM_E * TOK_ROWS, LANES), F32),
                            pltpu.SemaphoreType.DMA((2,)),
                            pltpu.SemaphoreType.DMA(())]),
        out_shape=jax.ShapeDtypeStruct((n_rows * TOK_ROWS, LANES), F32),
        compiler_params=_cp(("arbitrary",)),
        name="dispatch",
    )(lpos, cnt, lst, gst, pad, t_tiles)


def _combine_kernel(lpos_ref, cnt_ref, lst_ref, gst_ref, w_ref, ys_ref, xm_ref, fg_ref, o_ref,
                    stage_ref, comb_ref, sem, *, final):
    st = pl.program_id(0)
    last = pl.num_programs(0) - 1
    slot = st % 2
    base = st * 2 * ST

    def copies(step, start):
        buf = step % 2
        _strip_copies(
            step, cnt_ref, lst_ref, gst_ref,
            lambda loc, glob, rows: pltpu.make_async_copy(
                ys_ref.at[pl.ds(glob, rows)], stage_ref.at[buf, pl.ds(loc, rows)], sem.at[buf]),
            start)

    @pl.when(st == 0)
    def _():
        copies(st, True)

    @pl.when(st < last)
    def _():
        copies(st + 1, True)

    pltpu.make_async_copy(ys_ref.at[pl.ds(0, 2 * ST * TOK_ROWS)], stage_ref.at[slot], sem.at[slot]).wait()

    def gather(i, carry):
        for u in range(SUBLANES):
            t = i * SUBLANES + u
            p0 = lpos_ref[base + t]
            p1 = lpos_ref[base + ST + t]
            y0 = stage_ref[slot, pl.ds(pl.multiple_of(p0 * TOK_ROWS, TOK_ROWS), TOK_ROWS), :]
            y1 = stage_ref[slot, pl.ds(pl.multiple_of(p1 * TOK_ROWS, TOK_ROWS), TOK_ROWS), :]
            comb_ref[pl.ds(pl.multiple_of(t * TOK_ROWS, TOK_ROWS), TOK_ROWS), :] = (
                w_ref[0, 0, t] * y0 + w_ref[0, 1, t] * y1)
        return carry

    lax.fori_loop(0, ST // SUBLANES, gather, 0)
    moe = jnp.concatenate([comb_ref[pl.ds(s, ST, stride=TOK_ROWS), :] for s in range(TOK_ROWS)], axis=1)
    x_new = xm_ref[...] + moe
    o_ref[...] = _rms(x_new, fg_ref[...]) if final else x_new


def combine(lpos, cnt, lst, gst, w, ys, xm, fg, final):
    T = xm.shape[0]
    return pl.pallas_call(
        functools.partial(_combine_kernel, final=final),
        grid_spec=pltpu.PrefetchScalarGridSpec(
            num_scalar_prefetch=4,
            grid=(T // ST,),
            in_specs=[pl.BlockSpec((1, 2, ST), lambda i, *_: (i, 0, 0), memory_space=pltpu.SMEM),
                      pl.BlockSpec(memory_space=pl.ANY),
                      pl.BlockSpec((ST, D_MODEL), lambda i, *_: (i, 0)),
                      pl.BlockSpec((1, D_MODEL), lambda i, *_: (0, 0))],
            out_specs=pl.BlockSpec((ST, D_MODEL), lambda i, *_: (i, 0)),
            scratch_shapes=[pltpu.VMEM((2, 2 * ST * TOK_ROWS, LANES), F32),
                            pltpu.VMEM((ST * TOK_ROWS, LANES), F32),
                            pltpu.SemaphoreType.DMA((2,))]),
        out_shape=jax.ShapeDtypeStruct((T, D_MODEL), F32),
        compiler_params=_cp(("arbitrary",)),
        name="combine",
    )(lpos, cnt, lst, gst, w, ys, xm, fg)


def _expert_kernel(te_ref, nused_ref, xs_ref, wg_ref, wu_ref, wd_ref, ys_ref):
    i = pl.program_id(0)
    tm = xs_ref.shape[0] // TOK_ROWS

    @pl.when(i < nused_ref[0])
    def _():
        x = jnp.concatenate([xs_ref[pl.ds(s, tm, stride=TOK_ROWS), :] for s in range(TOK_ROWS)],
                            axis=1).astype(BF16)
        g = jnp.dot(x, wg_ref[0, 0].astype(BF16), preferred_element_type=F32)
        u = jnp.dot(x, wu_ref[0, 0].astype(BF16), preferred_element_type=F32)
        h = (g * jax.nn.sigmoid(g) * u).astype(BF16)
        y = jnp.dot(h, wd_ref[0, 0].astype(BF16), preferred_element_type=F32)
        for s in range(TOK_ROWS):
            ys_ref[pl.ds(s, tm, stride=TOK_ROWS), :] = y[:, s * LANES:(s + 1) * LANES]

    @pl.when(i >= nused_ref[0])
    def _():
        ys_ref[...] = jnp.zeros_like(ys_ref)


def experts(tile_expert, n_used, xs, wg, wu, wd, layer):
    n_tiles = xs.shape[0] // (TM_E * TOK_ROWS)
    tile = lambda i, te, nu: (jnp.maximum(jnp.minimum(i, nu[0] - 1), 0), 0)
    return pl.pallas_call(
        _expert_kernel,
        grid_spec=pltpu.PrefetchScalarGridSpec(
            num_scalar_prefetch=2,
            grid=(n_tiles,),
            in_specs=[pl.BlockSpec((TM_E * TOK_ROWS, LANES), tile),
                      pl.BlockSpec((1, 1, D_MODEL, D_EXP), lambda i, te, nu: (layer, te[i], 0, 0)),
                      pl.BlockSpec((1, 1, D_MODEL, D_EXP), lambda i, te, nu: (layer, te[i], 0, 0)),
                      pl.BlockSpec((1, 1, D_EXP, D_MODEL), lambda i, te, nu: (layer, te[i], 0, 0))],
            out_specs=pl.BlockSpec((TM_E * TOK_ROWS, LANES), lambda i, te, nu: (i, 0))),
        out_shape=jax.ShapeDtypeStruct(xs.shape, F32),
        compiler_params=_cp(("arbitrary",)),
        name="experts",
    )(tile_expert, n_used, xs, wg, wu, wd)


def _rope_tables(S):
    inv = 1.0 / (ROPE_THETA ** (jnp.arange(0, ROPE, 2, dtype=F32) / ROPE))
    ang = jnp.arange(S, dtype=F32)[:, None] * inv[None, :]
    cos, sin = jnp.cos(ang), jnp.sin(ang)
    z = lambda n: jnp.zeros((S, n), F32)
    scale = (NOPE + ROPE) ** -0.5 * np.log2(np.e)
    pad = LANES - NOPE - ROPE
    cq = scale * jnp.concatenate([jnp.ones((S, NOPE), F32), cos, cos, z(pad)], axis=1)
    sq = scale * jnp.concatenate([z(NOPE), -sin, sin, z(pad)], axis=1)
    ck = jnp.concatenate([z(NOPE), cos, cos, z(pad)], axis=1)
    sk = jnp.concatenate([z(NOPE), -sin, sin, z(pad)], axis=1)
    return cq, sq, ck, sk


def _block_diag(w):
    n, d, _ = w.shape
    return (jnp.eye(n, dtype=w.dtype)[:, None, :, None] * w[:, :, None, :]).reshape(n * d, n * d)


def _in_weights(w_in):
    z = lambda n: jnp.zeros((D_MODEL, n), F32)
    o = 4 * ML_W
    mi, mf = w_in[:, o:o + ML_HEADS], w_in[:, o + ML_HEADS:o + 2 * ML_HEADS]
    o += 2 * ML_HEADS
    rest = w_in[:, o:o + 2 * LRU_W + Q_LORA + KV_LORA]
    kr = w_in[:, o + 2 * LRU_W + Q_LORA + KV_LORA:]
    small = jnp.concatenate([mi, mf, z(NOPE - 2 * ML_HEADS), kr, z(LANES - NOPE - ROPE)], axis=1)
    return jnp.concatenate([w_in[:, :4 * ML_W], rest, small], axis=1).astype(BF16)


def _routing_tables(cnt_rows):
    n_st = cnt_rows.shape[0] // R_LOG
    rows = cnt_rows.reshape(n_st, R_LOG, LANES)
    cnt = rows[:, N_GROUPS:N_GROUPS + N_EXP, 0].astype(jnp.int32)
    lst = rows[:, N_GROUPS:N_GROUPS + N_EXP, 1].astype(jnp.int32)
    tiles_e = (jnp.sum(cnt, axis=0) + TM_E - 1) // TM_E
    tile_cum = jnp.cumsum(tiles_e)
    base = (tile_cum - tiles_e) * TM_E
    gst = base[None, :] + jnp.cumsum(cnt, axis=0) - cnt
    n_used = tile_cum[-1:].astype(jnp.int32)
    n_e = jnp.sum(cnt, axis=0)
    pad = jnp.concatenate([base + n_e, tiles_e * TM_E - n_e, n_used]).astype(jnp.int32)
    return cnt.reshape(-1), lst.reshape(-1), gst.reshape(-1), pad, tile_cum, n_used


def kernel(x, norm1_g, w_in, ml_i_bias, ml_f_bias, ml_norm_g, conv_w, conv_b, lru_w_r, lru_b_r, lru_w_i, lru_b_i, lru_lambda, lru_norm_g, q_norm_g, w_uq, kv_norm_g, w_ukv, mla_norm_g, w_out, norm2_g, w_group, b_group, w_expert, b_expert, w_gate, w_up, w_down, final_norm_g):
    B, S, _ = x.shape
    T = B * S
    depth = w_in.shape[0]
    n_tiles = (2 * T) // TM_E + N_EXP
    tabs = _rope_tables(S)
    vones = jnp.tile(jnp.concatenate([jnp.zeros((MLA_V,), F32), jnp.ones((V_EXT - MLA_V,), F32)]),
                     MLA_H).reshape(MLA_H * V_EXT, 1)
    row = lambda a: a.reshape(1, -1)
    x2d = x.reshape(T, D_MODEL)
    for l in range(depth):
        proj, small = in_proj(x2d, row(norm1_g[l]), _in_weights(w_in[l]))
        gate_bias = jnp.concatenate([ml_i_bias[l], ml_f_bias[l], jnp.zeros((LANES - 2 * ML_HEADS,), F32)])
        proj3 = proj.reshape(B, S, SMALL_COL)
        y_m = mlstm(proj3, small.reshape(B, S, LANES), row(gate_bias),
                    ml_norm_g[l].reshape(ML_W, 1)).reshape(T, ML_W)
        w_gates = jnp.concatenate([_block_diag(lru_w_r[l]), _block_diag(lru_w_i[l])], axis=1).astype(BF16)
        b_gates = jnp.concatenate([lru_b_r[l], lru_b_i[l]])
        y_r = rglru(proj3, conv_w[l], row(conv_b[l]), w_gates, row(b_gates),
                    row(lru_lambda[l]), row(lru_norm_g[l])).reshape(T, LRU_W)
        wq3 = w_uq[l].reshape(Q_LORA, MLA_H, NOPE + ROPE)
        wq = jnp.pad(wq3, ((0, 0), (0, 0), (0, LANES - NOPE - ROPE))).reshape(Q_LORA, MLA_H * LANES).astype(BF16)
        wqs = jnp.concatenate([jnp.zeros((Q_LORA, MLA_H, NOPE), F32), wq3[:, :, NOPE + ROPE // 2:],
                               wq3[:, :, NOPE:NOPE + ROPE // 2],
                               jnp.zeros((Q_LORA, MLA_H, LANES - NOPE - ROPE), F32)],
                              axis=2).reshape(Q_LORA, MLA_H * LANES).astype(BF16)
        wkv = w_ukv[l].reshape(KV_LORA, MLA_H, NOPE + MLA_V)
        wk = jnp.pad(wkv[:, :, :NOPE], ((0, 0), (0, 0), (0, LANES - NOPE))).reshape(KV_LORA, MLA_H * LANES).astype(BF16)
        wvt = jnp.pad(wkv[:, :, NOPE:], ((0, 0), (0, 0), (0, V_EXT - MLA_V))).reshape(KV_LORA, MLA_H * V_EXT).T.astype(BF16)
        q, k, vt = mla_proj(proj, small, row(q_norm_g[l]), row(kv_norm_g[l]), wq, wqs, wk, wvt, vones, tabs, S)
        y_a = attention(q, k, vt, mla_norm_g[l].reshape(MLA_W, 1), B, S)
        w_router = jnp.concatenate([w_group[l], w_expert[l],
                                    jnp.zeros((D_MODEL, R_LOG - N_GROUPS - N_EXP), F32)], axis=1).T
        wr_hi = w_router.astype(BF16)
        wr_lo = (w_router - wr_hi.astype(F32)).astype(BF16)
        b_router = jnp.concatenate([b_group[l], b_expert[l], jnp.zeros((R_LOG - N_GROUPS - N_EXP,), F32)])
        x_mid, t_tiles, logits = out_proj(x2d, y_m, y_r, y_a, w_out[l].astype(BF16), row(norm2_g[l]),
                                          wr_hi, wr_lo, b_router.reshape(R_LOG, 1))
        meta, cnt_rows = route(logits)
        meta = meta.reshape(T // ST, SUBLANES, ST)
        lpos = meta[:, 0:2, :].astype(jnp.int32).reshape(-1)
        w_pick = meta[:, 2:4, :]
        cnt, lst, gst, pad, tile_cum, n_used = _routing_tables(cnt_rows)
        tile_ids = jnp.minimum(jnp.arange(n_tiles, dtype=jnp.int32), n_used[0] - 1)
        tile_expert = jnp.minimum(jnp.sum(tile_ids[:, None] >= tile_cum[None, :], axis=1),
                                  N_EXP - 1).astype(jnp.int32)
        xs = dispatch(lpos, cnt, lst, gst, pad, t_tiles, n_tiles * TM_E)
        ys = experts(tile_expert, n_used, xs, w_gate, w_up, w_down, l)
        x2d = combine(lpos, cnt, lst, gst, w_pick, ys, x_mid, row(final_norm_g), l == depth - 1)
    return x2d.reshape(B, S, D_MODEL)
```

```python
import functools

import numpy as np
import jax
import jax.numpy as jnp
from jax import lax
from jax.experimental import pallas as pl
from jax.experimental.pallas import tpu as pltpu

F32 = jnp.float32
BF16 = jnp.bfloat16

D_MODEL = 1024
ML_HEADS = 4
ML_DH = 64
ML_W = ML_HEADS * ML_DH
LRU_W = 256
LRU_BLOCKS = 4
CONV_W = 4
LRU_C = 8.0
MLA_H = 8
NOPE = 64
ROPE = 32
MLA_V = 64
MLA_W = MLA_H * MLA_V
Q_LORA = 256
KV_LORA = 128
ROPE_THETA = 10000.0
N_GROUPS = 4
EPG = 8
N_EXP = N_GROUPS * EPG
D_EXP = 256
EPS = 1e-6

LANES = 128
SUBLANES = 8
V_EXT = MLA_V + SUBLANES
TOK_ROWS = D_MODEL // LANES
R_LOG = 40
PROJ_W = 2048
SMALL_COL = 1920
VMEM_LIMIT = 56 * 1024 * 1024

ML_CHUNK = 128
ML_NB = 2
LRU_TC = 256
TM_IN = 512
TQ = 256
ST = 1024
TM_E = 256
SLOT_ROWS = 2 * ST * TOK_ROWS


def _cp(sem):
    return pltpu.CompilerParams(dimension_semantics=sem, vmem_limit_bytes=VMEM_LIMIT)


def _rms(x, g):
    return x * lax.rsqrt(jnp.mean(x * x, axis=-1, keepdims=True) + EPS) * g


def _in_proj_kernel(x_ref, g_ref, w_ref, o_ref, small_ref):
    y = _rms(x_ref[...], g_ref[...])
    acc = jnp.dot(y.astype(BF16), w_ref[...], preferred_element_type=F32)
    o_ref[...] = acc[:, :SMALL_COL].astype(BF16)
    small_ref[...] = acc[:, SMALL_COL:]


def in_proj(x2d, g, w):
    T = x2d.shape[0]
    return pl.pallas_call(
        _in_proj_kernel,
        grid=(T // TM_IN,),
        in_specs=[pl.BlockSpec((TM_IN, D_MODEL), lambda i: (i, 0)),
                  pl.BlockSpec((1, D_MODEL), lambda i: (0, 0)),
                  pl.BlockSpec((D_MODEL, PROJ_W), lambda i: (0, 0))],
        out_specs=[pl.BlockSpec((TM_IN, SMALL_COL), lambda i: (i, 0)),
                   pl.BlockSpec((TM_IN, LANES), lambda i: (i, 0))],
        out_shape=[jax.ShapeDtypeStruct((T, SMALL_COL), BF16),
                   jax.ShapeDtypeStruct((T, LANES), F32)],
        compiler_params=_cp(("arbitrary",)),
        name="in_proj",
    )(x2d, g, w)


def _mlstm_kernel(q_ref, k_ref, v_ref, o_ref, s_ref, bias_ref, g_ref, y_ref, c_ref, m_ref):
    nb, L, _ = q_ref.shape
    c = pl.program_id(1)

    @pl.when(c == 0)
    def _():
        c_ref[...] = jnp.zeros_like(c_ref)
        m_ref[...] = jnp.zeros_like(m_ref)

    lane = lax.broadcasted_iota(jnp.int32, (L, LANES), 1)
    row_i = lax.broadcasted_iota(jnp.int32, (L, L), 0)
    col_i = lax.broadcasted_iota(jnp.int32, (L, L), 1)
    lower = (col_i <= row_i).astype(F32)
    causal_t = row_i <= col_i
    nt = (((1,), (1,)), ((), ()))
    ones = jnp.ones((SUBLANES, L), F32)
    units = [(bi, h) for bi in range(nb) for h in range(ML_HEADS)]

    gates, cums = [], []
    for bi in range(nb):
        pre = s_ref[bi] + bias_ref[...]
        logsig = jnp.minimum(pre, 0.0) - jnp.log1p(jnp.exp(-jnp.abs(pre)))
        gate = jnp.where(lane < ML_HEADS, pre, logsig)
        gates.append(gate)
        cums.append(jnp.dot(lower, gate, precision=lax.Precision.HIGHEST, preferred_element_type=F32))

    qs, ks, states, scores, qcs = {}, {}, {}, {}, {}
    for bi, h in units:
        sl = slice(h * ML_DH, (h + 1) * ML_DH)
        qs[bi, h] = q_ref[bi, :, sl].astype(BF16)
        ks[bi, h] = (k_ref[bi, :, sl].astype(F32) * (ML_DH ** -0.5)).astype(BF16)
        states[bi, h] = c_ref[bi, h]
        scores[bi, h] = lax.dot_general(ks[bi, h], qs[bi, h], nt, preferred_element_type=F32)
        qcs[bi, h] = lax.dot_general(states[bi, h].astype(BF16), qs[bi, h], nt, preferred_element_type=F32)

    gates_t = [g.T for g in gates]
    cums_t = [cu.T for cu in cums]
    v_t = [v_ref[bi].astype(F32).T for bi in range(nb)]
    o_t = [jax.nn.sigmoid(o_ref[bi].astype(F32)).T for bi in range(nb)]

    s_t, m_rows, w_inters, nums, d_states, decays = {}, {}, {}, {}, {}, {}
    for bi, h in units:
        sl = slice(h * ML_DH, (h + 1) * ML_DH)
        a_col = gates[bi][:, h:h + 1] - cums[bi][:, ML_HEADS + h:ML_HEADS + h + 1]
        b_row = cums_t[bi][ML_HEADS + h:ML_HEADS + h + 1, :]
        li_row = gates_t[bi][h:h + 1, :]
        m_prev = m_ref[bi, :, h:h + 1]
        dmat = jnp.where(causal_t, a_col + b_row, -jnp.inf)
        inter = b_row + m_prev
        m_row = jnp.maximum(jnp.max(dmat, axis=0, keepdims=True), inter)
        w_inters[bi, h] = jnp.exp(inter - m_row)
        m_rows[bi, h] = m_row
        s_t[bi, h] = scores[bi, h] * jnp.exp(dmat - m_row)
        b_last = b_row[:, L - 1:L]
        g_row = b_last - b_row + li_row
        m_new = jnp.maximum(b_last + m_prev, jnp.max(g_row, axis=1, keepdims=True))
        decays[bi, h] = jnp.exp(b_last + m_prev - m_new)
        m_ref[bi, :, h:h + 1] = m_new
        vt_ext = jnp.concatenate([v_t[bi][sl, :], ones], axis=0) * jnp.exp(g_row - m_new)
        nums[bi, h] = jnp.dot(v_t[bi][sl, :].astype(BF16), s_t[bi, h].astype(BF16), preferred_element_type=F32)
        d_states[bi, h] = jnp.dot(vt_ext.astype(BF16), ks[bi, h], preferred_element_type=F32)

    for bi in range(nb):
        ys = []
        for h in range(ML_HEADS):
            sl = slice(h * ML_DH, (h + 1) * ML_DH)
            qc, w_inter = qcs[bi, h], w_inters[bi, h]
            den = jnp.sum(s_t[bi, h], axis=0, keepdims=True) + w_inter * qc[ML_DH:ML_DH + 1, :]
            hh = (nums[bi, h] + w_inter * qc[:ML_DH, :]) / jnp.maximum(jnp.abs(den), jnp.exp(-m_rows[bi, h]))
            y = o_t[bi][sl, :] * hh
            ys.append(y * lax.rsqrt(jnp.mean(y * y, axis=0, keepdims=True) + EPS) * g_ref[sl, :])
            c_ref[bi, h] = decays[bi, h] * states[bi, h] + d_states[bi, h]
        y_ref[bi] = jnp.concatenate(ys, axis=0).T.astype(BF16)


def mlstm(proj3, small3, bias, g):
    B, S, _ = proj3.shape
    L = ML_CHUNK
    nb = ML_NB
    col = lambda j: pl.BlockSpec((nb, L, ML_W), lambda b, c, j=j: (b, c, j))
    return pl.pallas_call(
        _mlstm_kernel,
        grid=(B // nb, S // L),
        in_specs=[col(0), col(1), col(2), col(3),
                  pl.BlockSpec((nb, L, LANES), lambda b, c: (b, c, 0)),
                  pl.BlockSpec((1, LANES), lambda b, c: (0, 0)),
                  pl.BlockSpec((ML_W, 1), lambda b, c: (0, 0))],
        out_specs=pl.BlockSpec((nb, L, ML_W), lambda b, c: (b, c, 0)),
        out_shape=jax.ShapeDtypeStruct((B, S, ML_W), BF16),
        scratch_shapes=[pltpu.VMEM((nb, ML_HEADS, ML_DH + SUBLANES, ML_DH), F32),
                        pltpu.VMEM((nb, 1, LANES), F32)],
        compiler_params=_cp(("arbitrary", "arbitrary")),
        name="mlstm",
    )(proj3, proj3, proj3, proj3, small3, bias, g)


def _rglru_kernel(x_ref, gate_ref, cw_ref, cb_ref, wg_ref, bg_ref, lam_ref, g_ref, y_ref,
                  xp_ref, a_ref, u_ref, h_ref, carry_ref):
    B, tc, W = x_ref.shape
    c = pl.program_id(0)

    @pl.when(c == 0)
    def _():
        xp_ref[:, 0:SUBLANES, :] = jnp.zeros((B, SUBLANES, W), F32)
        carry_ref[...] = jnp.zeros_like(carry_ref)

    xp_ref[:, SUBLANES:, :] = x_ref[...].astype(F32)
    xc = jnp.zeros((B, tc, W), F32) + cb_ref[...]
    for j in range(CONV_W):
        off = SUBLANES - (CONV_W - 1) + j
        xc = xc + cw_ref[j:j + 1, :] * xp_ref[:, off:off + tc, :]
    xp_ref[:, 0:SUBLANES, :] = x_ref[:, tc - SUBLANES:, :].astype(F32)

    xc2 = xc.reshape(B * tc, W)
    ri = jax.nn.sigmoid(jnp.dot(xc2.astype(BF16), wg_ref[...], preferred_element_type=F32) + bg_ref[...])
    r = ri[:, :W]
    ig = ri[:, W:]
    lam = lam_ref[...]
    softplus_neg = jnp.maximum(-lam, 0.0) + jnp.log1p(jnp.exp(-jnp.abs(lam)))
    log_a = -LRU_C * r * softplus_neg
    a = jnp.exp(log_a)
    u = jnp.sqrt(1.0 - jnp.exp(2.0 * log_a)) * (ig * xc2)
    halves = W // LANES
    for k in range(halves):
        a_ref[k] = a[:, k * LANES:(k + 1) * LANES]
        u_ref[k] = u[:, k * LANES:(k + 1) * LANES]

    def step(i, hs):
        base = pl.multiple_of(i * SUBLANES, SUBLANES)
        for j in range(SUBLANES):
            rows = pl.ds(base + j, B, stride=tc)
            hs = tuple(a_ref.at[k][rows, :] * hs[k] + u_ref.at[k][rows, :] for k in range(halves))
            for k in range(halves):
                h_ref.at[k][rows, :] = hs[k]
        return hs

    init = tuple(carry_ref[k] for k in range(halves))
    last = lax.fori_loop(0, tc // SUBLANES, step, init)
    for k in range(halves):
        carry_ref[k] = last[k]
    h_all = jnp.concatenate([h_ref[k] for k in range(halves)], axis=1)
    y = h_all.reshape(B, tc, W) * jax.nn.gelu(gate_ref[...].astype(F32))
    y_ref[...] = _rms(y, g_ref[...]).astype(BF16)


def rglru(proj3, cw, cb, wg, bg, lam, g):
    B, S, _ = proj3.shape
    tc = LRU_TC
    vec = lambda n: pl.BlockSpec((1, n), lambda c: (0, 0))
    return pl.pallas_call(
        _rglru_kernel,
        grid=(S // tc,),
        in_specs=[pl.BlockSpec((B, tc, LRU_W), lambda c: (0, c, 4)),
                  pl.BlockSpec((B, tc, LRU_W), lambda c: (0, c, 5)),
                  pl.BlockSpec((CONV_W, LRU_W), lambda c: (0, 0)),
                  vec(LRU_W),
                  pl.BlockSpec((LRU_W, 2 * LRU_W), lambda c: (0, 0)),
                  vec(2 * LRU_W), vec(LRU_W), vec(LRU_W)],
        out_specs=pl.BlockSpec((B, tc, LRU_W), lambda c: (0, c, 0)),
        out_shape=jax.ShapeDtypeStruct((B, S, LRU_W), BF16),
        scratch_shapes=[pltpu.VMEM((B, tc + SUBLANES, LRU_W), F32),
                        pltpu.VMEM((LRU_W // LANES, B * tc, LANES), F32),
                        pltpu.VMEM((LRU_W // LANES, B * tc, LANES), F32),
                        pltpu.VMEM((LRU_W // LANES, B * tc, LANES), F32),
                        pltpu.VMEM((LRU_W // LANES, B, LANES), F32)],
        compiler_params=_cp(("arbitrary",)),
        name="rglru",
    )(proj3, proj3, cw, cb, wg, bg, lam, g)


def _rot_half(blk):
    lane = lax.broadcasted_iota(jnp.int32, blk.shape, 1)
    return jnp.where(lane < NOPE + ROPE // 2,
                     pltpu.roll(blk, LANES - ROPE // 2, 1), pltpu.roll(blk, ROPE // 2, 1))


def _mla_proj_kernel(cq_ref, ckv_ref, small_ref, qg_ref, kvg_ref, wq_ref, wqs_ref, wk_ref, wvt_ref, vones_ref,
                     cq_t, sq_t, ck_t, sk_t, q_ref, k_ref, vt_ref):
    qn = _rms(cq_ref[...].astype(F32), qg_ref[...]).astype(BF16)
    qf = jnp.dot(qn, wq_ref[...], preferred_element_type=F32)
    qs = jnp.dot(qn, wqs_ref[...], preferred_element_type=F32)
    kvn = _rms(ckv_ref[...].astype(F32), kvg_ref[...]).astype(BF16)
    kf = jnp.dot(kvn, wk_ref[...], preferred_element_type=F32)
    vt_ref[...] = (lax.dot_general(wvt_ref[...], kvn, (((1,), (1,)), ((), ())), preferred_element_type=F32)
                   + vones_ref[...]).astype(BF16)
    small = small_ref[...]
    kr = small * ck_t[...] + _rot_half(small) * sk_t[...]
    cq = cq_t[...]
    sq = sq_t[...]
    for h in range(MLA_H):
        sl = slice(h * LANES, (h + 1) * LANES)
        q_ref[:, sl] = (qf[:, sl] * cq + qs[:, sl] * sq).astype(BF16)
        k_ref[:, sl] = (kf[:, sl] + kr).astype(BF16)


def mla_proj(proj, small, qg, kvg, wq, wqs, wk, wvt, vones, tabs, S):
    T = proj.shape[0]
    tm = TM_IN
    npos = S // tm
    tab = pl.BlockSpec((tm, LANES), lambda i: (i % npos, 0))
    full = lambda a: pl.BlockSpec(a.shape, lambda i: (0, 0))
    return pl.pallas_call(
        _mla_proj_kernel,
        grid=(T // tm,),
        in_specs=[pl.BlockSpec((tm, Q_LORA), lambda i: (i, 6)),
                  pl.BlockSpec((tm, KV_LORA), lambda i: (i, (SMALL_COL - KV_LORA) // KV_LORA)),
                  pl.BlockSpec((tm, LANES), lambda i: (i, 0)),
                  full(qg), full(kvg), full(wq), full(wqs), full(wk), full(wvt), full(vones),
                  tab, tab, tab, tab],
        out_specs=[pl.BlockSpec((tm, MLA_H * LANES), lambda i: (i, 0)),
                   pl.BlockSpec((tm, MLA_H * LANES), lambda i: (i, 0)),
                   pl.BlockSpec((MLA_H * V_EXT, tm), lambda i: (0, i))],
        out_shape=[jax.ShapeDtypeStruct((T, MLA_H * LANES), BF16),
                   jax.ShapeDtypeStruct((T, MLA_H * LANES), BF16),
                   jax.ShapeDtypeStruct((MLA_H * V_EXT, T), BF16)],
        compiler_params=_cp(("arbitrary",)),
        name="mla_proj",
    )(proj, proj, small, qg, kvg, wq, wqs, wk, wvt, vones, *tabs)


def _attn_kernel(q_ref, k_ref, vt_ref, g_ref, o_ref, acc_ref, m_ref, yt_ref):
    tq = q_ref.shape[0]
    i = pl.program_id(1)
    key_i = lax.broadcasted_iota(jnp.int32, (tq, tq), 0)
    qry_i = lax.broadcasted_iota(jnp.int32, (tq, tq), 1)
    causal = key_i <= qry_i
    nt = (((1,), (1,)), ((), ()))
    m_ref[...] = jnp.full_like(m_ref, -jnp.inf)
    acc_ref[...] = jnp.zeros_like(acc_ref)

    ahead = 4

    def scores(off, h):
        hsl = slice(h * LANES, (h + 1) * LANES)
        return lax.dot_general(k_ref[pl.ds(off, tq), hsl], q_ref[:, hsl], nt, preferred_element_type=F32)

    def update(h, alpha, pv):
        rows = slice(h * V_EXT, (h + 1) * V_EXT)
        acc_ref[rows, :] = alpha * acc_ref[rows, :] + pv

    def blocks(items):
        work = [(pl.multiple_of(j * tq, tq), h, masked) for j, masked in items for h in range(MLA_H)]
        pending = [scores(off, h) for off, h, _ in work[:ahead]]
        late = None
        for n, (off, h, masked) in enumerate(work):
            s = pending.pop(0)
            if n + ahead < len(work):
                pending.append(scores(*work[n + ahead][:2]))
            if masked:
                s = jnp.where(causal, s, -jnp.inf)
            m = m_ref[h:h + 1, :]
            m_new = jnp.maximum(m, jnp.max(s, axis=0, keepdims=True))
            m_ref[h:h + 1, :] = m_new
            alpha = jnp.exp2(m - m_new)
            p = jnp.exp2(s - m_new).astype(BF16)
            pv = jnp.dot(vt_ref[h * V_EXT:(h + 1) * V_EXT, pl.ds(off, tq)], p, preferred_element_type=F32)
            if late is not None:
                update(*late)
            late = (h, alpha, pv)
        update(*late)

    def body(pair, carry):
        blocks([(2 * pair, False), (2 * pair + 1, False)])
        return carry

    lax.fori_loop(0, i // 2, body, 0)

    @pl.when(i % 2 == 1)
    def _():
        blocks([(i - 1, False), (i, True)])

    @pl.when(i % 2 == 0)
    def _():
        blocks([(i, True)])

    ssq = jnp.zeros((1, tq), F32)
    for h in range(MLA_H):
        vrows = slice(h * MLA_V, (h + 1) * MLA_V)
        yh = acc_ref[h * V_EXT:h * V_EXT + MLA_V, :] * (1.0 / acc_ref[h * V_EXT + MLA_V:h * V_EXT + MLA_V + 1, :])
        yt_ref[vrows, :] = yh
        ssq = ssq + jnp.sum(yh * yh, axis=0, keepdims=True)
    yt = yt_ref[...] * lax.rsqrt(ssq * (1.0 / MLA_W) + EPS) * g_ref[...]
    o_ref[...] = yt.T.astype(BF16)


def attention(q, k, vt, g, B, S):
    T = B * S
    nq = S // TQ
    return pl.pallas_call(
        _attn_kernel,
        grid=(B, nq),
        in_specs=[pl.BlockSpec((TQ, MLA_H * LANES), lambda b, i: (b * nq + i, 0)),
                  pl.BlockSpec((S, MLA_H * LANES), lambda b, i: (b, 0)),
                  pl.BlockSpec((MLA_H * V_EXT, S), lambda b, i: (0, b)),
                  pl.BlockSpec((MLA_W, 1), lambda b, i: (0, 0))],
        out_specs=pl.BlockSpec((TQ, MLA_W), lambda b, i: (b * nq + i, 0)),
        out_shape=jax.ShapeDtypeStruct((T, MLA_W), BF16),
        scratch_shapes=[pltpu.VMEM((MLA_H * V_EXT, TQ), F32),
                        pltpu.VMEM((MLA_H, TQ), F32),
                        pltpu.VMEM((MLA_W, TQ), F32)],
        compiler_params=_cp(("arbitrary", "arbitrary")),
        name="attention",
    )(q, k, vt, g)


def _out_proj_kernel(x_ref, ym_ref, yr_ref, ya_ref, w_ref, g_ref, wrh_ref, wrl_ref, br_ref,
                     xm_ref, t_ref, lg_ref):
    acc = jnp.dot(ym_ref[...].astype(BF16), w_ref[0:ML_W, :], preferred_element_type=F32)
    acc += jnp.dot(yr_ref[...].astype(BF16), w_ref[ML_W:ML_W + LRU_W, :], preferred_element_type=F32)
    acc += jnp.dot(ya_ref[...].astype(BF16), w_ref[ML_W + LRU_W:, :], preferred_element_type=F32)
    xm = x_ref[...] + acc
    xm_ref[...] = xm
    t = _rms(xm, g_ref[...])
    t_hi = t.astype(BF16)
    t_lo = (t - t_hi.astype(F32)).astype(BF16)
    nt = (((1,), (1,)), ((), ()))
    lg = lax.dot_general(wrh_ref[...], t_hi, nt, preferred_element_type=F32)
    lg += lax.dot_general(wrl_ref[...], t_hi, nt, preferred_element_type=F32)
    lg += lax.dot_general(wrh_ref[...], t_lo, nt, preferred_element_type=F32)
    lg_ref[...] = lg + br_ref[...]
    for s in range(TOK_ROWS):
        t_ref[pl.ds(s, t.shape[0], stride=TOK_ROWS), :] = t[:, s * LANES:(s + 1) * LANES]


def out_proj(x2d, ym, yr, ya, w, g, wr_hi, wr_lo, br):
    T = x2d.shape[0]
    tm = TM_IN
    row = lambda n: pl.BlockSpec((tm, n), lambda i: (i, 0))
    full = lambda a: pl.BlockSpec(a.shape, lambda i: (0, 0))
    return pl.pallas_call(
        _out_proj_kernel,
        grid=(T // tm,),
        in_specs=[row(D_MODEL), row(ML_W), row(LRU_W), row(MLA_W), full(w), full(g),
                  full(wr_hi), full(wr_lo), full(br)],
        out_specs=[row(D_MODEL),
                   pl.BlockSpec((tm * TOK_ROWS, LANES), lambda i: (i, 0)),
                   pl.BlockSpec((R_LOG, tm), lambda i: (0, i))],
        out_shape=[jax.ShapeDtypeStruct((T, D_MODEL), F32),
                   jax.ShapeDtypeStruct((T * TOK_ROWS, LANES), F32),
                   jax.ShapeDtypeStruct((R_LOG, T), F32)],
        compiler_params=_cp(("arbitrary",)),
        name="out_proj",
    )(x2d, ym, yr, ya, w, g, wr_hi, wr_lo, br)


def _route_kernel(lg_ref, meta_ref, cnt_ref):
    st = lg_ref.shape[1]
    lg = lg_ref[...]
    row = lax.broadcasted_iota(jnp.int32, (R_LOG, st), 0)
    col_max = lambda a: jnp.max(a, axis=0, keepdims=True)
    col_sum = lambda a: jnp.sum(a, axis=0, keepdims=True)
    first = lambda cond: jnp.min(jnp.where(cond, row, R_LOG), axis=0, keepdims=True)

    gmask = row < N_GROUPS
    gl = jnp.where(gmask, lg, -jnp.inf)
    ge = jnp.exp(gl - col_max(gl))
    gp = ge / col_sum(ge)
    g_val = col_max(gp)
    g_idx = first(gmask & (gp == g_val))

    smask = (row >= N_GROUPS) & (row < N_GROUPS + N_EXP) & (((row - N_GROUPS) >> 3) == g_idx)
    el = jnp.where(smask, lg, -jnp.inf)
    ee = jnp.exp(el - col_max(el))
    ep = ee / col_sum(ee)
    v1 = col_max(ep)
    i1 = first(smask & (ep == v1))
    rest = smask & (row != i1)
    v2 = col_max(jnp.where(rest, ep, -1.0))
    i2 = first(rest & (ep == v2))
    den = v1 + v2
    w1 = v1 / den * g_val
    w2 = v2 / den * g_val

    onehot = ((row == i1) | (row == i2)).astype(F32)
    sb = min(st, 256)
    r_i = lax.broadcasted_iota(jnp.int32, (sb, sb), 0)
    c_i = lax.broadcasted_iota(jnp.int32, (sb, sb), 1)
    before = (r_i < c_i).astype(BF16)
    run = jnp.zeros((R_LOG, 1), F32)
    ranks = []
    for j in range(st // sb):
        oh = onehot[:, j * sb:(j + 1) * sb]
        ranks.append(jnp.dot(oh.astype(BF16), before, preferred_element_type=F32) + run)
        run = run + jnp.sum(oh, axis=1, keepdims=True)
    rank = jnp.concatenate(ranks, axis=1)
    e_r = lax.broadcasted_iota(jnp.int32, (R_LOG, R_LOG), 0)
    e_c = lax.broadcasted_iota(jnp.int32, (R_LOG, R_LOG), 1)
    counts = jnp.broadcast_to(run, (R_LOG, LANES))
    lstart = jnp.dot((e_c < e_r).astype(F32), counts, precision=lax.Precision.HIGHEST,
                     preferred_element_type=F32)
    slot = rank + lstart[:, 0:1]
    p1 = col_sum(jnp.where(row == i1, slot, 0.0))
    p2 = col_sum(jnp.where(row == i2, slot, 0.0))
    meta_ref[...] = jnp.concatenate([p1, p2, w1, w2, jnp.zeros((SUBLANES - 4, st), F32)], axis=0)
    lane = lax.broadcasted_iota(jnp.int32, (R_LOG, LANES), 1)
    cnt_ref[...] = jnp.where(lane == 0, counts, jnp.where(lane == 1, lstart, 0.0))


def route(logits_t):
    T = logits_t.shape[1]
    n_st = T // ST
    return pl.pallas_call(
        _route_kernel,
        grid=(n_st,),
        in_specs=[pl.BlockSpec((R_LOG, ST), lambda i: (0, i))],
        out_specs=[pl.BlockSpec((SUBLANES, ST), lambda i: (i, 0)),
                   pl.BlockSpec((R_LOG, LANES), lambda i: (i, 0))],
        out_shape=[jax.ShapeDtypeStruct((n_st * SUBLANES, ST), F32),
                   jax.ShapeDtypeStruct((n_st * R_LOG, LANES), F32)],
        compiler_params=_cp(("arbitrary",)),
        name="route",
    )(logits_t)


def _copy_by_bits(n, nbits, copy_at, start):
    off = jnp.int32(0)
    for bit in reversed(range(nbits)):
        sz = 1 << bit

        @pl.when((n & sz) != 0)
        def _(off=off, sz=sz):
            cp = copy_at(off, sz)
            if start:
                cp.start()
            else:
                cp.wait()

        off = off + (n & sz)


def _strip_copies(st, cnt_ref, lst_ref, gst_ref, make_copy, start):
    def body(e, carry):
        loc = lst_ref[st * N_EXP + e]
        glob = gst_ref[st * N_EXP + e]
        _copy_by_bits(
            cnt_ref[st * N_EXP + e], int(np.log2(ST)) + 1,
            lambda off, sz: make_copy(pl.multiple_of((loc + off) * TOK_ROWS, TOK_ROWS),
                                      pl.multiple_of((glob + off) * TOK_ROWS, TOK_ROWS), sz * TOK_ROWS),
            start)
        return carry

    lax.fori_loop(0, N_EXP, body, 0)


def _zero_padding(pad_ref, xs_ref, zero_ref, sem, start):
    tile_rows = TM_E * TOK_ROWS

    def body(e, carry):
        first = pad_ref[e]
        _copy_by_bits(
            pad_ref[N_EXP + e], int(np.log2(TM_E)),
            lambda off, sz: pltpu.make_async_copy(
                zero_ref.at[pl.ds(0, sz * TOK_ROWS)],
                xs_ref.at[pl.ds(pl.multiple_of((first + off) * TOK_ROWS, TOK_ROWS), sz * TOK_ROWS)], sem),
            start)
        return carry

    lax.fori_loop(0, N_EXP, body, 0)

    def tail(i, carry):
        cp = pltpu.make_async_copy(
            zero_ref, xs_ref.at[pl.ds(pl.multiple_of(i * tile_rows, tile_rows), tile_rows)], sem)
        if start:
            cp.start()
        else:
            cp.wait()
        return carry

    lax.fori_loop(pad_ref[2 * N_EXP], xs_ref.shape[0] // tile_rows, tail, 0)


def _dispatch_kernel(lrow_ref, cnt_ref, lst_ref, gst_ref, pad_ref, t_ref, xs_ref,
                     stage_ref, zero_ref, sem, zsem):
    st = pl.program_id(0)
    last = pl.num_programs(0) - 1
    slot = st % 2

    @pl.when(st == 0)
    def _():
        zero_ref[...] = jnp.zeros_like(zero_ref)
        _zero_padding(pad_ref, xs_ref, zero_ref, zsem, True)

    def scatter(i, carry):
        pair0 = (st * ST + i * SUBLANES) * 2
        for u in range(SUBLANES):
            t = i * SUBLANES + u
            tile = t_ref[pl.ds(pl.multiple_of(t * TOK_ROWS, TOK_ROWS), TOK_ROWS), :]
            for k in range(2):
                row = pl.multiple_of(lrow_ref[pair0 + 2 * u + k], TOK_ROWS)
                stage_ref[pl.ds(row, TOK_ROWS), :] = tile
        return carry

    lax.fori_loop(0, ST // SUBLANES, scatter, 0)

    def slot_rows(buf, loc, rows):
        return stage_ref.at[pl.ds(pl.multiple_of(buf * SLOT_ROWS + loc, TOK_ROWS), rows)]

    def copies(step, start):
        buf = step % 2
        _strip_copies(
            step, cnt_ref, lst_ref, gst_ref,
            lambda loc, glob, rows: pltpu.make_async_copy(
                slot_rows(buf, loc, rows), xs_ref.at[pl.ds(glob, rows)], sem.at[buf]),
            start)

    def wait_all(buf):
        pltpu.make_async_copy(slot_rows(buf, 0, SLOT_ROWS), xs_ref.at[pl.ds(0, SLOT_ROWS)], sem.at[buf]).wait()

    copies(st, True)

    @pl.when(st > 0)
    def _():
        wait_all(1 - slot)

    @pl.when(st == last)
    def _():
        wait_all(slot)
        _zero_padding(pad_ref, xs_ref, zero_ref, zsem, False)


def dispatch(lrow, cnt, lst, gst, pad, t_tiles, n_rows):
    T = t_tiles.shape[0] // TOK_ROWS
    return pl.pallas_call(
        _dispatch_kernel,
        grid_spec=pltpu.PrefetchScalarGridSpec(
            num_scalar_prefetch=5,
            grid=(T // ST,),
            in_specs=[pl.BlockSpec((ST * TOK_ROWS, LANES), lambda i, *_: (i, 0))],
            out_specs=pl.BlockSpec(memory_space=pl.ANY),
            scratch_shapes=[pltpu.VMEM((2 * SLOT_ROWS, LANES), F32),
                            pltpu.VMEM((TM_E * TOK_ROWS, LANES), F32),
                            pltpu.SemaphoreType.DMA((2,)),
                            pltpu.SemaphoreType.DMA(())]),
        out_shape=jax.ShapeDtypeStruct((n_rows * TOK_ROWS, LANES), F32),
        compiler_params=_cp(("arbitrary",)),
        name="dispatch",
    )(lrow, cnt, lst, gst, pad, t_tiles)


def _combine_kernel(lrow_ref, w_ref, cnt_ref, lst_ref, gst_ref, ys_ref, xm_ref, fg_ref, o_ref,
                    stage_ref, comb_ref, sem, *, final):
    st = pl.program_id(0)
    last = pl.num_programs(0) - 1
    slot = st % 2

    def slot_rows(buf, loc, rows):
        return stage_ref.at[pl.ds(pl.multiple_of(buf * SLOT_ROWS + loc, TOK_ROWS), rows)]

    def copies(step, start):
        buf = step % 2
        _strip_copies(
            step, cnt_ref, lst_ref, gst_ref,
            lambda loc, glob, rows: pltpu.make_async_copy(
                ys_ref.at[pl.ds(glob, rows)], slot_rows(buf, loc, rows), sem.at[buf]),
            start)

    @pl.when(st == 0)
    def _():
        copies(st, True)

    @pl.when(st < last)
    def _():
        copies(st + 1, True)

    pltpu.make_async_copy(ys_ref.at[pl.ds(0, SLOT_ROWS)], slot_rows(slot, 0, SLOT_ROWS), sem.at[slot]).wait()

    def gather(i, carry):
        pair0 = (st * ST + i * SUBLANES) * 2
        for u in range(SUBLANES):
            t = i * SUBLANES + u
            acc = None
            for k in range(2):
                row = pl.multiple_of(lrow_ref[pair0 + 2 * u + k], TOK_ROWS)
                term = w_ref[pair0 + 2 * u + k] * stage_ref[pl.ds(row, TOK_ROWS), :]
                acc = term if acc is None else acc + term
            comb_ref[pl.ds(pl.multiple_of(t * TOK_ROWS, TOK_ROWS), TOK_ROWS), :] = acc
        return carry

    lax.fori_loop(0, ST // SUBLANES, gather, 0)
    moe = jnp.concatenate([comb_ref[pl.ds(s, ST, stride=TOK_ROWS), :] for s in range(TOK_ROWS)], axis=1)
    x_new = xm_ref[...] + moe
    o_ref[...] = _rms(x_new, fg_ref[...]) if final else x_new


def combine(lrow, w, cnt, lst, gst, ys, xm, fg, final):
    T = xm.shape[0]
    return pl.pallas_call(
        functools.partial(_combine_kernel, final=final),
        grid_spec=pltpu.PrefetchScalarGridSpec(
            num_scalar_prefetch=5,
            grid=(T // ST,),
            in_specs=[pl.BlockSpec(memory_space=pl.ANY),
                      pl.BlockSpec((ST, D_MODEL), lambda i, *_: (i, 0)),
                      pl.BlockSpec((1, D_MODEL), lambda i, *_: (0, 0))],
            out_specs=pl.BlockSpec((ST, D_MODEL), lambda i, *_: (i, 0)),
            scratch_shapes=[pltpu.VMEM((2 * SLOT_ROWS, LANES), F32),
                            pltpu.VMEM((ST * TOK_ROWS, LANES), F32),
                            pltpu.SemaphoreType.DMA((2,))]),
        out_shape=jax.ShapeDtypeStruct((T, D_MODEL), F32),
        compiler_params=_cp(("arbitrary",)),
        name="combine",
    )(lrow, w, cnt, lst, gst, ys, xm, fg)


def _expert_kernel(te_ref, nused_ref, xs_ref, wg_ref, wu_ref, wd_ref, ys_ref):
    i = pl.program_id(0)
    tm = xs_ref.shape[0] // TOK_ROWS

    @pl.when(i < nused_ref[0])
    def _():
        x = jnp.concatenate([xs_ref[pl.ds(s, tm, stride=TOK_ROWS), :] for s in range(TOK_ROWS)],
                            axis=1).astype(BF16)
        g = jnp.dot(x, wg_ref[0, 0].astype(BF16), preferred_element_type=F32)
        u = jnp.dot(x, wu_ref[0, 0].astype(BF16), preferred_element_type=F32)
        h = (g * jax.nn.sigmoid(g) * u).astype(BF16)
        y = jnp.dot(h, wd_ref[0, 0].astype(BF16), preferred_element_type=F32)
        for s in range(TOK_ROWS):
            ys_ref[pl.ds(s, tm, stride=TOK_ROWS), :] = y[:, s * LANES:(s + 1) * LANES]

    @pl.when(i >= nused_ref[0])
    def _():
        ys_ref[...] = jnp.zeros_like(ys_ref)


def experts(tile_expert, n_used, xs, wg, wu, wd, layer):
    n_tiles = xs.shape[0] // (TM_E * TOK_ROWS)
    tile = lambda i, te, nu: (jnp.maximum(jnp.minimum(i, nu[0] - 1), 0), 0)
    return pl.pallas_call(
        _expert_kernel,
        grid_spec=pltpu.PrefetchScalarGridSpec(
            num_scalar_prefetch=2,
            grid=(n_tiles,),
            in_specs=[pl.BlockSpec((TM_E * TOK_ROWS, LANES), tile),
                      pl.BlockSpec((1, 1, D_MODEL, D_EXP), lambda i, te, nu: (layer, te[i], 0, 0)),
                      pl.BlockSpec((1, 1, D_MODEL, D_EXP), lambda i, te, nu: (layer, te[i], 0, 0)),
                      pl.BlockSpec((1, 1, D_EXP, D_MODEL), lambda i, te, nu: (layer, te[i], 0, 0))],
            out_specs=pl.BlockSpec((TM_E * TOK_ROWS, LANES), lambda i, te, nu: (i, 0))),
        out_shape=jax.ShapeDtypeStruct(xs.shape, F32),
        compiler_params=_cp(("arbitrary",)),
        name="experts",
    )(tile_expert, n_used, xs, wg, wu, wd)


def _rope_tables(S):
    inv = 1.0 / (ROPE_THETA ** (jnp.arange(0, ROPE, 2, dtype=F32) / ROPE))
    ang = jnp.arange(S, dtype=F32)[:, None] * inv[None, :]
    cos, sin = jnp.cos(ang), jnp.sin(ang)
    z = lambda n: jnp.zeros((S, n), F32)
    scale = (NOPE + ROPE) ** -0.5 * np.log2(np.e)
    pad = LANES - NOPE - ROPE
    cq = scale * jnp.concatenate([jnp.ones((S, NOPE), F32), cos, cos, z(pad)], axis=1)
    sq = scale * jnp.concatenate([z(NOPE), -sin, sin, z(pad)], axis=1)
    ck = jnp.concatenate([z(NOPE), cos, cos, z(pad)], axis=1)
    sk = jnp.concatenate([z(NOPE), -sin, sin, z(pad)], axis=1)
    return cq, sq, ck, sk


def _block_diag(w):
    n, d, _ = w.shape
    return (jnp.eye(n, dtype=w.dtype)[:, None, :, None] * w[:, :, None, :]).reshape(n * d, n * d)


def _in_weights(w_in):
    z = lambda n: jnp.zeros((D_MODEL, n), F32)
    o = 4 * ML_W
    mi, mf = w_in[:, o:o + ML_HEADS], w_in[:, o + ML_HEADS:o + 2 * ML_HEADS]
    o += 2 * ML_HEADS
    rest = w_in[:, o:o + 2 * LRU_W + Q_LORA + KV_LORA]
    kr = w_in[:, o + 2 * LRU_W + Q_LORA + KV_LORA:]
    small = jnp.concatenate([mi, mf, z(NOPE - 2 * ML_HEADS), kr, z(LANES - NOPE - ROPE)], axis=1)
    return jnp.concatenate([w_in[:, :4 * ML_W], rest, small], axis=1).astype(BF16)


def _routing_tables(cnt_rows):
    n_st = cnt_rows.shape[0] // R_LOG
    rows = cnt_rows.reshape(n_st, R_LOG, LANES)
    cnt = rows[:, N_GROUPS:N_GROUPS + N_EXP, 0].astype(jnp.int32)
    lst = rows[:, N_GROUPS:N_GROUPS + N_EXP, 1].astype(jnp.int32)
    tiles_e = (jnp.sum(cnt, axis=0) + TM_E - 1) // TM_E
    tile_cum = jnp.cumsum(tiles_e)
    base = (tile_cum - tiles_e) * TM_E
    gst = base[None, :] + jnp.cumsum(cnt, axis=0) - cnt
    n_used = tile_cum[-1:].astype(jnp.int32)
    n_e = jnp.sum(cnt, axis=0)
    pad = jnp.concatenate([base + n_e, tiles_e * TM_E - n_e, n_used]).astype(jnp.int32)
    return cnt.reshape(-1), lst.reshape(-1), gst.reshape(-1), pad, tile_cum, n_used


def kernel(x, norm1_g, w_in, ml_i_bias, ml_f_bias, ml_norm_g, conv_w, conv_b, lru_w_r, lru_b_r, lru_w_i, lru_b_i, lru_lambda, lru_norm_g, q_norm_g, w_uq, kv_norm_g, w_ukv, mla_norm_g, w_out, norm2_g, w_group, b_group, w_expert, b_expert, w_gate, w_up, w_down, final_norm_g):
    B, S, _ = x.shape
    T = B * S
    depth = w_in.shape[0]
    n_tiles = (2 * T) // TM_E + N_EXP
    tabs = _rope_tables(S)
    vones = jnp.tile(jnp.concatenate([jnp.zeros((MLA_V,), F32), jnp.ones((V_EXT - MLA_V,), F32)]),
                     MLA_H).reshape(MLA_H * V_EXT, 1)
    row = lambda a: a.reshape(1, -1)
    x2d = x.reshape(T, D_MODEL)
    for l in range(depth):
        proj, small = in_proj(x2d, row(norm1_g[l]), _in_weights(w_in[l]))
        gate_bias = jnp.concatenate([ml_i_bias[l], ml_f_bias[l], jnp.zeros((LANES - 2 * ML_HEADS,), F32)])
        proj3 = proj.reshape(B, S, SMALL_COL)
        y_m = mlstm(proj3, small.reshape(B, S, LANES), row(gate_bias),
                    ml_norm_g[l].reshape(ML_W, 1)).reshape(T, ML_W)
        w_gates = jnp.concatenate([_block_diag(lru_w_r[l]), _block_diag(lru_w_i[l])], axis=1).astype(BF16)
        b_gates = jnp.concatenate([lru_b_r[l], lru_b_i[l]])
        y_r = rglru(proj3, conv_w[l], row(conv_b[l]), w_gates, row(b_gates),
                    row(lru_lambda[l]), row(lru_norm_g[l])).reshape(T, LRU_W)
        wq3 = w_uq[l].reshape(Q_LORA, MLA_H, NOPE + ROPE)
        wq = jnp.pad(wq3, ((0, 0), (0, 0), (0, LANES - NOPE - ROPE))).reshape(Q_LORA, MLA_H * LANES).astype(BF16)
        wqs = jnp.concatenate([jnp.zeros((Q_LORA, MLA_H, NOPE), F32), wq3[:, :, NOPE + ROPE // 2:],
                               wq3[:, :, NOPE:NOPE + ROPE // 2],
                               jnp.zeros((Q_LORA, MLA_H, LANES - NOPE - ROPE), F32)],
                              axis=2).reshape(Q_LORA, MLA_H * LANES).astype(BF16)
        wkv = w_ukv[l].reshape(KV_LORA, MLA_H, NOPE + MLA_V)
        wk = jnp.pad(wkv[:, :, :NOPE], ((0, 0), (0, 0), (0, LANES - NOPE))).reshape(KV_LORA, MLA_H * LANES).astype(BF16)
        wvt = jnp.pad(wkv[:, :, NOPE:], ((0, 0), (0, 0), (0, V_EXT - MLA_V))).reshape(KV_LORA, MLA_H * V_EXT).T.astype(BF16)
        q, k, vt = mla_proj(proj, small, row(q_norm_g[l]), row(kv_norm_g[l]), wq, wqs, wk, wvt, vones, tabs, S)
        y_a = attention(q, k, vt, mla_norm_g[l].reshape(MLA_W, 1), B, S)
        w_router = jnp.concatenate([w_group[l], w_expert[l],
                                    jnp.zeros((D_MODEL, R_LOG - N_GROUPS - N_EXP), F32)], axis=1).T
        wr_hi = w_router.astype(BF16)
        wr_lo = (w_router - wr_hi.astype(F32)).astype(BF16)
        b_router = jnp.concatenate([b_group[l], b_expert[l], jnp.zeros((R_LOG - N_GROUPS - N_EXP,), F32)])
        x_mid, t_tiles, logits = out_proj(x2d, y_m, y_r, y_a, w_out[l].astype(BF16), row(norm2_g[l]),
                                          wr_hi, wr_lo, b_router.reshape(R_LOG, 1))
        meta, cnt_rows = route(logits)
        meta = meta.reshape(T // ST, SUBLANES, ST)
        slot_base = (jnp.arange(T // ST, dtype=jnp.int32) % 2) * SLOT_ROWS
        lrow = meta[:, 0:2, :].astype(jnp.int32) * TOK_ROWS + slot_base[:, None, None]
        lrow = lrow.transpose(0, 2, 1).reshape(-1)
        w_pick = meta[:, 2:4, :].transpose(0, 2, 1).reshape(-1)
        cnt, lst, gst, pad, tile_cum, n_used = _routing_tables(cnt_rows)
        tile_ids = jnp.minimum(jnp.arange(n_tiles, dtype=jnp.int32), n_used[0] - 1)
        tile_expert = jnp.minimum(jnp.sum(tile_ids[:, None] >= tile_cum[None, :], axis=1),
                                  N_EXP - 1).astype(jnp.int32)
        xs = dispatch(lrow, cnt, lst, gst, pad, t_tiles, n_tiles * TM_E)
        ys = experts(tile_expert, n_used, xs, w_gate, w_up, w_down, l)
        x2d = combine(lrow, w_pick, cnt, lst, gst, ys, x_mid, row(final_norm_g), l == depth - 1)
    return x2d.reshape(B, S, D_MODEL)
```

```python
import functools

import numpy as np
import jax
import jax.numpy as jnp
from jax import lax
from jax.experimental import pallas as pl
from jax.experimental.pallas import tpu as pltpu

F32 = jnp.float32
BF16 = jnp.bfloat16

D_MODEL = 1024
ML_HEADS = 4
ML_DH = 64
ML_W = ML_HEADS * ML_DH
LRU_W = 256
LRU_BLOCKS = 4
CONV_W = 4
LRU_C = 8.0
MLA_H = 8
NOPE = 64
ROPE = 32
MLA_V = 64
MLA_W = MLA_H * MLA_V
Q_LORA = 256
KV_LORA = 128
ROPE_THETA = 10000.0
N_GROUPS = 4
EPG = 8
N_EXP = N_GROUPS * EPG
D_EXP = 256
EPS = 1e-6

LANES = 128
SUBLANES = 8
V_EXT = MLA_V + SUBLANES
TOK_ROWS = D_MODEL // LANES
R_LOG = 40
PROJ_W = 1536
SMALL_COL = 1408
VMEM_LIMIT = 56 * 1024 * 1024

ML_CHUNK = 128
ML_NB = 4
LRU_TC = 256
TM_IN = 512
TQ = 256
ST = 1024
TM_E = 256
SLOT_ROWS = 2 * ST * TOK_ROWS


def _cp(sem):
    return pltpu.CompilerParams(dimension_semantics=sem, vmem_limit_bytes=VMEM_LIMIT)


def _rms(x, g):
    return x * lax.rsqrt(jnp.mean(x * x, axis=-1, keepdims=True) + EPS) * g


def _in_proj_kernel(x_ref, g_ref, w_ref, wt_ref, o_ref, small_ref, vt_ref, ot_ref):
    y = _rms(x_ref[...], g_ref[...]).astype(BF16)
    acc = jnp.dot(y, w_ref[...], preferred_element_type=F32)
    o_ref[...] = acc[:, :SMALL_COL].astype(BF16)
    small_ref[...] = acc[:, SMALL_COL:]
    vo_t = lax.dot_general(wt_ref[...], y, (((1,), (1,)), ((), ())), preferred_element_type=F32)
    vt_ref[...] = vo_t[:ML_W].astype(BF16)
    ot_ref[...] = vo_t[ML_W:].astype(BF16)


def in_proj(x2d, g, w, wt):
    T = x2d.shape[0]
    return pl.pallas_call(
        _in_proj_kernel,
        grid=(T // TM_IN,),
        in_specs=[pl.BlockSpec((TM_IN, D_MODEL), lambda i: (i, 0)),
                  pl.BlockSpec((1, D_MODEL), lambda i: (0, 0)),
                  pl.BlockSpec((D_MODEL, PROJ_W), lambda i: (0, 0)),
                  pl.BlockSpec((2 * ML_W, D_MODEL), lambda i: (0, 0))],
        out_specs=[pl.BlockSpec((TM_IN, SMALL_COL), lambda i: (i, 0)),
                   pl.BlockSpec((TM_IN, LANES), lambda i: (i, 0)),
                   pl.BlockSpec((ML_W, TM_IN), lambda i: (0, i)),
                   pl.BlockSpec((ML_W, TM_IN), lambda i: (0, i))],
        out_shape=[jax.ShapeDtypeStruct((T, SMALL_COL), BF16),
                   jax.ShapeDtypeStruct((T, LANES), F32),
                   jax.ShapeDtypeStruct((ML_W, T), BF16),
                   jax.ShapeDtypeStruct((ML_W, T), BF16)],
        compiler_params=_cp(("arbitrary",)),
        name="in_proj",
    )(x2d, g, w, wt)


def _mlstm_kernel(q_ref, k_ref, s_ref, bias_ref, g_ref, *rest):
    nb, L, _ = q_ref.shape
    vt_refs, ot_refs = rest[:nb], rest[nb:2 * nb]
    y_ref, c_ref, m_ref = rest[2 * nb:]
    c = pl.program_id(1)

    @pl.when(c == 0)
    def _():
        c_ref[...] = jnp.zeros_like(c_ref)
        m_ref[...] = jnp.zeros_like(m_ref)

    lane = lax.broadcasted_iota(jnp.int32, (L, LANES), 1)
    row_i = lax.broadcasted_iota(jnp.int32, (L, L), 0)
    col_i = lax.broadcasted_iota(jnp.int32, (L, L), 1)
    lower = (col_i <= row_i).astype(F32)
    causal_t = row_i <= col_i
    nt = (((1,), (1,)), ((), ()))
    ones = jnp.ones((SUBLANES, L), F32)
    units = [(bi, h) for bi in range(nb) for h in range(ML_HEADS)]

    gates, cums = [], []
    for bi in range(nb):
        pre = s_ref[bi] + bias_ref[...]
        logsig = jnp.minimum(pre, 0.0) - jnp.log1p(jnp.exp(-jnp.abs(pre)))
        gate = jnp.where(lane < ML_HEADS, pre, logsig)
        gates.append(gate)
        cums.append(jnp.dot(lower, gate, precision=lax.Precision.HIGHEST, preferred_element_type=F32))

    qs, ks, states, scores, qcs = {}, {}, {}, {}, {}
    for bi, h in units:
        sl = slice(h * ML_DH, (h + 1) * ML_DH)
        qs[bi, h] = q_ref[bi, :, sl].astype(BF16)
        ks[bi, h] = (k_ref[bi, :, sl].astype(F32) * (ML_DH ** -0.5)).astype(BF16)
        states[bi, h] = c_ref[bi, h]
        scores[bi, h] = lax.dot_general(ks[bi, h], qs[bi, h], nt, preferred_element_type=F32)
        qcs[bi, h] = lax.dot_general(states[bi, h].astype(BF16), qs[bi, h], nt, preferred_element_type=F32)

    gates_t = [g.T for g in gates]
    cums_t = [cu.T for cu in cums]
    v_t = [vt_refs[bi][...].astype(F32) for bi in range(nb)]
    o_t = [jax.nn.sigmoid(ot_refs[bi][...].astype(F32)) for bi in range(nb)]

    s_t, m_rows, w_inters, nums, d_states, decays = {}, {}, {}, {}, {}, {}
    for bi, h in units:
        sl = slice(h * ML_DH, (h + 1) * ML_DH)
        a_col = gates[bi][:, h:h + 1] - cums[bi][:, ML_HEADS + h:ML_HEADS + h + 1]
        b_row = cums_t[bi][ML_HEADS + h:ML_HEADS + h + 1, :]
        li_row = gates_t[bi][h:h + 1, :]
        m_prev = m_ref[bi, :, h:h + 1]
        dmat = jnp.where(causal_t, a_col + b_row, -jnp.inf)
        inter = b_row + m_prev
        m_row = jnp.maximum(jnp.max(dmat, axis=0, keepdims=True), inter)
        w_inters[bi, h] = jnp.exp(inter - m_row)
        m_rows[bi, h] = m_row
        s_t[bi, h] = scores[bi, h] * jnp.exp(dmat - m_row)
        b_last = b_row[:, L - 1:L]
        g_row = b_last - b_row + li_row
        m_new = jnp.maximum(b_last + m_prev, jnp.max(g_row, axis=1, keepdims=True))
        decays[bi, h] = jnp.exp(b_last + m_prev - m_new)
        m_ref[bi, :, h:h + 1] = m_new
        vt_ext = jnp.concatenate([v_t[bi][sl, :], ones], axis=0) * jnp.exp(g_row - m_new)
        nums[bi, h] = jnp.dot(v_t[bi][sl, :].astype(BF16), s_t[bi, h].astype(BF16), preferred_element_type=F32)
        d_states[bi, h] = jnp.dot(vt_ext.astype(BF16), ks[bi, h], preferred_element_type=F32)

    for bi in range(nb):
        ys = []
        for h in range(ML_HEADS):
            sl = slice(h * ML_DH, (h + 1) * ML_DH)
            qc, w_inter = qcs[bi, h], w_inters[bi, h]
            den = jnp.sum(s_t[bi, h], axis=0, keepdims=True) + w_inter * qc[ML_DH:ML_DH + 1, :]
            hh = (nums[bi, h] + w_inter * qc[:ML_DH, :]) / jnp.maximum(jnp.abs(den), jnp.exp(-m_rows[bi, h]))
            y = o_t[bi][sl, :] * hh
            ys.append(y * lax.rsqrt(jnp.mean(y * y, axis=0, keepdims=True) + EPS) * g_ref[sl, :])
            c_ref[bi, h] = decays[bi, h] * states[bi, h] + d_states[bi, h]
        y_ref[bi] = jnp.concatenate(ys, axis=0).T.astype(BF16)


def mlstm(proj3, small3, v_t, o_t, bias, g):
    B, S, _ = proj3.shape
    L = ML_CHUNK
    nb = ML_NB
    nc = S // L
    col = lambda j: pl.BlockSpec((nb, L, ML_W), lambda b, c, j=j: (b, c, j))
    tcol = [pl.BlockSpec((ML_W, L), lambda b, c, bi=bi: (0, (b * nb + bi) * nc + c)) for bi in range(nb)]
    return pl.pallas_call(
        _mlstm_kernel,
        grid=(B // nb, S // L),
        in_specs=[col(0), col(1),
                  pl.BlockSpec((nb, L, LANES), lambda b, c: (b, c, 0)),
                  pl.BlockSpec((1, LANES), lambda b, c: (0, 0)),
                  pl.BlockSpec((ML_W, 1), lambda b, c: (0, 0))] + tcol + tcol,
        out_specs=pl.BlockSpec((nb, L, ML_W), lambda b, c: (b, c, 0)),
        out_shape=jax.ShapeDtypeStruct((B, S, ML_W), BF16),
        scratch_shapes=[pltpu.VMEM((nb, ML_HEADS, ML_DH + SUBLANES, ML_DH), F32),
                        pltpu.VMEM((nb, 1, LANES), F32)],
        compiler_params=_cp(("arbitrary", "arbitrary")),
        name="mlstm",
    )(proj3, proj3, small3, bias, g, *([v_t] * nb), *([o_t] * nb))


def _rglru_kernel(x_ref, gate_ref, cw_ref, cb_ref, wg_ref, bg_ref, lam_ref, g_ref, y_ref,
                  xp_ref, a_ref, u_ref, h_ref, carry_ref):
    B, tc, W = x_ref.shape
    c = pl.program_id(0)

    @pl.when(c == 0)
    def _():
        xp_ref[:, 0:SUBLANES, :] = jnp.zeros((B, SUBLANES, W), F32)
        carry_ref[...] = jnp.zeros_like(carry_ref)

    xp_ref[:, SUBLANES:, :] = x_ref[...].astype(F32)
    xc = jnp.zeros((B, tc, W), F32) + cb_ref[...]
    for j in range(CONV_W):
        off = SUBLANES - (CONV_W - 1) + j
        xc = xc + cw_ref[j:j + 1, :] * xp_ref[:, off:off + tc, :]
    xp_ref[:, 0:SUBLANES, :] = x_ref[:, tc - SUBLANES:, :].astype(F32)

    xc2 = xc.reshape(B * tc, W)
    ri = jax.nn.sigmoid(jnp.dot(xc2.astype(BF16), wg_ref[...], preferred_element_type=F32) + bg_ref[...])
    r = ri[:, :W]
    ig = ri[:, W:]
    lam = lam_ref[...]
    softplus_neg = jnp.maximum(-lam, 0.0) + jnp.log1p(jnp.exp(-jnp.abs(lam)))
    log_a = -LRU_C * r * softplus_neg
    a = jnp.exp(log_a)
    u = jnp.sqrt(1.0 - jnp.exp(2.0 * log_a)) * (ig * xc2)
    halves = W // LANES
    for k in range(halves):
        a_ref[k] = a[:, k * LANES:(k + 1) * LANES]
        u_ref[k] = u[:, k * LANES:(k + 1) * LANES]

    def step(i, hs):
        base = pl.multiple_of(i * SUBLANES, SUBLANES)
        for j in range(SUBLANES):
            rows = pl.ds(base + j, B, stride=tc)
            hs = tuple(a_ref.at[k][rows, :] * hs[k] + u_ref.at[k][rows, :] for k in range(halves))
            for k in range(halves):
                h_ref.at[k][rows, :] = hs[k]
        return hs

    init = tuple(carry_ref[k] for k in range(halves))
    last = lax.fori_loop(0, tc // SUBLANES, step, init)
    for k in range(halves):
        carry_ref[k] = last[k]
    h_all = jnp.concatenate([h_ref[k] for k in range(halves)], axis=1)
    y = h_all.reshape(B, tc, W) * jax.nn.gelu(gate_ref[...].astype(F32))
    y_ref[...] = _rms(y, g_ref[...]).astype(BF16)


def rglru(proj3, cw, cb, wg, bg, lam, g):
    B, S, _ = proj3.shape
    tc = LRU_TC
    vec = lambda n: pl.BlockSpec((1, n), lambda c: (0, 0))
    return pl.pallas_call(
        _rglru_kernel,
        grid=(S // tc,),
        in_specs=[pl.BlockSpec((B, tc, LRU_W), lambda c: (0, c, 2)),
                  pl.BlockSpec((B, tc, LRU_W), lambda c: (0, c, 3)),
                  pl.BlockSpec((CONV_W, LRU_W), lambda c: (0, 0)),
                  vec(LRU_W),
                  pl.BlockSpec((LRU_W, 2 * LRU_W), lambda c: (0, 0)),
                  vec(2 * LRU_W), vec(LRU_W), vec(LRU_W)],
        out_specs=pl.BlockSpec((B, tc, LRU_W), lambda c: (0, c, 0)),
        out_shape=jax.ShapeDtypeStruct((B, S, LRU_W), BF16),
        scratch_shapes=[pltpu.VMEM((B, tc + SUBLANES, LRU_W), F32),
                        pltpu.VMEM((LRU_W // LANES, B * tc, LANES), F32),
                        pltpu.VMEM((LRU_W // LANES, B * tc, LANES), F32),
                        pltpu.VMEM((LRU_W // LANES, B * tc, LANES), F32),
                        pltpu.VMEM((LRU_W // LANES, B, LANES), F32)],
        compiler_params=_cp(("arbitrary",)),
        name="rglru",
    )(proj3, proj3, cw, cb, wg, bg, lam, g)


def _rot_half(blk):
    lane = lax.broadcasted_iota(jnp.int32, blk.shape, 1)
    return jnp.where(lane < NOPE + ROPE // 2,
                     pltpu.roll(blk, LANES - ROPE // 2, 1), pltpu.roll(blk, ROPE // 2, 1))


def _mla_proj_kernel(cq_ref, ckv_ref, small_ref, qg_ref, kvg_ref, wq_ref, wqs_ref, wk_ref, wvt_ref, vones_ref,
                     cq_t, sq_t, ck_t, sk_t, q_ref, k_ref, vt_ref):
    qn = _rms(cq_ref[...].astype(F32), qg_ref[...]).astype(BF16)
    qf = jnp.dot(qn, wq_ref[...], preferred_element_type=F32)
    qs = jnp.dot(qn, wqs_ref[...], preferred_element_type=F32)
    kvn = _rms(ckv_ref[...].astype(F32), kvg_ref[...]).astype(BF16)
    kf = jnp.dot(kvn, wk_ref[...], preferred_element_type=F32)
    vt_ref[...] = (lax.dot_general(wvt_ref[...], kvn, (((1,), (1,)), ((), ())), preferred_element_type=F32)
                   + vones_ref[...]).astype(BF16)
    small = small_ref[...]
    kr = small * ck_t[...] + _rot_half(small) * sk_t[...]
    cq = cq_t[...]
    sq = sq_t[...]
    for h in range(MLA_H):
        sl = slice(h * LANES, (h + 1) * LANES)
        q_ref[:, sl] = (qf[:, sl] * cq + qs[:, sl] * sq).astype(BF16)
        k_ref[:, sl] = (kf[:, sl] + kr).astype(BF16)


def mla_proj(proj, small, qg, kvg, wq, wqs, wk, wvt, vones, tabs, S):
    T = proj.shape[0]
    tm = TM_IN
    npos = S // tm
    tab = pl.BlockSpec((tm, LANES), lambda i: (i % npos, 0))
    full = lambda a: pl.BlockSpec(a.shape, lambda i: (0, 0))
    return pl.pallas_call(
        _mla_proj_kernel,
        grid=(T // tm,),
        in_specs=[pl.BlockSpec((tm, Q_LORA), lambda i: (i, 4)),
                  pl.BlockSpec((tm, KV_LORA), lambda i: (i, (SMALL_COL - KV_LORA) // KV_LORA)),
                  pl.BlockSpec((tm, LANES), lambda i: (i, 0)),
                  full(qg), full(kvg), full(wq), full(wqs), full(wk), full(wvt), full(vones),
                  tab, tab, tab, tab],
        out_specs=[pl.BlockSpec((tm, MLA_H * LANES), lambda i: (i, 0)),
                   pl.BlockSpec((tm, MLA_H * LANES), lambda i: (i, 0)),
                   pl.BlockSpec((MLA_H * V_EXT, tm), lambda i: (0, i))],
        out_shape=[jax.ShapeDtypeStruct((T, MLA_H * LANES), BF16),
                   jax.ShapeDtypeStruct((T, MLA_H * LANES), BF16),
                   jax.ShapeDtypeStruct((MLA_H * V_EXT, T), BF16)],
        compiler_params=_cp(("arbitrary",)),
        name="mla_proj",
    )(proj, proj, small, qg, kvg, wq, wqs, wk, wvt, vones, *tabs)


def _attn_kernel(q_ref, k_ref, vt_ref, g_ref, o_ref, acc_ref, m_ref, yt_ref):
    tq = q_ref.shape[0]
    i = pl.program_id(1)
    key_i = lax.broadcasted_iota(jnp.int32, (tq, tq), 0)
    qry_i = lax.broadcasted_iota(jnp.int32, (tq, tq), 1)
    causal = key_i <= qry_i
    nt = (((1,), (1,)), ((), ()))
    m_ref[...] = jnp.full_like(m_ref, -jnp.inf)
    acc_ref[...] = jnp.zeros_like(acc_ref)

    ahead = 4

    def scores(off, h):
        hsl = slice(h * LANES, (h + 1) * LANES)
        return lax.dot_general(k_ref[pl.ds(off, tq), hsl], q_ref[:, hsl], nt, preferred_element_type=F32)

    def update(h, alpha, pv):
        rows = slice(h * V_EXT, (h + 1) * V_EXT)
        acc_ref[rows, :] = alpha * acc_ref[rows, :] + pv

    def blocks(items):
        work = [(pl.multiple_of(j * tq, tq), h, masked) for j, masked in items for h in range(MLA_H)]
        pending = [scores(off, h) for off, h, _ in work[:ahead]]
        late = None
        for n, (off, h, masked) in enumerate(work):
            s = pending.pop(0)
            if n + ahead < len(work):
                pending.append(scores(*work[n + ahead][:2]))
            if masked:
                s = jnp.where(causal, s, -jnp.inf)
            m = m_ref[h:h + 1, :]
            m_new = jnp.maximum(m, jnp.max(s, axis=0, keepdims=True))
            m_ref[h:h + 1, :] = m_new
            alpha = jnp.exp2(m - m_new)
            p = jnp.exp2(s - m_new).astype(BF16)
            pv = jnp.dot(vt_ref[h * V_EXT:(h + 1) * V_EXT, pl.ds(off, tq)], p, preferred_element_type=F32)
            if late is not None:
                update(*late)
            late = (h, alpha, pv)
        update(*late)

    def body(pair, carry):
        blocks([(2 * pair, False), (2 * pair + 1, False)])
        return carry

    lax.fori_loop(0, i // 2, body, 0)

    @pl.when(i % 2 == 1)
    def _():
        blocks([(i - 1, False), (i, True)])

    @pl.when(i % 2 == 0)
    def _():
        blocks([(i, True)])

    ssq = jnp.zeros((1, tq), F32)
    for h in range(MLA_H):
        vrows = slice(h * MLA_V, (h + 1) * MLA_V)
        yh = acc_ref[h * V_EXT:h * V_EXT + MLA_V, :] * (1.0 / acc_ref[h * V_EXT + MLA_V:h * V_EXT + MLA_V + 1, :])
        yt_ref[vrows, :] = yh
        ssq = ssq + jnp.sum(yh * yh, axis=0, keepdims=True)
    yt = yt_ref[...] * lax.rsqrt(ssq * (1.0 / MLA_W) + EPS) * g_ref[...]
    o_ref[...] = yt.T.astype(BF16)


def attention(q, k, vt, g, B, S):
    T = B * S
    nq = S // TQ
    return pl.pallas_call(
        _attn_kernel,
        grid=(B, nq),
        in_specs=[pl.BlockSpec((TQ, MLA_H * LANES), lambda b, i: (b * nq + i, 0)),
                  pl.BlockSpec((S, MLA_H * LANES), lambda b, i: (b, 0)),
                  pl.BlockSpec((MLA_H * V_EXT, S), lambda b, i: (0, b)),
                  pl.BlockSpec((MLA_W, 1), lambda b, i: (0, 0))],
        out_specs=pl.BlockSpec((TQ, MLA_W), lambda b, i: (b * nq + i, 0)),
        out_shape=jax.ShapeDtypeStruct((T, MLA_W), BF16),
        scratch_shapes=[pltpu.VMEM((MLA_H * V_EXT, TQ), F32),
                        pltpu.VMEM((MLA_H, TQ), F32),
                        pltpu.VMEM((MLA_W, TQ), F32)],
        compiler_params=_cp(("arbitrary", "arbitrary")),
        name="attention",
    )(q, k, vt, g)


def _out_proj_kernel(x_ref, ym_ref, yr_ref, ya_ref, w_ref, g_ref, wrh_ref, wrl_ref, br_ref,
                     xm_ref, t_ref, lg_ref):
    acc = jnp.dot(ym_ref[0].astype(BF16), w_ref[0:ML_W, :], preferred_element_type=F32)
    acc += jnp.dot(yr_ref[0].astype(BF16), w_ref[ML_W:ML_W + LRU_W, :], preferred_element_type=F32)
    acc += jnp.dot(ya_ref[...].astype(BF16), w_ref[ML_W + LRU_W:, :], preferred_element_type=F32)
    xm = x_ref[...] + acc
    xm_ref[...] = xm
    t = _rms(xm, g_ref[...])
    t_hi = t.astype(BF16)
    t_lo = (t - t_hi.astype(F32)).astype(BF16)
    nt = (((1,), (1,)), ((), ()))
    lg = lax.dot_general(wrh_ref[...], t_hi, nt, preferred_element_type=F32)
    lg += lax.dot_general(wrl_ref[...], t_hi, nt, preferred_element_type=F32)
    lg += lax.dot_general(wrh_ref[...], t_lo, nt, preferred_element_type=F32)
    lg_ref[...] = lg + br_ref[...]
    for s in range(TOK_ROWS):
        t_ref[pl.ds(s, t.shape[0], stride=TOK_ROWS), :] = t[:, s * LANES:(s + 1) * LANES]


def out_proj(x2d, ym3, yr3, ya, w, g, wr_hi, wr_lo, br):
    T = x2d.shape[0]
    tm = TM_IN
    per_seq = ym3.shape[1] // tm
    row = lambda n: pl.BlockSpec((tm, n), lambda i: (i, 0))
    seq = lambda n: pl.BlockSpec((1, tm, n), lambda i: (i // per_seq, i % per_seq, 0))
    full = lambda a: pl.BlockSpec(a.shape, lambda i: (0, 0))
    return pl.pallas_call(
        _out_proj_kernel,
        grid=(T // tm,),
        in_specs=[row(D_MODEL), seq(ML_W), seq(LRU_W), row(MLA_W), full(w), full(g),
                  full(wr_hi), full(wr_lo), full(br)],
        out_specs=[row(D_MODEL),
                   pl.BlockSpec((tm * TOK_ROWS, LANES), lambda i: (i, 0)),
                   pl.BlockSpec((R_LOG, tm), lambda i: (0, i))],
        out_shape=[jax.ShapeDtypeStruct((T, D_MODEL), F32),
                   jax.ShapeDtypeStruct((T * TOK_ROWS, LANES), F32),
                   jax.ShapeDtypeStruct((R_LOG, T), F32)],
        compiler_params=_cp(("arbitrary",)),
        name="out_proj",
    )(x2d, ym3, yr3, ya, w, g, wr_hi, wr_lo, br)


def _route_kernel(lg_ref, meta_ref, cnt_ref):
    st = lg_ref.shape[1]
    lg = lg_ref[...]
    row = lax.broadcasted_iota(jnp.int32, (R_LOG, st), 0)
    col_max = lambda a: jnp.max(a, axis=0, keepdims=True)
    col_sum = lambda a: jnp.sum(a, axis=0, keepdims=True)
    first = lambda cond: jnp.min(jnp.where(cond, row, R_LOG), axis=0, keepdims=True)

    gmask = row < N_GROUPS
    gl = jnp.where(gmask, lg, -jnp.inf)
    ge = jnp.exp(gl - col_max(gl))
    gp = ge / col_sum(ge)
    g_val = col_max(gp)
    g_idx = first(gmask & (gp == g_val))

    smask = (row >= N_GROUPS) & (row < N_GROUPS + N_EXP) & (((row - N_GROUPS) >> 3) == g_idx)
    el = jnp.where(smask, lg, -jnp.inf)
    ee = jnp.exp(el - col_max(el))
    ep = ee / col_sum(ee)
    v1 = col_max(ep)
    i1 = first(smask & (ep == v1))
    rest = smask & (row != i1)
    v2 = col_max(jnp.where(rest, ep, -1.0))
    i2 = first(rest & (ep == v2))
    den = v1 + v2
    w1 = v1 / den * g_val
    w2 = v2 / den * g_val

    onehot = ((row == i1) | (row == i2)).astype(F32)
    sb = min(st, 256)
    r_i = lax.broadcasted_iota(jnp.int32, (sb, sb), 0)
    c_i = lax.broadcasted_iota(jnp.int32, (sb, sb), 1)
    before = (r_i < c_i).astype(BF16)
    run = jnp.zeros((R_LOG, 1), F32)
    ranks = []
    for j in range(st // sb):
        oh = onehot[:, j * sb:(j + 1) * sb]
        ranks.append(jnp.dot(oh.astype(BF16), before, preferred_element_type=F32) + run)
        run = run + jnp.sum(oh, axis=1, keepdims=True)
    rank = jnp.concatenate(ranks, axis=1)
    e_r = lax.broadcasted_iota(jnp.int32, (R_LOG, R_LOG), 0)
    e_c = lax.broadcasted_iota(jnp.int32, (R_LOG, R_LOG), 1)
    counts = jnp.broadcast_to(run, (R_LOG, LANES))
    lstart = jnp.dot((e_c < e_r).astype(F32), counts, precision=lax.Precision.HIGHEST,
                     preferred_element_type=F32)
    slot = rank + lstart[:, 0:1]
    p1 = col_sum(jnp.where(row == i1, slot, 0.0))
    p2 = col_sum(jnp.where(row == i2, slot, 0.0))
    meta_ref[...] = jnp.concatenate([p1, p2, w1, w2, jnp.zeros((SUBLANES - 4, st), F32)], axis=0)
    lane = lax.broadcasted_iota(jnp.int32, (R_LOG, LANES), 1)
    cnt_ref[...] = jnp.where(lane == 0, counts, jnp.where(lane == 1, lstart, 0.0))


def route(logits_t):
    T = logits_t.shape[1]
    n_st = T // ST
    return pl.pallas_call(
        _route_kernel,
        grid=(n_st,),
        in_specs=[pl.BlockSpec((R_LOG, ST), lambda i: (0, i))],
        out_specs=[pl.BlockSpec((SUBLANES, ST), lambda i: (i, 0)),
                   pl.BlockSpec((R_LOG, LANES), lambda i: (i, 0))],
        out_shape=[jax.ShapeDtypeStruct((n_st * SUBLANES, ST), F32),
                   jax.ShapeDtypeStruct((n_st * R_LOG, LANES), F32)],
        compiler_params=_cp(("arbitrary",)),
        name="route",
    )(logits_t)


def _copy_by_bits(n, nbits, copy_at, start):
    off = jnp.int32(0)
    for bit in reversed(range(nbits)):
        sz = 1 << bit

        @pl.when((n & sz) != 0)
        def _(off=off, sz=sz):
            cp = copy_at(off, sz)
            if start:
                cp.start()
            else:
                cp.wait()

        off = off + (n & sz)


def _strip_copies(st, cnt_ref, lst_ref, gst_ref, make_copy, start):
    def body(e, carry):
        loc = lst_ref[st * N_EXP + e]
        glob = gst_ref[st * N_EXP + e]
        _copy_by_bits(
            cnt_ref[st * N_EXP + e], int(np.log2(ST)) + 1,
            lambda off, sz: make_copy(pl.multiple_of((loc + off) * TOK_ROWS, TOK_ROWS),
                                      pl.multiple_of((glob + off) * TOK_ROWS, TOK_ROWS), sz * TOK_ROWS),
            start)
        return carry

    lax.fori_loop(0, N_EXP, body, 0)


def _zero_padding(pad_ref, xs_ref, zero_ref, sem, start):
    tile_rows = TM_E * TOK_ROWS

    def body(e, carry):
        first = pad_ref[e]
        _copy_by_bits(
            pad_ref[N_EXP + e], int(np.log2(TM_E)),
            lambda off, sz: pltpu.make_async_copy(
                zero_ref.at[pl.ds(0, sz * TOK_ROWS)],
                xs_ref.at[pl.ds(pl.multiple_of((first + off) * TOK_ROWS, TOK_ROWS), sz * TOK_ROWS)], sem),
            start)
        return carry

    lax.fori_loop(0, N_EXP, body, 0)

    def tail(i, carry):
        cp = pltpu.make_async_copy(
            zero_ref, xs_ref.at[pl.ds(pl.multiple_of(i * tile_rows, tile_rows), tile_rows)], sem)
        if start:
            cp.start()
        else:
            cp.wait()
        return carry

    lax.fori_loop(pad_ref[2 * N_EXP], xs_ref.shape[0] // tile_rows, tail, 0)


def _dispatch_kernel(lrow_ref, cnt_ref, lst_ref, gst_ref, pad_ref, t_ref, xs_ref,
                     stage_ref, zero_ref, sem, zsem):
    st = pl.program_id(0)
    last = pl.num_programs(0) - 1
    slot = st % 2

    @pl.when(st == 0)
    def _():
        zero_ref[...] = jnp.zeros_like(zero_ref)
        _zero_padding(pad_ref, xs_ref, zero_ref, zsem, True)

    def scatter(i, carry):
        pair0 = (st * ST + i * SUBLANES) * 2
        for u in range(SUBLANES):
            t = i * SUBLANES + u
            tile = t_ref[pl.ds(pl.multiple_of(t * TOK_ROWS, TOK_ROWS), TOK_ROWS), :]
            for k in range(2):
                row = pl.multiple_of(lrow_ref[pair0 + 2 * u + k], TOK_ROWS)
                stage_ref[pl.ds(row, TOK_ROWS), :] = tile
        return carry

    lax.fori_loop(0, ST // SUBLANES, scatter, 0)

    def slot_rows(buf, loc, rows):
        return stage_ref.at[pl.ds(pl.multiple_of(buf * SLOT_ROWS + loc, TOK_ROWS), rows)]

    def copies(step, start):
        buf = step % 2
        _strip_copies(
            step, cnt_ref, lst_ref, gst_ref,
            lambda loc, glob, rows: pltpu.make_async_copy(
                slot_rows(buf, loc, rows), xs_ref.at[pl.ds(glob, rows)], sem.at[buf]),
            start)

    def wait_all(buf):
        pltpu.make_async_copy(slot_rows(buf, 0, SLOT_ROWS), xs_ref.at[pl.ds(0, SLOT_ROWS)], sem.at[buf]).wait()

    copies(st, True)

    @pl.when(st > 0)
    def _():
        wait_all(1 - slot)

    @pl.when(st == last)
    def _():
        wait_all(slot)
        _zero_padding(pad_ref, xs_ref, zero_ref, zsem, False)


def dispatch(lrow, cnt, lst, gst, pad, t_tiles, n_rows):
    T = t_tiles.shape[0] // TOK_ROWS
    return pl.pallas_call(
        _dispatch_kernel,
        grid_spec=pltpu.PrefetchScalarGridSpec(
            num_scalar_prefetch=5,
            grid=(T // ST,),
            in_specs=[pl.BlockSpec((ST * TOK_ROWS, LANES), lambda i, *_: (i, 0))],
            out_specs=pl.BlockSpec(memory_space=pl.ANY),
            scratch_shapes=[pltpu.VMEM((2 * SLOT_ROWS, LANES), F32),
                            pltpu.VMEM((TM_E * TOK_ROWS, LANES), F32),
                            pltpu.SemaphoreType.DMA((2,)),
                            pltpu.SemaphoreType.DMA(())]),
        out_shape=jax.ShapeDtypeStruct((n_rows * TOK_ROWS, LANES), F32),
        compiler_params=_cp(("arbitrary",)),
        name="dispatch",
    )(lrow, cnt, lst, gst, pad, t_tiles)


def _combine_kernel(lrow_ref, w_ref, cnt_ref, lst_ref, gst_ref, ys_ref, xm_ref, fg_ref, o_ref,
                    stage_ref, comb_ref, sem, *, final):
    st = pl.program_id(0)
    last = pl.num_programs(0) - 1
    slot = st % 2

    def slot_rows(buf, loc, rows):
        return stage_ref.at[pl.ds(pl.multiple_of(buf * SLOT_ROWS + loc, TOK_ROWS), rows)]

    def copies(step, start):
        buf = step % 2
        _strip_copies(
            step, cnt_ref, lst_ref, gst_ref,
            lambda loc, glob, rows: pltpu.make_async_copy(
                ys_ref.at[pl.ds(glob, rows)], slot_rows(buf, loc, rows), sem.at[buf]),
            start)

    @pl.when(st == 0)
    def _():
        copies(st, True)

    @pl.when(st < last)
    def _():
        copies(st + 1, True)

    pltpu.make_async_copy(ys_ref.at[pl.ds(0, SLOT_ROWS)], slot_rows(slot, 0, SLOT_ROWS), sem.at[slot]).wait()

    def gather(i, carry):
        pair0 = (st * ST + i * SUBLANES) * 2
        for u in range(SUBLANES):
            t = i * SUBLANES + u
            acc = None
            for k in range(2):
                row = pl.multiple_of(lrow_ref[pair0 + 2 * u + k], TOK_ROWS)
                term = w_ref[pair0 + 2 * u + k] * stage_ref[pl.ds(row, TOK_ROWS), :]
                acc = term if acc is None else acc + term
            comb_ref[pl.ds(pl.multiple_of(t * TOK_ROWS, TOK_ROWS), TOK_ROWS), :] = acc
        return carry

    lax.fori_loop(0, ST // SUBLANES, gather, 0)
    moe = jnp.concatenate([comb_ref[pl.ds(s, ST, stride=TOK_ROWS), :] for s in range(TOK_ROWS)], axis=1)
    x_new = xm_ref[...] + moe
    o_ref[...] = _rms(x_new, fg_ref[...]) if final else x_new


def combine(lrow, w, cnt, lst, gst, ys, xm, fg, final):
    T = xm.shape[0]
    return pl.pallas_call(
        functools.partial(_combine_kernel, final=final),
        grid_spec=pltpu.PrefetchScalarGridSpec(
            num_scalar_prefetch=5,
            grid=(T // ST,),
            in_specs=[pl.BlockSpec(memory_space=pl.ANY),
                      pl.BlockSpec((ST, D_MODEL), lambda i, *_: (i, 0)),
                      pl.BlockSpec((1, D_MODEL), lambda i, *_: (0, 0))],
            out_specs=pl.BlockSpec((ST, D_MODEL), lambda i, *_: (i, 0)),
            scratch_shapes=[pltpu.VMEM((2 * SLOT_ROWS, LANES), F32),
                            pltpu.VMEM((ST * TOK_ROWS, LANES), F32),
                            pltpu.SemaphoreType.DMA((2,))]),
        out_shape=jax.ShapeDtypeStruct((T, D_MODEL), F32),
        compiler_params=_cp(("arbitrary",)),
        name="combine",
    )(lrow, w, cnt, lst, gst, ys, xm, fg)


def _expert_kernel(te_ref, nused_ref, xs_ref, wg_ref, wu_ref, wd_ref, ys_ref):
    i = pl.program_id(0)
    tm = xs_ref.shape[0] // TOK_ROWS

    @pl.when(i < nused_ref[0])
    def _():
        x = jnp.concatenate([xs_ref[pl.ds(s, tm, stride=TOK_ROWS), :] for s in range(TOK_ROWS)],
                            axis=1).astype(BF16)
        g = jnp.dot(x, wg_ref[0, 0].astype(BF16), preferred_element_type=F32)
        u = jnp.dot(x, wu_ref[0, 0].astype(BF16), preferred_element_type=F32)
        h = (g * jax.nn.sigmoid(g) * u).astype(BF16)
        y = jnp.dot(h, wd_ref[0, 0].astype(BF16), preferred_element_type=F32)
        for s in range(TOK_ROWS):
            ys_ref[pl.ds(s, tm, stride=TOK_ROWS), :] = y[:, s * LANES:(s + 1) * LANES]

    @pl.when(i >= nused_ref[0])
    def _():
        ys_ref[...] = jnp.zeros_like(ys_ref)


def experts(tile_expert, n_used, xs, wg, wu, wd, layer):
    n_tiles = xs.shape[0] // (TM_E * TOK_ROWS)
    tile = lambda i, te, nu: (jnp.maximum(jnp.minimum(i, nu[0] - 1), 0), 0)
    return pl.pallas_call(
        _expert_kernel,
        grid_spec=pltpu.PrefetchScalarGridSpec(
            num_scalar_prefetch=2,
            grid=(n_tiles,),
            in_specs=[pl.BlockSpec((TM_E * TOK_ROWS, LANES), tile),
                      pl.BlockSpec((1, 1, D_MODEL, D_EXP), lambda i, te, nu: (layer, te[i], 0, 0)),
                      pl.BlockSpec((1, 1, D_MODEL, D_EXP), lambda i, te, nu: (layer, te[i], 0, 0)),
                      pl.BlockSpec((1, 1, D_EXP, D_MODEL), lambda i, te, nu: (layer, te[i], 0, 0))],
            out_specs=pl.BlockSpec((TM_E * TOK_ROWS, LANES), lambda i, te, nu: (i, 0))),
        out_shape=jax.ShapeDtypeStruct(xs.shape, F32),
        compiler_params=_cp(("arbitrary",)),
        name="experts",
    )(tile_expert, n_used, xs, wg, wu, wd)


def _rope_tables(S):
    inv = 1.0 / (ROPE_THETA ** (jnp.arange(0, ROPE, 2, dtype=F32) / ROPE))
    ang = jnp.arange(S, dtype=F32)[:, None] * inv[None, :]
    cos, sin = jnp.cos(ang), jnp.sin(ang)
    z = lambda n: jnp.zeros((S, n), F32)
    scale = (NOPE + ROPE) ** -0.5 * np.log2(np.e)
    pad = LANES - NOPE - ROPE
    cq = scale * jnp.concatenate([jnp.ones((S, NOPE), F32), cos, cos, z(pad)], axis=1)
    sq = scale * jnp.concatenate([z(NOPE), -sin, sin, z(pad)], axis=1)
    ck = jnp.concatenate([z(NOPE), cos, cos, z(pad)], axis=1)
    sk = jnp.concatenate([z(NOPE), -sin, sin, z(pad)], axis=1)
    return cq, sq, ck, sk


def _block_diag(w):
    n, d, _ = w.shape
    return (jnp.eye(n, dtype=w.dtype)[:, None, :, None] * w[:, :, None, :]).reshape(n * d, n * d)


def _in_weights(w_in):
    z = lambda n: jnp.zeros((D_MODEL, n), F32)
    o = 4 * ML_W
    mi, mf = w_in[:, o:o + ML_HEADS], w_in[:, o + ML_HEADS:o + 2 * ML_HEADS]
    o += 2 * ML_HEADS
    rest = w_in[:, o:o + 2 * LRU_W + Q_LORA + KV_LORA]
    kr = w_in[:, o + 2 * LRU_W + Q_LORA + KV_LORA:]
    small = jnp.concatenate([mi, mf, z(NOPE - 2 * ML_HEADS), kr, z(LANES - NOPE - ROPE)], axis=1)
    w_main = jnp.concatenate([w_in[:, :2 * ML_W], rest, small], axis=1).astype(BF16)
    return w_main, w_in[:, 2 * ML_W:4 * ML_W].T.astype(BF16)


def _routing_tables(cnt_rows):
    n_st = cnt_rows.shape[0] // R_LOG
    rows = cnt_rows.reshape(n_st, R_LOG, LANES)
    cnt = rows[:, N_GROUPS:N_GROUPS + N_EXP, 0].astype(jnp.int32)
    lst = rows[:, N_GROUPS:N_GROUPS + N_EXP, 1].astype(jnp.int32)
    tiles_e = (jnp.sum(cnt, axis=0) + TM_E - 1) // TM_E
    tile_cum = jnp.cumsum(tiles_e)
    base = (tile_cum - tiles_e) * TM_E
    gst = base[None, :] + jnp.cumsum(cnt, axis=0) - cnt
    n_used = tile_cum[-1:].astype(jnp.int32)
    n_e = jnp.sum(cnt, axis=0)
    pad = jnp.concatenate([base + n_e, tiles_e * TM_E - n_e, n_used]).astype(jnp.int32)
    return cnt.reshape(-1), lst.reshape(-1), gst.reshape(-1), pad, tile_cum, n_used


def kernel(x, norm1_g, w_in, ml_i_bias, ml_f_bias, ml_norm_g, conv_w, conv_b, lru_w_r, lru_b_r, lru_w_i, lru_b_i, lru_lambda, lru_norm_g, q_norm_g, w_uq, kv_norm_g, w_ukv, mla_norm_g, w_out, norm2_g, w_group, b_group, w_expert, b_expert, w_gate, w_up, w_down, final_norm_g):
    B, S, _ = x.shape
    T = B * S
    depth = w_in.shape[0]
    n_tiles = (2 * T) // TM_E + N_EXP
    tabs = _rope_tables(S)
    vones = jnp.tile(jnp.concatenate([jnp.zeros((MLA_V,), F32), jnp.ones((V_EXT - MLA_V,), F32)]),
                     MLA_H).reshape(MLA_H * V_EXT, 1)
    row = lambda a: a.reshape(1, -1)
    x2d = x.reshape(T, D_MODEL)
    for l in range(depth):
        proj, small, v_t, o_t = in_proj(x2d, row(norm1_g[l]), *_in_weights(w_in[l]))
        gate_bias = jnp.concatenate([ml_i_bias[l], ml_f_bias[l], jnp.zeros((LANES - 2 * ML_HEADS,), F32)])
        proj3 = proj.reshape(B, S, SMALL_COL)
        y_m = mlstm(proj3, small.reshape(B, S, LANES), v_t, o_t, row(gate_bias),
                    ml_norm_g[l].reshape(ML_W, 1))
        w_gates = jnp.concatenate([_block_diag(lru_w_r[l]), _block_diag(lru_w_i[l])], axis=1).astype(BF16)
        b_gates = jnp.concatenate([lru_b_r[l], lru_b_i[l]])
        y_r = rglru(proj3, conv_w[l], row(conv_b[l]), w_gates, row(b_gates),
                    row(lru_lambda[l]), row(lru_norm_g[l]))
        wq3 = w_uq[l].reshape(Q_LORA, MLA_H, NOPE + ROPE)
        wq = jnp.pad(wq3, ((0, 0), (0, 0), (0, LANES - NOPE - ROPE))).reshape(Q_LORA, MLA_H * LANES).astype(BF16)
        wqs = jnp.concatenate([jnp.zeros((Q_LORA, MLA_H, NOPE), F32), wq3[:, :, NOPE + ROPE // 2:],
                               wq3[:, :, NOPE:NOPE + ROPE // 2],
                               jnp.zeros((Q_LORA, MLA_H, LANES - NOPE - ROPE), F32)],
                              axis=2).reshape(Q_LORA, MLA_H * LANES).astype(BF16)
        wkv = w_ukv[l].reshape(KV_LORA, MLA_H, NOPE + MLA_V)
        wk = jnp.pad(wkv[:, :, :NOPE], ((0, 0), (0, 0), (0, LANES - NOPE))).reshape(KV_LORA, MLA_H * LANES).astype(BF16)
        wvt = jnp.pad(wkv[:, :, NOPE:], ((0, 0), (0, 0), (0, V_EXT - MLA_V))).reshape(KV_LORA, MLA_H * V_EXT).T.astype(BF16)
        q, k, vt = mla_proj(proj, small, row(q_norm_g[l]), row(kv_norm_g[l]), wq, wqs, wk, wvt, vones, tabs, S)
        y_a = attention(q, k, vt, mla_norm_g[l].reshape(MLA_W, 1), B, S)
        w_router = jnp.concatenate([w_group[l], w_expert[l],
                                    jnp.zeros((D_MODEL, R_LOG - N_GROUPS - N_EXP), F32)], axis=1).T
        wr_hi = w_router.astype(BF16)
        wr_lo = (w_router - wr_hi.astype(F32)).astype(BF16)
        b_router = jnp.concatenate([b_group[l], b_expert[l], jnp.zeros((R_LOG - N_GROUPS - N_EXP,), F32)])
        x_mid, t_tiles, logits = out_proj(x2d, y_m, y_r, y_a, w_out[l].astype(BF16), row(norm2_g[l]),
                                          wr_hi, wr_lo, b_router.reshape(R_LOG, 1))
        meta, cnt_rows = route(logits)
        meta = meta.reshape(T // ST, SUBLANES, ST)
        slot_base = (jnp.arange(T // ST, dtype=jnp.int32) % 2) * SLOT_ROWS
        lrow = meta[:, 0:2, :].astype(jnp.int32) * TOK_ROWS + slot_base[:, None, None]
        lrow = lrow.transpose(0, 2, 1).reshape(-1)
        w_pick = meta[:, 2:4, :].transpose(0, 2, 1).reshape(-1)
        cnt, lst, gst, pad, tile_cum, n_used = _routing_tables(cnt_rows)
        tile_ids = jnp.minimum(jnp.arange(n_tiles, dtype=jnp.int32), n_used[0] - 1)
        tile_expert = jnp.minimum(jnp.sum(tile_ids[:, None] >= tile_cum[None, :], axis=1),
                                  N_EXP - 1).astype(jnp.int32)
        xs = dispatch(lrow, cnt, lst, gst, pad, t_tiles, n_tiles * TM_E)
        ys = experts(tile_expert, n_used, xs, w_gate, w_up, w_down, l)
        x2d = combine(lrow, w_pick, cnt, lst, gst, ys, x_mid, row(final_norm_g), l == depth - 1)
    return x2d.reshape(B, S, D_MODEL)
```

```python
import functools

import numpy as np
import jax
import jax.numpy as jnp
from jax import lax
from jax.experimental import pallas as pl
from jax.experimental.pallas import tpu as pltpu

F32 = jnp.float32
BF16 = jnp.bfloat16

D_MODEL = 1024
ML_HEADS = 4
ML_DH = 64
ML_W = ML_HEADS * ML_DH
LRU_W = 256
LRU_BLOCKS = 4
CONV_W = 4
LRU_C = 8.0
MLA_H = 8
NOPE = 64
ROPE = 32
MLA_V = 64
MLA_W = MLA_H * MLA_V
Q_LORA = 256
KV_LORA = 128
ROPE_THETA = 10000.0
N_GROUPS = 4
EPG = 8
N_EXP = N_GROUPS * EPG
D_EXP = 256
EPS = 1e-6

LANES = 128
SUBLANES = 8
V_EXT = MLA_V + SUBLANES
TOK_ROWS = D_MODEL // LANES
R_LOG = 40
PROJ_W = 1536
SMALL_COL = 1408
VMEM_LIMIT = 56 * 1024 * 1024

ML_CHUNK = 128
ML_NB = 4
LRU_TC = 256
TM_IN = 512
TM_OUT = 512
OUT_SUBTILES = 4
TQ = 256
ST = 1024
TM_E = 256
SLOT_ROWS = 2 * ST * TOK_ROWS


def _cp(sem):
    return pltpu.CompilerParams(dimension_semantics=sem, vmem_limit_bytes=VMEM_LIMIT)


def _rms(x, g):
    return x * lax.rsqrt(jnp.mean(x * x, axis=-1, keepdims=True) + EPS) * g


def _in_proj_kernel(x_ref, g_ref, w_ref, wt_ref, o_ref, small_ref, vt_ref, ot_ref):
    y = _rms(x_ref[...], g_ref[...]).astype(BF16)
    acc = jnp.dot(y, w_ref[...], preferred_element_type=F32)
    o_ref[...] = acc[:, :SMALL_COL].astype(BF16)
    small_ref[...] = acc[:, SMALL_COL:]
    vo_t = lax.dot_general(wt_ref[...], y, (((1,), (1,)), ((), ())), preferred_element_type=F32)
    vt_ref[...] = vo_t[:ML_W].astype(BF16)
    ot_ref[...] = vo_t[ML_W:].astype(BF16)


def in_proj(x2d, g, w, wt):
    T = x2d.shape[0]
    return pl.pallas_call(
        _in_proj_kernel,
        grid=(T // TM_IN,),
        in_specs=[pl.BlockSpec((TM_IN, D_MODEL), lambda i: (i, 0)),
                  pl.BlockSpec((1, D_MODEL), lambda i: (0, 0)),
                  pl.BlockSpec((D_MODEL, PROJ_W), lambda i: (0, 0)),
                  pl.BlockSpec((2 * ML_W, D_MODEL), lambda i: (0, 0))],
        out_specs=[pl.BlockSpec((TM_IN, SMALL_COL), lambda i: (i, 0)),
                   pl.BlockSpec((TM_IN, LANES), lambda i: (i, 0)),
                   pl.BlockSpec((ML_W, TM_IN), lambda i: (0, i)),
                   pl.BlockSpec((ML_W, TM_IN), lambda i: (0, i))],
        out_shape=[jax.ShapeDtypeStruct((T, SMALL_COL), BF16),
                   jax.ShapeDtypeStruct((T, LANES), F32),
                   jax.ShapeDtypeStruct((ML_W, T), BF16),
                   jax.ShapeDtypeStruct((ML_W, T), BF16)],
        compiler_params=_cp(("arbitrary",)),
        name="in_proj",
    )(x2d, g, w, wt)


def _mlstm_kernel(q_ref, k_ref, s_ref, bias_ref, g_ref, *rest):
    nb, L, _ = q_ref.shape
    vt_refs, ot_refs = rest[:nb], rest[nb:2 * nb]
    y_ref, c_ref, m_ref = rest[2 * nb:]
    c = pl.program_id(1)

    @pl.when(c == 0)
    def _():
        c_ref[...] = jnp.zeros_like(c_ref)
        m_ref[...] = jnp.zeros_like(m_ref)

    lane = lax.broadcasted_iota(jnp.int32, (L, LANES), 1)
    row_i = lax.broadcasted_iota(jnp.int32, (L, L), 0)
    col_i = lax.broadcasted_iota(jnp.int32, (L, L), 1)
    lower = (col_i <= row_i).astype(F32)
    causal_t = row_i <= col_i
    nt = (((1,), (1,)), ((), ()))
    ones = jnp.ones((SUBLANES, L), F32)
    units = [(bi, h) for bi in range(nb) for h in range(ML_HEADS)]

    gates, cums = [], []
    for bi in range(nb):
        pre = s_ref[bi] + bias_ref[...]
        logsig = jnp.minimum(pre, 0.0) - jnp.log1p(jnp.exp(-jnp.abs(pre)))
        gate = jnp.where(lane < ML_HEADS, pre, logsig)
        gates.append(gate)
        cums.append(jnp.dot(lower, gate, precision=lax.Precision.HIGHEST, preferred_element_type=F32))

    qs, ks, states, scores, qcs = {}, {}, {}, {}, {}
    for bi, h in units:
        sl = slice(h * ML_DH, (h + 1) * ML_DH)
        qs[bi, h] = q_ref[bi, :, sl].astype(BF16)
        ks[bi, h] = (k_ref[bi, :, sl].astype(F32) * (ML_DH ** -0.5)).astype(BF16)
        states[bi, h] = c_ref[bi, h]
        scores[bi, h] = lax.dot_general(ks[bi, h], qs[bi, h], nt, preferred_element_type=F32)
        qcs[bi, h] = lax.dot_general(states[bi, h].astype(BF16), qs[bi, h], nt, preferred_element_type=F32)

    gates_t = [g.T for g in gates]
    cums_t = [cu.T for cu in cums]
    v_t = [vt_refs[bi][...].astype(F32) for bi in range(nb)]
    o_t = [jax.nn.sigmoid(ot_refs[bi][...].astype(F32)) for bi in range(nb)]

    s_t, m_rows, w_inters, nums, d_states, decays = {}, {}, {}, {}, {}, {}
    for bi, h in units:
        sl = slice(h * ML_DH, (h + 1) * ML_DH)
        a_col = gates[bi][:, h:h + 1] - cums[bi][:, ML_HEADS + h:ML_HEADS + h + 1]
        b_row = cums_t[bi][ML_HEADS + h:ML_HEADS + h + 1, :]
        li_row = gates_t[bi][h:h + 1, :]
        m_prev = m_ref[bi, :, h:h + 1]
        dmat = jnp.where(causal_t, a_col + b_row, -jnp.inf)
        inter = b_row + m_prev
        m_row = jnp.maximum(jnp.max(dmat, axis=0, keepdims=True), inter)
        w_inters[bi, h] = jnp.exp(inter - m_row)
        m_rows[bi, h] = m_row
        s_t[bi, h] = scores[bi, h] * jnp.exp(dmat - m_row)
        b_last = b_row[:, L - 1:L]
        g_row = b_last - b_row + li_row
        m_new = jnp.maximum(b_last + m_prev, jnp.max(g_row, axis=1, keepdims=True))
        decays[bi, h] = jnp.exp(b_last + m_prev - m_new)
        m_ref[bi, :, h:h + 1] = m_new
        vt_ext = jnp.concatenate([v_t[bi][sl, :], ones], axis=0) * jnp.exp(g_row - m_new)
        nums[bi, h] = jnp.dot(v_t[bi][sl, :].astype(BF16), s_t[bi, h].astype(BF16), preferred_element_type=F32)
        d_states[bi, h] = jnp.dot(vt_ext.astype(BF16), ks[bi, h], preferred_element_type=F32)

    for bi in range(nb):
        ys = []
        for h in range(ML_HEADS):
            sl = slice(h * ML_DH, (h + 1) * ML_DH)
            qc, w_inter = qcs[bi, h], w_inters[bi, h]
            den = jnp.sum(s_t[bi, h], axis=0, keepdims=True) + w_inter * qc[ML_DH:ML_DH + 1, :]
            hh = (nums[bi, h] + w_inter * qc[:ML_DH, :]) / jnp.maximum(jnp.abs(den), jnp.exp(-m_rows[bi, h]))
            y = o_t[bi][sl, :] * hh
            ys.append(y * lax.rsqrt(jnp.mean(y * y, axis=0, keepdims=True) + EPS) * g_ref[sl, :])
            c_ref[bi, h] = decays[bi, h] * states[bi, h] + d_states[bi, h]
        y_ref[bi] = jnp.concatenate(ys, axis=0).T.astype(BF16)


def mlstm(proj3, small3, v_t, o_t, bias, g):
    B, S, _ = proj3.shape
    L = ML_CHUNK
    nb = ML_NB
    nc = S // L
    col = lambda j: pl.BlockSpec((nb, L, ML_W), lambda b, c, j=j: (b, c, j))
    tcol = [pl.BlockSpec((ML_W, L), lambda b, c, bi=bi: (0, (b * nb + bi) * nc + c)) for bi in range(nb)]
    return pl.pallas_call(
        _mlstm_kernel,
        grid=(B // nb, S // L),
        in_specs=[col(0), col(1),
                  pl.BlockSpec((nb, L, LANES), lambda b, c: (b, c, 0)),
                  pl.BlockSpec((1, LANES), lambda b, c: (0, 0)),
                  pl.BlockSpec((ML_W, 1), lambda b, c: (0, 0))] + tcol + tcol,
        out_specs=pl.BlockSpec((nb, L, ML_W), lambda b, c: (b, c, 0)),
        out_shape=jax.ShapeDtypeStruct((B, S, ML_W), BF16),
        scratch_shapes=[pltpu.VMEM((nb, ML_HEADS, ML_DH + SUBLANES, ML_DH), F32),
                        pltpu.VMEM((nb, 1, LANES), F32)],
        compiler_params=_cp(("arbitrary", "arbitrary")),
        name="mlstm",
    )(proj3, proj3, small3, bias, g, *([v_t] * nb), *([o_t] * nb))


def _rglru_kernel(x_ref, gate_ref, cw_ref, cb_ref, wg_ref, bg_ref, lam_ref, g_ref, y_ref,
                  xp_ref, a_ref, u_ref, h_ref, carry_ref):
    B, tc, W = x_ref.shape
    c = pl.program_id(0)

    @pl.when(c == 0)
    def _():
        xp_ref[:, 0:SUBLANES, :] = jnp.zeros((B, SUBLANES, W), F32)
        carry_ref[...] = jnp.zeros_like(carry_ref)

    xp_ref[:, SUBLANES:, :] = x_ref[...].astype(F32)
    xc = jnp.zeros((B, tc, W), F32) + cb_ref[...]
    for j in range(CONV_W):
        off = SUBLANES - (CONV_W - 1) + j
        xc = xc + cw_ref[j:j + 1, :] * xp_ref[:, off:off + tc, :]
    xp_ref[:, 0:SUBLANES, :] = x_ref[:, tc - SUBLANES:, :].astype(F32)

    xc2 = xc.reshape(B * tc, W)
    ri = jax.nn.sigmoid(jnp.dot(xc2.astype(BF16), wg_ref[...], preferred_element_type=F32) + bg_ref[...])
    r = ri[:, :W]
    ig = ri[:, W:]
    lam = lam_ref[...]
    softplus_neg = jnp.maximum(-lam, 0.0) + jnp.log1p(jnp.exp(-jnp.abs(lam)))
    log_a = -LRU_C * r * softplus_neg
    a = jnp.exp(log_a)
    u = jnp.sqrt(1.0 - jnp.exp(2.0 * log_a)) * (ig * xc2)
    halves = W // LANES
    for k in range(halves):
        a_ref[k] = a[:, k * LANES:(k + 1) * LANES]
        u_ref[k] = u[:, k * LANES:(k + 1) * LANES]

    def step(i, hs):
        base = pl.multiple_of(i * SUBLANES, SUBLANES)
        for j in range(SUBLANES):
            rows = pl.ds(base + j, B, stride=tc)
            hs = tuple(a_ref.at[k][rows, :] * hs[k] + u_ref.at[k][rows, :] for k in range(halves))
            for k in range(halves):
                h_ref.at[k][rows, :] = hs[k]
        return hs

    init = tuple(carry_ref[k] for k in range(halves))
    last = lax.fori_loop(0, tc // SUBLANES, step, init)
    for k in range(halves):
        carry_ref[k] = last[k]
    h_all = jnp.concatenate([h_ref[k] for k in range(halves)], axis=1)
    y = h_all.reshape(B, tc, W) * jax.nn.gelu(gate_ref[...].astype(F32))
    y_ref[...] = _rms(y, g_ref[...]).astype(BF16)


def rglru(proj3, cw, cb, wg, bg, lam, g):
    B, S, _ = proj3.shape
    tc = LRU_TC
    vec = lambda n: pl.BlockSpec((1, n), lambda c: (0, 0))
    return pl.pallas_call(
        _rglru_kernel,
        grid=(S // tc,),
        in_specs=[pl.BlockSpec((B, tc, LRU_W), lambda c: (0, c, 2)),
                  pl.BlockSpec((B, tc, LRU_W), lambda c: (0, c, 3)),
                  pl.BlockSpec((CONV_W, LRU_W), lambda c: (0, 0)),
                  vec(LRU_W),
                  pl.BlockSpec((LRU_W, 2 * LRU_W), lambda c: (0, 0)),
                  vec(2 * LRU_W), vec(LRU_W), vec(LRU_W)],
        out_specs=pl.BlockSpec((B, tc, LRU_W), lambda c: (0, c, 0)),
        out_shape=jax.ShapeDtypeStruct((B, S, LRU_W), BF16),
        scratch_shapes=[pltpu.VMEM((B, tc + SUBLANES, LRU_W), F32),
                        pltpu.VMEM((LRU_W // LANES, B * tc, LANES), F32),
                        pltpu.VMEM((LRU_W // LANES, B * tc, LANES), F32),
                        pltpu.VMEM((LRU_W // LANES, B * tc, LANES), F32),
                        pltpu.VMEM((LRU_W // LANES, B, LANES), F32)],
        compiler_params=_cp(("arbitrary",)),
        name="rglru",
    )(proj3, proj3, cw, cb, wg, bg, lam, g)


def _rot_half(blk):
    lane = lax.broadcasted_iota(jnp.int32, blk.shape, 1)
    return jnp.where(lane < NOPE + ROPE // 2,
                     pltpu.roll(blk, LANES - ROPE // 2, 1), pltpu.roll(blk, ROPE // 2, 1))


def _mla_proj_kernel(cq_ref, ckv_ref, small_ref, qg_ref, kvg_ref, wq_ref, wqs_ref, wk_ref, wvt_ref, vones_ref,
                     cq_t, sq_t, ck_t, sk_t, q_ref, k_ref, vt_ref):
    qn = _rms(cq_ref[...].astype(F32), qg_ref[...]).astype(BF16)
    qf = jnp.dot(qn, wq_ref[...], preferred_element_type=F32)
    qs = jnp.dot(qn, wqs_ref[...], preferred_element_type=F32)
    kvn = _rms(ckv_ref[...].astype(F32), kvg_ref[...]).astype(BF16)
    kf = jnp.dot(kvn, wk_ref[...], preferred_element_type=F32)
    vt_ref[...] = (lax.dot_general(wvt_ref[...], kvn, (((1,), (1,)), ((), ())), preferred_element_type=F32)
                   + vones_ref[...]).astype(BF16)
    small = small_ref[...]
    kr = small * ck_t[...] + _rot_half(small) * sk_t[...]
    cq = cq_t[...]
    sq = sq_t[...]
    for h in range(MLA_H):
        sl = slice(h * LANES, (h + 1) * LANES)
        q_ref[:, sl] = (qf[:, sl] * cq + qs[:, sl] * sq).astype(BF16)
        k_ref[:, sl] = (kf[:, sl] + kr).astype(BF16)


def mla_proj(proj, small, qg, kvg, wq, wqs, wk, wvt, vones, tabs, S):
    T = proj.shape[0]
    tm = TM_IN
    npos = S // tm
    tab = pl.BlockSpec((tm, LANES), lambda i: (i % npos, 0))
    full = lambda a: pl.BlockSpec(a.shape, lambda i: (0, 0))
    return pl.pallas_call(
        _mla_proj_kernel,
        grid=(T // tm,),
        in_specs=[pl.BlockSpec((tm, Q_LORA), lambda i: (i, 4)),
                  pl.BlockSpec((tm, KV_LORA), lambda i: (i, (SMALL_COL - KV_LORA) // KV_LORA)),
                  pl.BlockSpec((tm, LANES), lambda i: (i, 0)),
                  full(qg), full(kvg), full(wq), full(wqs), full(wk), full(wvt), full(vones),
                  tab, tab, tab, tab],
        out_specs=[pl.BlockSpec((tm, MLA_H * LANES), lambda i: (i, 0)),
                   pl.BlockSpec((tm, MLA_H * LANES), lambda i: (i, 0)),
                   pl.BlockSpec((MLA_H * V_EXT, tm), lambda i: (0, i))],
        out_shape=[jax.ShapeDtypeStruct((T, MLA_H * LANES), BF16),
                   jax.ShapeDtypeStruct((T, MLA_H * LANES), BF16),
                   jax.ShapeDtypeStruct((MLA_H * V_EXT, T), BF16)],
        compiler_params=_cp(("arbitrary",)),
        name="mla_proj",
    )(proj, proj, small, qg, kvg, wq, wqs, wk, wvt, vones, *tabs)


def _attn_kernel(q_ref, k_ref, vt_ref, g_ref, o_ref, acc_ref, m_ref, yt_ref):
    tq = q_ref.shape[0]
    i = pl.program_id(1)
    key_i = lax.broadcasted_iota(jnp.int32, (tq, tq), 0)
    qry_i = lax.broadcasted_iota(jnp.int32, (tq, tq), 1)
    causal = key_i <= qry_i
    nt = (((1,), (1,)), ((), ()))
    m_ref[...] = jnp.full_like(m_ref, -jnp.inf)
    acc_ref[...] = jnp.zeros_like(acc_ref)

    ahead = 4

    def scores(off, h):
        hsl = slice(h * LANES, (h + 1) * LANES)
        return lax.dot_general(k_ref[pl.ds(off, tq), hsl], q_ref[:, hsl], nt, preferred_element_type=F32)

    def update(h, alpha, pv):
        rows = slice(h * V_EXT, (h + 1) * V_EXT)
        acc_ref[rows, :] = alpha * acc_ref[rows, :] + pv

    def blocks(items):
        work = [(pl.multiple_of(j * tq, tq), h, masked) for j, masked in items for h in range(MLA_H)]
        pending = [scores(off, h) for off, h, _ in work[:ahead]]
        late = None
        for n, (off, h, masked) in enumerate(work):
            s = pending.pop(0)
            if n + ahead < len(work):
                pending.append(scores(*work[n + ahead][:2]))
            if masked:
                s = jnp.where(causal, s, -jnp.inf)
            m = m_ref[h:h + 1, :]
            m_new = jnp.maximum(m, jnp.max(s, axis=0, keepdims=True))
            m_ref[h:h + 1, :] = m_new
            alpha = jnp.exp2(m - m_new)
            p = jnp.exp2(s - m_new).astype(BF16)
            pv = jnp.dot(vt_ref[h * V_EXT:(h + 1) * V_EXT, pl.ds(off, tq)], p, preferred_element_type=F32)
            if late is not None:
                update(*late)
            late = (h, alpha, pv)
        update(*late)

    def body(pair, carry):
        blocks([(2 * pair, False), (2 * pair + 1, False)])
        return carry

    lax.fori_loop(0, i // 2, body, 0)

    @pl.when(i % 2 == 1)
    def _():
        blocks([(i - 1, False), (i, True)])

    @pl.when(i % 2 == 0)
    def _():
        blocks([(i, True)])

    ssq = jnp.zeros((1, tq), F32)
    for h in range(MLA_H):
        vrows = slice(h * MLA_V, (h + 1) * MLA_V)
        yh = acc_ref[h * V_EXT:h * V_EXT + MLA_V, :] * (1.0 / acc_ref[h * V_EXT + MLA_V:h * V_EXT + MLA_V + 1, :])
        yt_ref[vrows, :] = yh
        ssq = ssq + jnp.sum(yh * yh, axis=0, keepdims=True)
    yt = yt_ref[...] * lax.rsqrt(ssq * (1.0 / MLA_W) + EPS) * g_ref[...]
    o_ref[...] = yt.T.astype(BF16)


def attention(q, k, vt, g, B, S):
    T = B * S
    nq = S // TQ
    return pl.pallas_call(
        _attn_kernel,
        grid=(B, nq),
        in_specs=[pl.BlockSpec((TQ, MLA_H * LANES), lambda b, i: (b * nq + i, 0)),
                  pl.BlockSpec((S, MLA_H * LANES), lambda b, i: (b, 0)),
                  pl.BlockSpec((MLA_H * V_EXT, S), lambda b, i: (0, b)),
                  pl.BlockSpec((MLA_W, 1), lambda b, i: (0, 0))],
        out_specs=pl.BlockSpec((TQ, MLA_W), lambda b, i: (b * nq + i, 0)),
        out_shape=jax.ShapeDtypeStruct((T, MLA_W), BF16),
        scratch_shapes=[pltpu.VMEM((MLA_H * V_EXT, TQ), F32),
                        pltpu.VMEM((MLA_H, TQ), F32),
                        pltpu.VMEM((MLA_W, TQ), F32)],
        compiler_params=_cp(("arbitrary", "arbitrary")),
        name="attention",
    )(q, k, vt, g)


def _out_proj_kernel(x_ref, ym_ref, yr_ref, ya_ref, w_ref, g_ref, wrh_ref, wrl_ref, br_ref,
                     xm_ref, t_ref, lg_ref):
    tm = x_ref.shape[0]
    sub = tm // OUT_SUBTILES
    nt = (((1,), (1,)), ((), ()))

    def project(k):
        rows = slice(k * sub, (k + 1) * sub)
        acc = jnp.dot(ym_ref[0, rows, :].astype(BF16), w_ref[0:ML_W, :], preferred_element_type=F32)
        acc += jnp.dot(yr_ref[0, rows, :].astype(BF16), w_ref[ML_W:ML_W + LRU_W, :], preferred_element_type=F32)
        acc += jnp.dot(ya_ref[rows, :].astype(BF16), w_ref[ML_W + LRU_W:, :], preferred_element_type=F32)
        return acc

    def finish(k, acc):
        rows = slice(k * sub, (k + 1) * sub)
        xm = x_ref[rows, :] + acc
        xm_ref[rows, :] = xm
        t = _rms(xm, g_ref[...])
        t_hi = t.astype(BF16)
        t_lo = (t - t_hi.astype(F32)).astype(BF16)
        lg = lax.dot_general(wrh_ref[...], t_hi, nt, preferred_element_type=F32)
        lg += lax.dot_general(wrl_ref[...], t_hi, nt, preferred_element_type=F32)
        lg += lax.dot_general(wrh_ref[...], t_lo, nt, preferred_element_type=F32)
        lg_ref[:, rows] = lg + br_ref[...]
        for s in range(TOK_ROWS):
            t_ref[pl.ds(k * sub * TOK_ROWS + s, sub, stride=TOK_ROWS), :] = t[:, s * LANES:(s + 1) * LANES]

    acc = project(0)
    for k in range(OUT_SUBTILES):
        nxt = project(k + 1) if k + 1 < OUT_SUBTILES else None
        finish(k, acc)
        acc = nxt


def out_proj(x2d, ym3, yr3, ya, w, g, wr_hi, wr_lo, br):
    T = x2d.shape[0]
    tm = TM_OUT
    per_seq = ym3.shape[1] // tm
    row = lambda n: pl.BlockSpec((tm, n), lambda i: (i, 0))
    seq = lambda n: pl.BlockSpec((1, tm, n), lambda i: (i // per_seq, i % per_seq, 0))
    full = lambda a: pl.BlockSpec(a.shape, lambda i: (0, 0))
    return pl.pallas_call(
        _out_proj_kernel,
        grid=(T // tm,),
        in_specs=[row(D_MODEL), seq(ML_W), seq(LRU_W), row(MLA_W), full(w), full(g),
                  full(wr_hi), full(wr_lo), full(br)],
        out_specs=[row(D_MODEL),
                   pl.BlockSpec((tm * TOK_ROWS, LANES), lambda i: (i, 0)),
                   pl.BlockSpec((R_LOG, tm), lambda i: (0, i))],
        out_shape=[jax.ShapeDtypeStruct((T, D_MODEL), F32),
                   jax.ShapeDtypeStruct((T * TOK_ROWS, LANES), F32),
                   jax.ShapeDtypeStruct((R_LOG, T), F32)],
        compiler_params=_cp(("arbitrary",)),
        name="out_proj",
    )(x2d, ym3, yr3, ya, w, g, wr_hi, wr_lo, br)


def _route_kernel(lg_ref, meta_ref, cnt_ref):
    st = lg_ref.shape[1]
    lg = lg_ref[...]
    row = lax.broadcasted_iota(jnp.int32, (R_LOG, st), 0)
    col_max = lambda a: jnp.max(a, axis=0, keepdims=True)
    col_sum = lambda a: jnp.sum(a, axis=0, keepdims=True)
    first = lambda cond: jnp.min(jnp.where(cond, row, R_LOG), axis=0, keepdims=True)

    gmask = row < N_GROUPS
    gl = jnp.where(gmask, lg, -jnp.inf)
    ge = jnp.exp(gl - col_max(gl))
    gp = ge / col_sum(ge)
    g_val = col_max(gp)
    g_idx = first(gmask & (gp == g_val))

    smask = (row >= N_GROUPS) & (row < N_GROUPS + N_EXP) & (((row - N_GROUPS) >> 3) == g_idx)
    el = jnp.where(smask, lg, -jnp.inf)
    ee = jnp.exp(el - col_max(el))
    ep = ee / col_sum(ee)
    v1 = col_max(ep)
    i1 = first(smask & (ep == v1))
    rest = smask & (row != i1)
    v2 = col_max(jnp.where(rest, ep, -1.0))
    i2 = first(rest & (ep == v2))
    den = v1 + v2
    w1 = v1 / den * g_val
    w2 = v2 / den * g_val

    onehot = ((row == i1) | (row == i2)).astype(F32)
    sb = min(st, 256)
    r_i = lax.broadcasted_iota(jnp.int32, (sb, sb), 0)
    c_i = lax.broadcasted_iota(jnp.int32, (sb, sb), 1)
    before = (r_i < c_i).astype(BF16)
    run = jnp.zeros((R_LOG, 1), F32)
    ranks = []
    for j in range(st // sb):
        oh = onehot[:, j * sb:(j + 1) * sb]
        ranks.append(jnp.dot(oh.astype(BF16), before, preferred_element_type=F32) + run)
        run = run + jnp.sum(oh, axis=1, keepdims=True)
    rank = jnp.concatenate(ranks, axis=1)
    e_r = lax.broadcasted_iota(jnp.int32, (R_LOG, R_LOG), 0)
    e_c = lax.broadcasted_iota(jnp.int32, (R_LOG, R_LOG), 1)
    counts = jnp.broadcast_to(run, (R_LOG, LANES))
    lstart = jnp.dot((e_c < e_r).astype(F32), counts, precision=lax.Precision.HIGHEST,
                     preferred_element_type=F32)
    slot = rank + lstart[:, 0:1]
    p1 = col_sum(jnp.where(row == i1, slot, 0.0))
    p2 = col_sum(jnp.where(row == i2, slot, 0.0))
    meta_ref[...] = jnp.concatenate([p1, p2, w1, w2, jnp.zeros((SUBLANES - 4, st), F32)], axis=0)
    lane = lax.broadcasted_iota(jnp.int32, (R_LOG, LANES), 1)
    cnt_ref[...] = jnp.where(lane == 0, counts, jnp.where(lane == 1, lstart, 0.0))


def route(logits_t):
    T = logits_t.shape[1]
    n_st = T // ST
    return pl.pallas_call(
        _route_kernel,
        grid=(n_st,),
        in_specs=[pl.BlockSpec((R_LOG, ST), lambda i: (0, i))],
        out_specs=[pl.BlockSpec((SUBLANES, ST), lambda i: (i, 0)),
                   pl.BlockSpec((R_LOG, LANES), lambda i: (i, 0))],
        out_shape=[jax.ShapeDtypeStruct((n_st * SUBLANES, ST), F32),
                   jax.ShapeDtypeStruct((n_st * R_LOG, LANES), F32)],
        compiler_params=_cp(("arbitrary",)),
        name="route",
    )(logits_t)


def _copy_by_bits(n, nbits, copy_at, start):
    off = jnp.int32(0)
    for bit in reversed(range(nbits)):
        sz = 1 << bit

        @pl.when((n & sz) != 0)
        def _(off=off, sz=sz):
            cp = copy_at(off, sz)
            if start:
                cp.start()
            else:
                cp.wait()

        off = off + (n & sz)


def _strip_copies(st, cnt_ref, lst_ref, gst_ref, make_copy, start):
    def body(e, carry):
        loc = lst_ref[st * N_EXP + e]
        glob = gst_ref[st * N_EXP + e]
        _copy_by_bits(
            cnt_ref[st * N_EXP + e], int(np.log2(ST)) + 1,
            lambda off, sz: make_copy(pl.multiple_of((loc + off) * TOK_ROWS, TOK_ROWS),
                                      pl.multiple_of((glob + off) * TOK_ROWS, TOK_ROWS), sz * TOK_ROWS),
            start)
        return carry

    lax.fori_loop(0, N_EXP, body, 0)


def _zero_padding(pad_ref, xs_ref, zero_ref, sem, start):
    tile_rows = TM_E * TOK_ROWS

    def body(e, carry):
        first = pad_ref[e]
        _copy_by_bits(
            pad_ref[N_EXP + e], int(np.log2(TM_E)),
            lambda off, sz: pltpu.make_async_copy(
                zero_ref.at[pl.ds(0, sz * TOK_ROWS)],
                xs_ref.at[pl.ds(pl.multiple_of((first + off) * TOK_ROWS, TOK_ROWS), sz * TOK_ROWS)], sem),
            start)
        return carry

    lax.fori_loop(0, N_EXP, body, 0)

    def tail(i, carry):
        cp = pltpu.make_async_copy(
            zero_ref, xs_ref.at[pl.ds(pl.multiple_of(i * tile_rows, tile_rows), tile_rows)], sem)
        if start:
            cp.start()
        else:
            cp.wait()
        return carry

    lax.fori_loop(pad_ref[2 * N_EXP], xs_ref.shape[0] // tile_rows, tail, 0)


def _dispatch_kernel(lrow_ref, cnt_ref, lst_ref, gst_ref, pad_ref, t_ref, xs_ref,
                     stage_ref, zero_ref, sem, zsem):
    st = pl.program_id(0)
    last = pl.num_programs(0) - 1
    slot = st % 2

    @pl.when(st == 0)
    def _():
        zero_ref[...] = jnp.zeros_like(zero_ref)
        _zero_padding(pad_ref, xs_ref, zero_ref, zsem, True)

    def scatter(i, carry):
        pick0 = st * 2 * ST + i * SUBLANES
        for u in range(SUBLANES):
            t = i * SUBLANES + u
            tile = t_ref[pl.ds(pl.multiple_of(t * TOK_ROWS, TOK_ROWS), TOK_ROWS), :]
            for k in range(2):
                row = pl.multiple_of(lrow_ref[pick0 + k * ST + u], TOK_ROWS)
                stage_ref[pl.ds(row, TOK_ROWS), :] = tile
        return carry

    lax.fori_loop(0, ST // SUBLANES, scatter, 0)

    def slot_rows(buf, loc, rows):
        return stage_ref.at[pl.ds(pl.multiple_of(buf * SLOT_ROWS + loc, TOK_ROWS), rows)]

    def copies(step, start):
        buf = step % 2
        _strip_copies(
            step, cnt_ref, lst_ref, gst_ref,
            lambda loc, glob, rows: pltpu.make_async_copy(
                slot_rows(buf, loc, rows), xs_ref.at[pl.ds(glob, rows)], sem.at[buf]),
            start)

    def wait_all(buf):
        pltpu.make_async_copy(slot_rows(buf, 0, SLOT_ROWS), xs_ref.at[pl.ds(0, SLOT_ROWS)], sem.at[buf]).wait()

    copies(st, True)

    @pl.when(st > 0)
    def _():
        wait_all(1 - slot)

    @pl.when(st == last)
    def _():
        wait_all(slot)
        _zero_padding(pad_ref, xs_ref, zero_ref, zsem, False)


def dispatch(lrow, cnt, lst, gst, pad, t_tiles, n_rows):
    T = t_tiles.shape[0] // TOK_ROWS
    return pl.pallas_call(
        _dispatch_kernel,
        grid_spec=pltpu.PrefetchScalarGridSpec(
            num_scalar_prefetch=5,
            grid=(T // ST,),
            in_specs=[pl.BlockSpec((ST * TOK_ROWS, LANES), lambda i, *_: (i, 0))],
            out_specs=pl.BlockSpec(memory_space=pl.ANY),
            scratch_shapes=[pltpu.VMEM((2 * SLOT_ROWS, LANES), F32),
                            pltpu.VMEM((TM_E * TOK_ROWS, LANES), F32),
                            pltpu.SemaphoreType.DMA((2,)),
                            pltpu.SemaphoreType.DMA(())]),
        out_shape=jax.ShapeDtypeStruct((n_rows * TOK_ROWS, LANES), F32),
        compiler_params=_cp(("arbitrary",)),
        name="dispatch",
    )(lrow, cnt, lst, gst, pad, t_tiles)


def _combine_kernel(lrow_ref, w_ref, cnt_ref, lst_ref, gst_ref, ys_ref, xm_ref, fg_ref, o_ref,
                    stage_ref, comb_ref, sem, *, final):
    st = pl.program_id(0)
    last = pl.num_programs(0) - 1
    slot = st % 2

    def slot_rows(buf, loc, rows):
        return stage_ref.at[pl.ds(pl.multiple_of(buf * SLOT_ROWS + loc, TOK_ROWS), rows)]

    def copies(step, start):
        buf = step % 2
        _strip_copies(
            step, cnt_ref, lst_ref, gst_ref,
            lambda loc, glob, rows: pltpu.make_async_copy(
                ys_ref.at[pl.ds(glob, rows)], slot_rows(buf, loc, rows), sem.at[buf]),
            start)

    @pl.when(st == 0)
    def _():
        copies(st, True)

    @pl.when(st < last)
    def _():
        copies(st + 1, True)

    pltpu.make_async_copy(ys_ref.at[pl.ds(0, SLOT_ROWS)], slot_rows(slot, 0, SLOT_ROWS), sem.at[slot]).wait()

    def gather(i, carry):
        pick0 = st * 2 * ST + i * SUBLANES
        for u in range(SUBLANES):
            t = i * SUBLANES + u
            acc = None
            for k in range(2):
                row = pl.multiple_of(lrow_ref[pick0 + k * ST + u], TOK_ROWS)
                term = w_ref[pick0 + k * ST + u] * stage_ref[pl.ds(row, TOK_ROWS), :]
                acc = term if acc is None else acc + term
            comb_ref[pl.ds(pl.multiple_of(t * TOK_ROWS, TOK_ROWS), TOK_ROWS), :] = acc
        return carry

    lax.fori_loop(0, ST // SUBLANES, gather, 0)
    moe = jnp.concatenate([comb_ref[pl.ds(s, ST, stride=TOK_ROWS), :] for s in range(TOK_ROWS)], axis=1)
    x_new = xm_ref[...] + moe
    o_ref[...] = _rms(x_new, fg_ref[...]) if final else x_new


def combine(lrow, w, cnt, lst, gst, ys, xm, fg, final):
    T = xm.shape[0]
    return pl.pallas_call(
        functools.partial(_combine_kernel, final=final),
        grid_spec=pltpu.PrefetchScalarGridSpec(
            num_scalar_prefetch=5,
            grid=(T // ST,),
            in_specs=[pl.BlockSpec(memory_space=pl.ANY),
                      pl.BlockSpec((ST, D_MODEL), lambda i, *_: (i, 0)),
                      pl.BlockSpec((1, D_MODEL), lambda i, *_: (0, 0))],
            out_specs=pl.BlockSpec((ST, D_MODEL), lambda i, *_: (i, 0)),
            scratch_shapes=[pltpu.VMEM((2 * SLOT_ROWS, LANES), F32),
                            pltpu.VMEM((ST * TOK_ROWS, LANES), F32),
                            pltpu.SemaphoreType.DMA((2,))]),
        out_shape=jax.ShapeDtypeStruct((T, D_MODEL), F32),
        compiler_params=_cp(("arbitrary",)),
        name="combine",
    )(lrow, w, cnt, lst, gst, ys, xm, fg)


def _expert_kernel(te_ref, nused_ref, xs_ref, wg_ref, wu_ref, wd_ref, ys_ref):
    i = pl.program_id(0)
    tm = xs_ref.shape[0] // TOK_ROWS

    @pl.when(i < nused_ref[0])
    def _():
        x = jnp.concatenate([xs_ref[pl.ds(s, tm, stride=TOK_ROWS), :] for s in range(TOK_ROWS)],
                            axis=1).astype(BF16)
        g = jnp.dot(x, wg_ref[0, 0].astype(BF16), preferred_element_type=F32)
        u = jnp.dot(x, wu_ref[0, 0].astype(BF16), preferred_element_type=F32)
        h = (g * jax.nn.sigmoid(g) * u).astype(BF16)
        y = jnp.dot(h, wd_ref[0, 0].astype(BF16), preferred_element_type=F32)
        for s in range(TOK_ROWS):
            ys_ref[pl.ds(s, tm, stride=TOK_ROWS), :] = y[:, s * LANES:(s + 1) * LANES]

    @pl.when(i >= nused_ref[0])
    def _():
        ys_ref[...] = jnp.zeros_like(ys_ref)


def experts(tile_expert, n_used, xs, wg, wu, wd, layer):
    n_tiles = xs.shape[0] // (TM_E * TOK_ROWS)
    tile = lambda i, te, nu: (jnp.maximum(jnp.minimum(i, nu[0] - 1), 0), 0)
    return pl.pallas_call(
        _expert_kernel,
        grid_spec=pltpu.PrefetchScalarGridSpec(
            num_scalar_prefetch=2,
            grid=(n_tiles,),
            in_specs=[pl.BlockSpec((TM_E * TOK_ROWS, LANES), tile),
                      pl.BlockSpec((1, 1, D_MODEL, D_EXP), lambda i, te, nu: (layer, te[i], 0, 0)),
                      pl.BlockSpec((1, 1, D_MODEL, D_EXP), lambda i, te, nu: (layer, te[i], 0, 0)),
                      pl.BlockSpec((1, 1, D_EXP, D_MODEL), lambda i, te, nu: (layer, te[i], 0, 0))],
            out_specs=pl.BlockSpec((TM_E * TOK_ROWS, LANES), lambda i, te, nu: (i, 0))),
        out_shape=jax.ShapeDtypeStruct(xs.shape, F32),
        compiler_params=_cp(("arbitrary",)),
        name="experts",
    )(tile_expert, n_used, xs, wg, wu, wd)


def _rope_tables(S):
    inv = 1.0 / (ROPE_THETA ** (jnp.arange(0, ROPE, 2, dtype=F32) / ROPE))
    ang = jnp.arange(S, dtype=F32)[:, None] * inv[None, :]
    cos, sin = jnp.cos(ang), jnp.sin(ang)
    z = lambda n: jnp.zeros((S, n), F32)
    scale = (NOPE + ROPE) ** -0.5 * np.log2(np.e)
    pad = LANES - NOPE - ROPE
    cq = scale * jnp.concatenate([jnp.ones((S, NOPE), F32), cos, cos, z(pad)], axis=1)
    sq = scale * jnp.concatenate([z(NOPE), -sin, sin, z(pad)], axis=1)
    ck = jnp.concatenate([z(NOPE), cos, cos, z(pad)], axis=1)
    sk = jnp.concatenate([z(NOPE), -sin, sin, z(pad)], axis=1)
    return cq, sq, ck, sk


def _block_diag(w):
    n, d, _ = w.shape
    return (jnp.eye(n, dtype=w.dtype)[:, None, :, None] * w[:, :, None, :]).reshape(n * d, n * d)


def _in_weights(w_in):
    z = lambda n: jnp.zeros((D_MODEL, n), F32)
    o = 4 * ML_W
    mi, mf = w_in[:, o:o + ML_HEADS], w_in[:, o + ML_HEADS:o + 2 * ML_HEADS]
    o += 2 * ML_HEADS
    rest = w_in[:, o:o + 2 * LRU_W + Q_LORA + KV_LORA]
    kr = w_in[:, o + 2 * LRU_W + Q_LORA + KV_LORA:]
    small = jnp.concatenate([mi, mf, z(NOPE - 2 * ML_HEADS), kr, z(LANES - NOPE - ROPE)], axis=1)
    w_main = jnp.concatenate([w_in[:, :2 * ML_W], rest, small], axis=1).astype(BF16)
    return w_main, w_in[:, 2 * ML_W:4 * ML_W].T.astype(BF16)


def _routing_tables(cnt_rows):
    n_st = cnt_rows.shape[0] // R_LOG
    rows = cnt_rows.reshape(n_st, R_LOG, LANES)
    cnt = rows[:, N_GROUPS:N_GROUPS + N_EXP, 0].astype(jnp.int32)
    lst = rows[:, N_GROUPS:N_GROUPS + N_EXP, 1].astype(jnp.int32)
    tiles_e = (jnp.sum(cnt, axis=0) + TM_E - 1) // TM_E
    tile_cum = jnp.cumsum(tiles_e)
    base = (tile_cum - tiles_e) * TM_E
    gst = base[None, :] + jnp.cumsum(cnt, axis=0) - cnt
    n_used = tile_cum[-1:].astype(jnp.int32)
    n_e = jnp.sum(cnt, axis=0)
    pad = jnp.concatenate([base + n_e, tiles_e * TM_E - n_e, n_used]).astype(jnp.int32)
    return cnt.reshape(-1), lst.reshape(-1), gst.reshape(-1), pad, tile_cum, n_used


def kernel(x, norm1_g, w_in, ml_i_bias, ml_f_bias, ml_norm_g, conv_w, conv_b, lru_w_r, lru_b_r, lru_w_i, lru_b_i, lru_lambda, lru_norm_g, q_norm_g, w_uq, kv_norm_g, w_ukv, mla_norm_g, w_out, norm2_g, w_group, b_group, w_expert, b_expert, w_gate, w_up, w_down, final_norm_g):
    B, S, _ = x.shape
    T = B * S
    depth = w_in.shape[0]
    n_tiles = (2 * T) // TM_E + N_EXP
    tabs = _rope_tables(S)
    vones = jnp.tile(jnp.concatenate([jnp.zeros((MLA_V,), F32), jnp.ones((V_EXT - MLA_V,), F32)]),
                     MLA_H).reshape(MLA_H * V_EXT, 1)
    row = lambda a: a.reshape(1, -1)
    x2d = x.reshape(T, D_MODEL)
    for l in range(depth):
        proj, small, v_t, o_t = in_proj(x2d, row(norm1_g[l]), *_in_weights(w_in[l]))
        gate_bias = jnp.concatenate([ml_i_bias[l], ml_f_bias[l], jnp.zeros((LANES - 2 * ML_HEADS,), F32)])
        proj3 = proj.reshape(B, S, SMALL_COL)
        y_m = mlstm(proj3, small.reshape(B, S, LANES), v_t, o_t, row(gate_bias),
                    ml_norm_g[l].reshape(ML_W, 1))
        w_gates = jnp.concatenate([_block_diag(lru_w_r[l]), _block_diag(lru_w_i[l])], axis=1).astype(BF16)
        b_gates = jnp.concatenate([lru_b_r[l], lru_b_i[l]])
        y_r = rglru(proj3, conv_w[l], row(conv_b[l]), w_gates, row(b_gates),
                    row(lru_lambda[l]), row(lru_norm_g[l]))
        wq3 = w_uq[l].reshape(Q_LORA, MLA_H, NOPE + ROPE)
        wq = jnp.pad(wq3, ((0, 0), (0, 0), (0, LANES - NOPE - ROPE))).reshape(Q_LORA, MLA_H * LANES).astype(BF16)
        wqs = jnp.concatenate([jnp.zeros((Q_LORA, MLA_H, NOPE), F32), wq3[:, :, NOPE + ROPE // 2:],
                               wq3[:, :, NOPE:NOPE + ROPE // 2],
                               jnp.zeros((Q_LORA, MLA_H, LANES - NOPE - ROPE), F32)],
                              axis=2).reshape(Q_LORA, MLA_H * LANES).astype(BF16)
        wkv = w_ukv[l].reshape(KV_LORA, MLA_H, NOPE + MLA_V)
        wk = jnp.pad(wkv[:, :, :NOPE], ((0, 0), (0, 0), (0, LANES - NOPE))).reshape(KV_LORA, MLA_H * LANES).astype(BF16)
        wvt = jnp.pad(wkv[:, :, NOPE:], ((0, 0), (0, 0), (0, V_EXT - MLA_V))).reshape(KV_LORA, MLA_H * V_EXT).T.astype(BF16)
        q, k, vt = mla_proj(proj, small, row(q_norm_g[l]), row(kv_norm_g[l]), wq, wqs, wk, wvt, vones, tabs, S)
        y_a = attention(q, k, vt, mla_norm_g[l].reshape(MLA_W, 1), B, S)
        w_router = jnp.concatenate([w_group[l], w_expert[l],
                                    jnp.zeros((D_MODEL, R_LOG - N_GROUPS - N_EXP), F32)], axis=1).T
        wr_hi = w_router.astype(BF16)
        wr_lo = (w_router - wr_hi.astype(F32)).astype(BF16)
        b_router = jnp.concatenate([b_group[l], b_expert[l], jnp.zeros((R_LOG - N_GROUPS - N_EXP,), F32)])
        x_mid, t_tiles, logits = out_proj(x2d, y_m, y_r, y_a, w_out[l].astype(BF16), row(norm2_g[l]),
                                          wr_hi, wr_lo, b_router.reshape(R_LOG, 1))
        meta, cnt_rows = route(logits)
        meta = meta.reshape(T // ST, SUBLANES, ST)
        slot_base = (jnp.arange(T // ST, dtype=jnp.int32) % 2) * SLOT_ROWS
        lrow = (meta[:, 0:2, :].astype(jnp.int32) * TOK_ROWS + slot_base[:, None, None]).reshape(-1)
        w_pick = meta[:, 2:4, :].reshape(-1)
        cnt, lst, gst, pad, tile_cum, n_used = _routing_tables(cnt_rows)
        tile_ids = jnp.minimum(jnp.arange(n_tiles, dtype=jnp.int32), n_used[0] - 1)
        tile_expert = jnp.minimum(jnp.sum(tile_ids[:, None] >= tile_cum[None, :], axis=1),
                                  N_EXP - 1).astype(jnp.int32)
        xs = dispatch(lrow, cnt, lst, gst, pad, t_tiles, n_tiles * TM_E)
        ys = experts(tile_expert, n_used, xs, w_gate, w_up, w_down, l)
        x2d = combine(lrow, w_pick, cnt, lst, gst, ys, x_mid, row(final_norm_g), l == depth - 1)
    return x2d.reshape(B, S, D_MODEL)
```
